```python
import math
import jax
import jax.numpy as jnp
from jax import lax
import numpy as np

D_MODEL = 1024
BATCH = 8
SEQ = 4096
DEPTH = 2

MIX_WIDTH = D_MODEL
N_MIXERS = 4
GROUP_WIDTH = MIX_WIDTH // N_MIXERS
HEADS_PER_GROUP = 4
HEAD_DIM = GROUP_WIDTH // HEADS_PER_GROUP
CONV_WIDTH = 31
MOBA_BLOCK = 256
MOBA_TOPK = 3
Q_CHUNK = 128
N_BUCKETS = 32
T5_MAX_DISTANCE = 128
POOL_WINDOWS = (2, 4, 8, 16)
N_POOL_GROUPS = len(POOL_WINDOWS)
POOL_GROUP_WIDTH = GROUP_WIDTH // N_POOL_GROUPS
SGU_CHUNK = 128
N_EXPERTS = 32
TOP_K = 4
D_FF = D_MODEL
SWIGLU_LIMIT = 7.0
SWIGLU_ALPHA = 1.702
EXPERT_BLOCK = 128
EPS = 1e-6

CONV_IN = 2 * GROUP_WIDTH
ATTN_IN = 3 * GROUP_WIDTH
POOL_IN = GROUP_WIDTH
SGU_IN = 2 * GROUP_WIDTH
CONV_OFF = 0
ATTN_OFF = CONV_OFF + CONV_IN
POOL_OFF = ATTN_OFF + ATTN_IN
SGU_OFF = POOL_OFF + POOL_IN
IN_WIDTH = SGU_OFF + SGU_IN

kernel_name = "hybrid_conv_moba_pool_sgu_moe"


def _rmsnorm(x, g):
    xf = x.astype(jnp.float32)
    y = xf * lax.rsqrt(jnp.mean(xf * xf, axis=-1, keepdims=True) + EPS)
    return (y * g.astype(jnp.float32)).astype(x.dtype)


def _layernorm(x, g, b):
    xf = x.astype(jnp.float32)
    mu = jnp.mean(xf, axis=-1, keepdims=True)
    xc = xf - mu
    y = xc * lax.rsqrt(jnp.mean(xc * xc, axis=-1, keepdims=True) + EPS)
    return (y * g.astype(jnp.float32) + b.astype(jnp.float32)).astype(x.dtype)


def _modulate(h, shift, scale):
    return h * (1 + scale[:, None, :]) + shift[:, None, :]


def conformer_conv(z, conv_w, conv_b, norm_g, norm_b):
    a, gte = jnp.split(z, 2, axis=-1)
    y = a * jax.nn.sigmoid(gte)
    y = lax.conv_general_dilated(
        y, conv_w[:, None, :], window_strides=(1,), padding=[(CONV_WIDTH - 1, 0)],
        dimension_numbers=('NWC', 'WIO', 'NWC'), feature_group_count=GROUP_WIDTH) + conv_b
    B, S, C = y.shape
    y = _layernorm(y.reshape(B, S, HEADS_PER_GROUP, HEAD_DIM),
                   norm_g.reshape(HEADS_PER_GROUP, HEAD_DIM),
                   norm_b.reshape(HEADS_PER_GROUP, HEAD_DIM)).reshape(B, S, C)
    return jax.nn.silu(y)


def _t5_bucket(dist):
    n = jnp.maximum(dist, 0)
    max_exact = N_BUCKETS // 2
    nf = jnp.maximum(n, 1).astype(jnp.float32)
    large = max_exact + (jnp.log(nf / max_exact) / math.log(T5_MAX_DISTANCE / max_exact)
                         * (N_BUCKETS - max_exact)).astype(jnp.int32)
    large = jnp.minimum(large, N_BUCKETS - 1)
    return jnp.where(n < max_exact, n, large)


def moba_attention(q, k, v, rel_bias):
    B, S, H, Dh = q.shape
    q = q.transpose(0, 2, 1, 3)
    k = k.transpose(0, 2, 1, 3)
    v = v.transpose(0, 2, 1, 3)
    n_blocks = -(-S // MOBA_BLOCK)
    s_pad = n_blocks * MOBA_BLOCK
    pad = ((0, 0), (0, 0), (0, s_pad - S), (0, 0))
    k = jnp.pad(k, pad)
    v = jnp.pad(v, pad)
    kb = k.reshape(B, H, n_blocks, MOBA_BLOCK, Dh)
    vb = v.reshape(B, H, n_blocks, MOBA_BLOCK, Dh)
    k_mean = jnp.mean(kb.astype(jnp.float32), axis=3).astype(q.dtype)
    k_sel = min(MOBA_TOPK, n_blocks)
    scale = HEAD_DIM ** -0.5
    bias_table = rel_bias.T.astype(jnp.float32)
    head_idx = jnp.arange(H)
    offs = jnp.arange(MOBA_BLOCK)
    gather_blocks = jax.vmap(jax.vmap(lambda blocks, idx: blocks[idx]))

    def chunk(ci):
        q0 = ci * Q_CHUNK
        qc = lax.dynamic_slice_in_dim(q, q0, Q_CHUNK, axis=2)
        qpos = q0 + jnp.arange(Q_CHUNK)
        own = q0 // MOBA_BLOCK
        gate = jnp.einsum('bhqd,bhnd->bhqn', qc, k_mean).astype(jnp.float32)
        gate = jnp.where(jnp.arange(n_blocks) < own, gate, -jnp.inf)
        _, idx = lax.top_k(gate, k_sel)
        valid = idx < own
        kg = gather_blocks(kb, idx)
        vg = gather_blocks(vb, idx)
        dist_sel = qpos[:, None, None] - (idx[..., None] * MOBA_BLOCK + offs)
        bias_sel = bias_table[head_idx[:, None, None, None], _t5_bucket(dist_sel)]
        s_sel = jnp.einsum('bhqd,bhqnkd->bhqnk', qc, kg).astype(jnp.float32) * scale + bias_sel
        s_sel = jnp.where(valid[..., None], s_sel, -jnp.inf).reshape(B, H, Q_CHUNK, k_sel * MOBA_BLOCK)
        own_start = own * MOBA_BLOCK
        ko = lax.dynamic_slice_in_dim(k, own_start, MOBA_BLOCK, axis=2)
        vo = lax.dynamic_slice_in_dim(v, own_start, MOBA_BLOCK, axis=2)
        dist_own = qpos[:, None] - (own_start + offs)[None, :]
        bias_own = bias_table[:, _t5_bucket(dist_own)]
        s_own = jnp.einsum('bhqd,bhkd->bhqk', qc, ko).astype(jnp.float32) * scale + bias_own
        s_own = jnp.where(dist_own >= 0, s_own, -jnp.inf)
        p = jax.nn.softmax(jnp.concatenate([s_sel, s_own], axis=-1), axis=-1).astype(v.dtype)
        p_sel = p[..., :k_sel * MOBA_BLOCK].reshape(B, H, Q_CHUNK, k_sel, MOBA_BLOCK)
        p_own = p[..., k_sel * MOBA_BLOCK:]
        return (jnp.einsum('bhqnk,bhqnkd->bhqd', p_sel, vg)
                + jnp.einsum('bhqk,bhkd->bhqd', p_own, vo))

    out = lax.map(chunk, jnp.arange(S // Q_CHUNK))
    return out.transpose(1, 0, 3, 2, 4).reshape(B, S, H * Dh)


def multiscale_pool(z, pool_w, pool_scale):
    B, S, C = z.shape
    zg = z.reshape(B, S, N_POOL_GROUPS, POOL_GROUP_WIDTH).astype(jnp.float32)
    csum = jnp.cumsum(zg, axis=1)
    t = jnp.arange(1, S + 1, dtype=jnp.float32)
    pooled = []
    for gi, w in enumerate(POOL_WINDOWS):
        cs = csum[:, :, gi]
        lag = jnp.pad(cs, ((0, 0), (w, 0), (0, 0)))[:, :S]
        pooled.append((cs - lag) / jnp.minimum(t, w)[None, :, None])
    pooled = jnp.stack(pooled, axis=2) - zg
    y = jnp.einsum('bsgc,gcd->bsgd', pooled.astype(z.dtype), pool_w)
    return y.reshape(B, S, C) * pool_scale


def spatial_gating(z, norm_g, norm_b, sgu_w, sgu_b):
    z = jax.nn.gelu(z, approximate=False)
    u, v = jnp.split(z, 2, axis=-1)
    v = _layernorm(v, norm_g, norm_b)
    B, S, C = v.shape
    vc = v.reshape(B, S // SGU_CHUNK, SGU_CHUNK, HEADS_PER_GROUP, HEAD_DIM)
    w = jnp.tril(sgu_w)
    mixed = jnp.einsum('hij,bnjhc->bnihc', w, vc) + sgu_b.T[None, None, :, :, None]
    return u * mixed.reshape(B, S, C)


def _clamped_swiglu(xb, w1, b1, w2, b2):
    hh = xb @ w1 + b1
    x_glu, x_lin = jnp.split(hh, 2, axis=-1)
    x_glu = jnp.minimum(x_glu, SWIGLU_LIMIT)
    x_lin = jnp.clip(x_lin, -SWIGLU_LIMIT, SWIGLU_LIMIT)
    act = x_glu * jax.nn.sigmoid(SWIGLU_ALPHA * x_glu) * (x_lin + 1)
    return act @ w2 + b2


def moe_ffn(h, router_w, router_b, w1, b1, w2, b2):
    B, S, D = h.shape
    N = B * S
    hf = h.reshape(N, D)
    logits = (hf @ router_w + router_b).astype(jnp.float32)
    top_vals, top_idx = lax.top_k(logits, TOP_K)
    gates = jax.nn.softmax(top_vals, axis=-1)
    A = N * TOP_K
    e_flat = top_idx.reshape(A)
    tok_flat = jnp.arange(A) // TOP_K
    order = jnp.argsort(e_flat)
    e_sorted = e_flat[order]
    tok_sorted = tok_flat[order]
    g_sorted = gates.reshape(A)[order]
    sizes = jnp.bincount(e_flat, length=N_EXPERTS)
    padded = ((sizes + EXPERT_BLOCK - 1) // EXPERT_BLOCK) * EXPERT_BLOCK
    pad_end = jnp.cumsum(padded)
    pad_start = pad_end - padded
    grp_start = jnp.cumsum(sizes) - sizes
    dest = pad_start[e_sorted] + jnp.arange(A) - grp_start[e_sorted]
    P = A + N_EXPERTS * EXPERT_BLOCK
    n_blk = P // EXPERT_BLOCK
    row_token = jnp.full((P,), N, dtype=jnp.int32).at[dest].set(tok_sorted.astype(jnp.int32))
    block_expert = jnp.minimum(
        jnp.searchsorted(pad_end, jnp.arange(n_blk) * EXPERT_BLOCK, side='right'), N_EXPERTS - 1)
    x_pad = jnp.concatenate([hf, jnp.zeros((1, D), hf.dtype)], axis=0)
    xs = x_pad[row_token].reshape(n_blk, EXPERT_BLOCK, D)

    def expert_block(args):
        xb, e = args
        return _clamped_swiglu(xb, w1[e], b1[e], w2[e], b2[e])

    ys = lax.map(expert_block, (xs, block_expert)).reshape(P, D)
    y_assign = ys[dest] * g_sorted[:, None].astype(ys.dtype)
    out = jax.ops.segment_sum(y_assign, tok_sorted, num_segments=N)
    return out.reshape(B, S, D).astype(h.dtype)


def setup_inputs(seed: int = 0) -> dict:
    key = jax.random.key(seed)
    ks = jax.random.split(key, 28)
    f32 = jnp.float32

    def nrm(k, shape, scale):
        return jax.random.normal(k, shape, f32) * scale

    L, D, GW, H = DEPTH, D_MODEL, GROUP_WIDTH, HEADS_PER_GROUP
    return {
        'x': nrm(ks[0], (BATCH, SEQ, D), 1.0),
        'c': nrm(ks[1], (BATCH, D), 1.0),
        'w_mod': nrm(ks[2], (L, D, 6 * D), 0.5 * D ** -0.5),
        'b_mod': nrm(ks[3], (L, 6 * D), 0.02),
        'norm1_g': 1.0 + nrm(ks[4], (L, D), 0.05),
        'w_in': nrm(ks[5], (L, D, IN_WIDTH), D ** -0.5),
        'conv_w': nrm(ks[6], (L, CONV_WIDTH, GW), CONV_WIDTH ** -0.5),
        'conv_b': nrm(ks[7], (L, GW), 0.02),
        'conv_norm_g': 1.0 + nrm(ks[8], (L, GW), 0.05),
        'conv_norm_b': nrm(ks[9], (L, GW), 0.02),
        'rel_bias': nrm(ks[10], (N_BUCKETS, H), 0.3),
        'pool_w': nrm(ks[11], (L, N_POOL_GROUPS, POOL_GROUP_WIDTH, POOL_GROUP_WIDTH), POOL_GROUP_WIDTH ** -0.5),
        'pool_scale': 1.0 + nrm(ks[12], (L, GW), 0.05),
        'sgu_norm_g': 1.0 + nrm(ks[13], (L, GW), 0.05),
        'sgu_norm_b': nrm(ks[14], (L, GW), 0.02),
        'sgu_w': nrm(ks[15], (L, H, SGU_CHUNK, SGU_CHUNK), SGU_CHUNK ** -0.5),
        'sgu_b': 1.0 + nrm(ks[16], (L, H, SGU_CHUNK), 0.05),
        'mix_out_g': 1.0 + nrm(ks[17], (L, MIX_WIDTH), 0.05),
        'w_out': nrm(ks[18], (L, MIX_WIDTH, D), MIX_WIDTH ** -0.5),
        'norm2_g': 1.0 + nrm(ks[19], (L, D), 0.05),
        'router_w': nrm(ks[20], (L, D, N_EXPERTS), D ** -0.5),
        'router_b': nrm(ks[21], (L, N_EXPERTS), 0.01),
        'exp_w1': nrm(ks[22], (L, N_EXPERTS, D, 2 * D_FF), D ** -0.5),
        'exp_b1': nrm(ks[23], (L, N_EXPERTS, 2 * D_FF), 0.02),
        'exp_w2': nrm(ks[24], (L, N_EXPERTS, D_FF, D), D_FF ** -0.5),
        'exp_b2': nrm(ks[25], (L, N_EXPERTS, D), 0.02),
        'final_norm_g': 1.0 + nrm(ks[26], (D,), 0.05),
    }


def reference(x, c, w_mod, b_mod, norm1_g, w_in, conv_w, conv_b, conv_norm_g, conv_norm_b,
              rel_bias, pool_w, pool_scale, sgu_norm_g, sgu_norm_b, sgu_w, sgu_b, mix_out_g,
              w_out, norm2_g, router_w, router_b, exp_w1, exp_b1, exp_w2, exp_b2, final_norm_g):
    B, S, D = x.shape
    cond = jax.nn.silu(c)
    for l in range(DEPTH):
        mod = cond @ w_mod[l] + b_mod[l]
        shift1, scale1, gate1, shift2, scale2, gate2 = jnp.split(mod, 6, axis=-1)
        h = _modulate(_rmsnorm(x, norm1_g[l]), shift1, scale1)
        proj = h @ w_in[l]
        y_conv = conformer_conv(proj[..., CONV_OFF:CONV_OFF + CONV_IN],
                                conv_w[l], conv_b[l], conv_norm_g[l], conv_norm_b[l])
        qkv = proj[..., ATTN_OFF:ATTN_OFF + ATTN_IN].reshape(B, S, 3, HEADS_PER_GROUP, HEAD_DIM)
        y_attn = moba_attention(qkv[:, :, 0], qkv[:, :, 1], qkv[:, :, 2], rel_bias)
        y_pool = multiscale_pool(proj[..., POOL_OFF:POOL_OFF + POOL_IN], pool_w[l], pool_scale[l])
        y_sgu = spatial_gating(proj[..., SGU_OFF:SGU_OFF + SGU_IN],
                               sgu_norm_g[l], sgu_norm_b[l], sgu_w[l], sgu_b[l])
        mixed = jnp.concatenate([y_conv, y_attn, y_pool, y_sgu], axis=-1)
        mixed = _rmsnorm(mixed.reshape(B, S, N_MIXERS, GROUP_WIDTH),
                         mix_out_g[l].reshape(N_MIXERS, GROUP_WIDTH)).reshape(B, S, MIX_WIDTH)
        x = x + gate1[:, None, :] * (mixed @ w_out[l])
        h = _modulate(_rmsnorm(x, norm2_g[l]), shift2, scale2)
        x = x + gate2[:, None, :] * moe_ffn(h, router_w[l], router_b[l],
                                            exp_w1[l], exp_b1[l], exp_w2[l], exp_b2[l])
    return _rmsnorm(x, final_norm_g)
```

```python
import functools
import math

import numpy as np
import jax
import jax.numpy as jnp
from jax import lax
from jax.experimental import pallas as pl
from jax.experimental.pallas import tpu as pltpu

f32, bf16, i32 = jnp.float32, jnp.bfloat16, jnp.int32

GROUP_WIDTH = 256
HEADS = 4
HEAD_DIM = 64
CONV_WIDTH = 31
MOBA_BLOCK = 256
MOBA_TOPK = 3
Q_CHUNK = 128
N_BUCKETS = 32
T5_MAX_DISTANCE = 128
POOL_WINDOWS = (2, 4, 8, 16)
SGU_CHUNK = 128
N_EXPERTS = 32
TOP_K = 4
SWIGLU_LIMIT = 7.0
SWIGLU_ALPHA = 1.702
EPS = 1e-6

HALO = 32
SEQ_TILE = 512
ROUTE_TILE = 256
ROW_CHUNK = 8
FFN_TILE = 512
NEG = -1e30
VMEM_LIMIT = 56 * 1024 * 1024


def _cparams(sem):
    return pltpu.CompilerParams(dimension_semantics=sem, vmem_limit_bytes=VMEM_LIMIT)


def _split_bf16(a):
    hi = a.astype(bf16)
    lo = (a - hi.astype(f32)).astype(bf16)
    return hi, lo


def _mod_kernel(c_ref, w_ref, b_ref, o_ref):
    c = c_ref[...]
    cond = c * jax.nn.sigmoid(c)
    o_ref[0] = jnp.dot(cond, w_ref[0], preferred_element_type=f32,
                       precision=lax.Precision.HIGHEST) + b_ref[0]


def _modulation(c, w_mod, b_mod):
    L, D, M = w_mod.shape
    B = c.shape[0]
    tn = 1536
    return pl.pallas_call(
        _mod_kernel,
        grid=(L, M // tn),
        in_specs=[pl.BlockSpec((B, D), lambda l, j: (0, 0)),
                  pl.BlockSpec((1, D, tn), lambda l, j: (l, 0, j)),
                  pl.BlockSpec((1, 1, tn), lambda l, j: (l, 0, j))],
        out_specs=pl.BlockSpec((1, B, tn), lambda l, j: (l, 0, j)),
        out_shape=jax.ShapeDtypeStruct((L, B, M), f32),
        compiler_params=_cparams(("arbitrary", "arbitrary")),
        name="modulation",
    )(c, w_mod, b_mod.reshape(L, 1, M))


def _inproj_kernel(x_ref, sh_ref, sc_ref, g_ref, w_ref,
                   pc_ref, qT_ref, k_ref, vT_ref, km_ref, pp_ref, ps_ref):
    x = x_ref[0]
    ms = jnp.mean(x * x, axis=-1, keepdims=True)
    h = x * lax.rsqrt(ms + EPS) * g_ref[...]
    h = h * (1.0 + sc_ref[0]) + sh_ref[0]
    proj = jnp.dot(h.astype(bf16), w_ref[...], preferred_element_type=f32)
    gw = GROUP_WIDTH
    pc_ref[0] = proj[:, 0:2 * gw]
    q = proj[:, 2 * gw:3 * gw] * (HEAD_DIM ** -0.5)
    qT_ref[0] = q.T.astype(bf16)
    kk = proj[:, 3 * gw:4 * gw]
    k_ref[0] = kk.astype(bf16)
    for j in range(SEQ_TILE // MOBA_BLOCK):
        km_ref[0, 0, j:j + 1, :] = jnp.mean(kk[j * MOBA_BLOCK:(j + 1) * MOBA_BLOCK], axis=0, keepdims=True)
    for j in range(SEQ_TILE // MOBA_BLOCK):
        vT_ref[0, j] = proj[j * MOBA_BLOCK:(j + 1) * MOBA_BLOCK, 4 * gw:5 * gw].T.astype(bf16)
    pp_ref[0] = proj[:, 5 * gw:6 * gw]
    ps_ref[0] = proj[:, 6 * gw:8 * gw]


def _in_projection(x, shift, scale, g, w_bf):
    B, S, D = x.shape
    gw = GROUP_WIDTH
    ts = SEQ_TILE
    nt = S // ts
    row = lambda b, i: (b, i, 0)
    col = lambda b, i: (b, 0, i)
    vec = lambda b, i: (b, 0, 0)
    return pl.pallas_call(
        _inproj_kernel,
        grid=(B, nt),
        in_specs=[pl.BlockSpec((1, ts, D), row),
                  pl.BlockSpec((1, 1, D), vec),
                  pl.BlockSpec((1, 1, D), vec),
                  pl.BlockSpec((1, D), lambda b, i: (0, 0)),
                  pl.BlockSpec(w_bf.shape, lambda b, i: (0, 0))],
        out_specs=[pl.BlockSpec((1, ts, 2 * gw), row),
                   pl.BlockSpec((1, gw, ts), col),
                   pl.BlockSpec((1, ts, gw), row),
                   pl.BlockSpec((1, ts // MOBA_BLOCK, gw, MOBA_BLOCK), lambda b, i: (b, i, 0, 0)),
                   pl.BlockSpec((1, 1, ts // MOBA_BLOCK, gw), lambda b, i: (b, i, 0, 0)),
                   pl.BlockSpec((1, ts, gw), row),
                   pl.BlockSpec((1, ts, 2 * gw), row)],
        out_shape=[jax.ShapeDtypeStruct((B, S, 2 * gw), f32),
                   jax.ShapeDtypeStruct((B, gw, S), bf16),
                   jax.ShapeDtypeStruct((B, S, gw), bf16),
                   jax.ShapeDtypeStruct((B, S // MOBA_BLOCK, gw, MOBA_BLOCK), bf16),
                   jax.ShapeDtypeStruct((B, nt, ts // MOBA_BLOCK, gw), f32),
                   jax.ShapeDtypeStruct((B, S, gw), f32),
                   jax.ShapeDtypeStruct((B, S, 2 * gw), f32)],
        compiler_params=_cparams(("arbitrary", "arbitrary")),
        name="in_projection",
    )(x, shift, scale, g, w_bf)


def _group_rms(y, g):
    return y * lax.rsqrt(jnp.mean(y * y, axis=-1, keepdims=True) + EPS) * g


def _local_kernel(pc_ref, pcp_ref, pp_ref, ppp_ref, ps_ref,
                  cw_ref, cb_ref, cng_ref, cnb_ref, pw_ref, psc_ref,
                  sg_ref, sb_ref, sw_ref, sbias_ref, mg_ref,
                  yc_ref, yp_ref, ys_ref,
                  gext, zext, s2, s4, s8):
    i = pl.program_id(1)
    ts = SEQ_TILE
    gw = GROUP_WIDTH
    first = i == 0
    lane = lax.broadcasted_iota(i32, (1, gw), 1)

    pc = pc_ref[0]
    g = pc[:, :gw] * jax.nn.sigmoid(pc[:, gw:])
    ph = pcp_ref[0]
    gh = ph[:, :gw] * jax.nn.sigmoid(ph[:, gw:])
    gext[0:HALO, :] = jnp.where(first, 0.0, gh)
    gext[HALO:HALO + ts, :] = g
    acc = jnp.zeros((ts, gw), f32)
    for j in range(CONV_WIDTH):
        acc = acc + cw_ref[j:j + 1, :] * gext[pl.ds(HALO - (CONV_WIDTH - 1) + j, ts), :]
    y = acc + cb_ref[...]
    r = lax.broadcasted_iota(i32, (gw, gw), 0) // HEAD_DIM
    c = lax.broadcasted_iota(i32, (gw, gw), 1) // HEAD_DIM
    avg = jnp.where(r == c, 1.0 / HEAD_DIM, 0.0).astype(bf16)

    def head_mean(t):
        hi, lo = _split_bf16(t)
        return (jnp.dot(hi, avg, preferred_element_type=f32)
                + jnp.dot(lo, avg, preferred_element_type=f32))

    mu = head_mean(y)
    yc = y - mu
    var = head_mean(yc * yc)
    yn = yc * lax.rsqrt(var + EPS) * cng_ref[...] + cnb_ref[...]
    yconv = yn * jax.nn.sigmoid(yn)
    yc_ref[0] = _group_rms(yconv, mg_ref[:, 0:gw]).astype(bf16)

    z = pp_ref[0]
    zext[0:HALO, :] = jnp.where(first, 0.0, ppp_ref[0])
    zext[HALO:HALO + ts, :] = z
    n2, n4, n8 = ts + 14, ts + 12, ts + 8
    s2[0:n2, :] = zext[pl.ds(HALO - 14, n2), :] + zext[pl.ds(HALO - 15, n2), :]
    s4[0:n4, :] = s2[pl.ds(2, n4), :] + s2[pl.ds(0, n4), :]
    s8[0:n8, :] = s4[pl.ds(4, n8), :] + s4[pl.ds(0, n8), :]
    w2 = s2[pl.ds(14, ts), :]
    w4 = s4[pl.ds(12, ts), :]
    w8 = s8[pl.ds(8, ts), :]
    w16 = w8 + s8[pl.ds(0, ts), :]
    tpos = (i * ts + lax.broadcasted_iota(i32, (ts, 1), 0) + 1).astype(f32)
    grp = lane // (gw // len(POOL_WINDOWS))
    pooled = jnp.zeros((ts, gw), f32)
    for gi, (w, sw) in enumerate(zip(POOL_WINDOWS, (w2, w4, w8, w16))):
        pooled = jnp.where(grp == gi, sw / jnp.minimum(tpos, float(w)), pooled)
    pooled = pooled - z
    yp = jnp.dot(pooled.astype(bf16), pw_ref[...], preferred_element_type=f32) * psc_ref[...]
    yp_ref[0] = _group_rms(yp, mg_ref[:, 2 * gw:3 * gw]).astype(bf16)

    zz = ps_ref[0]
    zz = 0.5 * zz * (1.0 + lax.erf(zz * (1.0 / math.sqrt(2.0))))
    u = zz[:, :gw]
    v = zz[:, gw:]
    vm = jnp.mean(v, axis=-1, keepdims=True)
    vc = v - vm
    vv = jnp.mean(vc * vc, axis=-1, keepdims=True)
    vn = (vc * lax.rsqrt(vv + EPS) * sg_ref[...] + sb_ref[...]).astype(bf16)
    li = lax.broadcasted_iota(i32, (SGU_CHUNK, SGU_CHUNK), 0)
    lj = lax.broadcasted_iota(i32, (SGU_CHUNK, SGU_CHUNK), 1)
    head_of_lane = lane // HEAD_DIM
    wts = [jnp.where(li >= lj, sw_ref[h], 0.0).astype(bf16) for h in range(HEADS)]
    outs = []
    for n in range(ts // SGU_CHUNK):
        vch = vn[n * SGU_CHUNK:(n + 1) * SGU_CHUNK]
        mixed = sbias_ref[...]
        for h in range(HEADS):
            mh = jnp.dot(wts[h], vch, preferred_element_type=f32)
            mixed = mixed + jnp.where(head_of_lane == h, mh, 0.0)
        outs.append(u[n * SGU_CHUNK:(n + 1) * SGU_CHUNK] * mixed)
    ysgu = jnp.concatenate(outs, axis=0)
    ys_ref[0] = _group_rms(ysgu, mg_ref[:, 3 * gw:4 * gw]).astype(bf16)


def _local_mixers(pc, pp, ps, cw, cb, cng, cnb, pw_bd, psc, sg, sb, sw, sbias, mg):
    B, S, _ = pc.shape
    gw = GROUP_WIDTH
    ts = SEQ_TILE
    hb = ts // HALO
    row = lambda b, i: (b, i, 0)
    prev = lambda b, i: (b, jnp.maximum(i * hb - 1, 0), 0)
    full2 = lambda b, i: (0, 0)
    full3 = lambda b, i: (0, 0, 0)
    out = jax.ShapeDtypeStruct((B, S, gw), bf16)
    return pl.pallas_call(
        _local_kernel,
        grid=(B, S // ts),
        in_specs=[pl.BlockSpec((1, ts, 2 * gw), row),
                  pl.BlockSpec((1, HALO, 2 * gw), prev),
                  pl.BlockSpec((1, ts, gw), row),
                  pl.BlockSpec((1, HALO, gw), prev),
                  pl.BlockSpec((1, ts, 2 * gw), row),
                  pl.BlockSpec(cw.shape, full2), pl.BlockSpec(cb.shape, full2),
                  pl.BlockSpec(cng.shape, full2), pl.BlockSpec(cnb.shape, full2),
                  pl.BlockSpec(pw_bd.shape, full2), pl.BlockSpec(psc.shape, full2),
                  pl.BlockSpec(sg.shape, full2), pl.BlockSpec(sb.shape, full2),
                  pl.BlockSpec(sw.shape, full3), pl.BlockSpec(sbias.shape, full2),
                  pl.BlockSpec(mg.shape, full2)],
        out_specs=[pl.BlockSpec((1, ts, gw), row)] * 3,
        out_shape=[out, out, out],
        scratch_shapes=[pltpu.VMEM((ts + HALO, gw), f32), pltpu.VMEM((ts + HALO, gw), f32),
                        pltpu.VMEM((ts + 16, gw), f32), pltpu.VMEM((ts + 16, gw), f32),
                        pltpu.VMEM((ts + 16, gw), f32)],
        compiler_params=_cparams(("arbitrary", "arbitrary")),
        name="local_mixers",
    )(pc, pc, pp, pp, ps, cw, cb, cng, cnb, pw_bd, psc, sg, sb, sw, sbias, mg)


def _t5_bucket_table(max_dist):
    d = np.arange(max_dist, dtype=np.int64)
    max_exact = N_BUCKETS // 2
    nf = np.maximum(d, 1).astype(np.float32)
    large = max_exact + (np.log(nf / np.float32(max_exact)) / np.float32(math.log(T5_MAX_DISTANCE / max_exact))
                         * np.float32(N_BUCKETS - max_exact)).astype(np.int32)
    large = np.minimum(large, N_BUCKETS - 1)
    return np.where(d < max_exact, d, large).astype(np.int32)


def _bias_tiles(rel_bias):
    blk, qc = MOBA_BLOCK, Q_CHUNK
    table = _t5_bucket_table(2 * blk + qc)
    by_dist = rel_bias.astype(f32)[table]
    j = np.arange(blk)[:, None]
    q = np.arange(qc)[None, :]
    tiles = []
    for base in (0, qc, blk, blk + qc):
        d = base + q - j
        t = by_dist[np.clip(d, 0, None)]
        t = jnp.where((d >= 0)[..., None], t, NEG)
        tiles.append(jnp.transpose(t, (0, 2, 1)).reshape(blk, HEADS * qc))
    return jnp.stack(tiles)


def _attn_kernel(qT_ref, k_ref, vT_ref, km_ref, bias_ref, mg_ref, o_ref, mask_ref):
    ci = pl.program_id(1)
    blk, qc, gw = MOBA_BLOCK, Q_CHUNK, GROUP_WIDTH
    per = blk // qc
    own = ci // per
    off = ci % per
    nb = km_ref.shape[1]
    width = HEADS * qc

    qT = qT_ref[0]
    ch_head = lax.broadcasted_iota(i32, (gw, width), 0) // HEAD_DIM
    col_head = lax.broadcasted_iota(i32, (gw, width), 1) // qc
    qbd = jnp.where(ch_head == col_head, jnp.concatenate([qT] * HEADS, axis=1), jnp.zeros((), bf16))

    km_hi, km_lo = _split_bf16(km_ref[0])
    gate = (jnp.dot(km_hi, qbd, preferred_element_type=f32)
            + jnp.dot(km_lo, qbd, preferred_element_type=f32))
    nio = lax.broadcasted_iota(i32, (nb, width), 0)
    past = nio < own
    gate = jnp.where(past, gate, -jnp.inf)
    rank = jnp.zeros((nb, width), i32)
    for m in range(nb):
        gm = gate[m:m + 1, :]
        beats = (gm > gate) | ((gm == gate) & (m < nio))
        rank = rank + beats.astype(i32)
    sel = (rank < MOBA_TOPK) & past
    mask_ref[...] = jnp.where(sel, 0.0, NEG)

    def scores(n, tile):
        kb = k_ref[0, pl.ds(pl.multiple_of(n * blk, blk), blk), :]
        return jnp.dot(kb, qbd, preferred_element_type=f32) + bias_ref[tile]

    def weighted_values(n, p):
        vb = vT_ref[0, n]
        res = jnp.dot(vb, p.astype(bf16), preferred_element_type=f32)
        return jnp.concatenate(
            [res[h * HEAD_DIM:(h + 1) * HEAD_DIM, h * qc:(h + 1) * qc] for h in range(HEADS)], axis=0)

    def per_channel(rowvec):
        return jnp.concatenate(
            [jnp.broadcast_to(rowvec[:, h * qc:(h + 1) * qc], (HEAD_DIM, qc)) for h in range(HEADS)], axis=0)

    s = scores(own, off)
    m0 = jnp.max(s, axis=0, keepdims=True)
    p = jnp.exp(s - m0)
    l0 = jnp.sum(p, axis=0, keepdims=True)
    acc0 = weighted_values(own, p)

    def body(n, carry):
        m, l, acc = carry
        tile = jnp.where(n == own - 1, per + off, 2 * per - 1)
        s = scores(n, tile) + mask_ref[pl.ds(n, 1), :]
        m_new = jnp.maximum(m, jnp.max(s, axis=0, keepdims=True))
        alpha = jnp.exp(m - m_new)
        p = jnp.exp(s - m_new)
        l = alpha * l + jnp.sum(p, axis=0, keepdims=True)
        acc = acc * per_channel(alpha) + weighted_values(n, p)
        return m_new, l, acc

    m, l, acc = lax.fori_loop(0, own, body, (m0, l0, acc0))
    outT = acc / per_channel(l)
    o_ref[0] = _group_rms(outT.T, mg_ref[:, gw:2 * gw]).astype(bf16)


def _moba_attention(qT, k, vT, kmean, bias_tiles, mg):
    B, S, gw = k.shape
    nq = S // Q_CHUNK
    nb = kmean.shape[1]
    return pl.pallas_call(
        _attn_kernel,
        grid=(B, nq),
        in_specs=[pl.BlockSpec((1, gw, Q_CHUNK), lambda b, c: (b, 0, c)),
                  pl.BlockSpec((1, S, gw), lambda b, c: (b, 0, 0)),
                  pl.BlockSpec((1, nb, gw, MOBA_BLOCK), lambda b, c: (b, 0, 0, 0)),
                  pl.BlockSpec((1, nb, gw), lambda b, c: (b, 0, 0)),
                  pl.BlockSpec(bias_tiles.shape, lambda b, c: (0, 0, 0)),
                  pl.BlockSpec(mg.shape, lambda b, c: (0, 0))],
        out_specs=pl.BlockSpec((1, Q_CHUNK, gw), lambda b, c: (b, c, 0)),
        out_shape=jax.ShapeDtypeStruct((B, S, gw), bf16),
        scratch_shapes=[pltpu.VMEM((nb, HEADS * Q_CHUNK), f32)],
        compiler_params=_cparams(("arbitrary", "arbitrary")),
        name="moba_attention",
    )(qT, k, vT, kmean, bias_tiles, mg)


def _outproj_router_kernel(yc_ref, ya_ref, yp_ref, ys_ref, x_ref, g1_ref, wo_ref, n2_ref, sh_ref, sc_ref,
                           rw_ref, rb_ref,
                           x1_ref, h2_ref, idx_ref, gate_ref, rank_ref, cnt_ref):
    gw = GROUP_WIDTH
    W = ROUTE_TILE
    acc = jnp.dot(yc_ref[...], wo_ref[0:gw, :], preferred_element_type=f32)
    acc = acc + jnp.dot(ya_ref[...], wo_ref[gw:2 * gw, :], preferred_element_type=f32)
    acc = acc + jnp.dot(yp_ref[...], wo_ref[2 * gw:3 * gw, :], preferred_element_type=f32)
    acc = acc + jnp.dot(ys_ref[...], wo_ref[3 * gw:4 * gw, :], preferred_element_type=f32)
    x1 = x_ref[...] + g1_ref[0] * acc
    x1_ref[...] = x1
    ms = jnp.mean(x1 * x1, axis=-1, keepdims=True)
    h = x1 * lax.rsqrt(ms + EPS) * n2_ref[...]
    h = h * (1.0 + sc_ref[0]) + sh_ref[0]
    h_hi, h_lo = _split_bf16(h)
    h2_ref[...] = h_hi

    nt = (((1,), (1,)), ((), ()))
    rw_hi, rw_lo = _split_bf16(rw_ref[...])
    logits = (lax.dot_general(rw_hi, h_hi, nt, preferred_element_type=f32)
              + lax.dot_general(rw_hi, h_lo, nt, preferred_element_type=f32)
              + lax.dot_general(rw_lo, h_hi, nt, preferred_element_type=f32)) + rb_ref[...]
    eio = lax.broadcasted_iota(i32, (N_EXPERTS, W), 0)
    work = logits
    vals, sels = [], []
    for k in range(TOP_K):
        m = jnp.max(work, axis=0, keepdims=True)
        idx = jnp.min(jnp.where(work == m, eio, N_EXPERTS), axis=0, keepdims=True)
        sel = eio == idx
        idx_ref[k:k + 1, :] = idx
        vals.append(m)
        sels.append(sel)
        work = jnp.where(sel, -jnp.inf, work)
    exps = [jnp.exp(v - vals[0]) for v in vals]
    denom = exps[0] + exps[1] + exps[2] + exps[3]
    for k in range(TOP_K):
        gate_ref[k:k + 1, :] = exps[k] / denom

    multi = jnp.zeros((N_EXPERTS, W), f32)
    for sel in sels:
        multi = multi + sel.astype(f32)
    before = (lax.broadcasted_iota(i32, (W, W), 0) < lax.broadcasted_iota(i32, (W, W), 1)).astype(bf16)
    earlier = jnp.dot(multi.astype(bf16), before, preferred_element_type=f32)
    for k in range(TOP_K):
        rank_ref[k:k + 1, :] = jnp.sum(jnp.where(sels[k], earlier, 0.0), axis=0, keepdims=True).astype(i32)
    cnt_ref[0] = jnp.sum(multi, axis=1, keepdims=True).astype(i32)


def _outproj_router(yc, ya, yp, ys, x, gate1, wo_bf, n2g, shift2, scale2, rwT, rb, seq):
    N, D = x.shape
    gw = GROUP_WIDTH
    W = ROUTE_TILE
    nw = N // W
    per_b = seq // W
    row = lambda i: (i, 0)
    vec = lambda i: (i // per_b, 0, 0)
    full = lambda i: (0, 0)
    colblk = lambda i: (0, i)
    return pl.pallas_call(
        _outproj_router_kernel,
        grid=(nw,),
        in_specs=[pl.BlockSpec((W, gw), row)] * 4 + [
            pl.BlockSpec((W, D), row),
            pl.BlockSpec((1, 1, D), vec),
            pl.BlockSpec(wo_bf.shape, full),
            pl.BlockSpec((1, D), full),
            pl.BlockSpec((1, 1, D), vec),
            pl.BlockSpec((1, 1, D), vec),
            pl.BlockSpec(rwT.shape, full),
            pl.BlockSpec(rb.shape, full)],
        out_specs=[pl.BlockSpec((W, D), row),
                   pl.BlockSpec((W, D), row),
                   pl.BlockSpec((TOP_K, W), colblk),
                   pl.BlockSpec((TOP_K, W), colblk),
                   pl.BlockSpec((TOP_K, W), colblk),
                   pl.BlockSpec((1, N_EXPERTS, 1), lambda i: (i, 0, 0))],
        out_shape=[jax.ShapeDtypeStruct((N, D), f32),
                   jax.ShapeDtypeStruct((N, D), bf16),
                   jax.ShapeDtypeStruct((TOP_K, N), i32),
                   jax.ShapeDtypeStruct((TOP_K, N), f32),
                   jax.ShapeDtypeStruct((TOP_K, N), i32),
                   jax.ShapeDtypeStruct((nw, N_EXPERTS, 1), i32)],
        compiler_params=_cparams(("arbitrary",)),
        name="outproj_router",
    )(yc, ya, yp, ys, x, gate1, wo_bf, n2g, shift2, scale2, rwT, rb)


def _max_window_rows():
    return -(-(ROUTE_TILE * TOP_K + N_EXPERTS * (ROW_CHUNK - 1)) // 128) * 128


def _dispatch_kernel(lstart_s, nchunk_s, gbase_s, tail_start_s, tail_chunks_s, n_used_s,
                     h_ref, dest_ref, xs_ref, sorted_ref, zero_ref, sem):
    w = pl.program_id(0)
    W = ROUTE_TILE
    R = sorted_ref.shape[0]
    rio = lax.broadcasted_iota(i32, (R, W), 0)
    hit = rio == dest_ref[0:1, :]
    for k in range(1, TOP_K):
        hit = hit | (rio == dest_ref[k:k + 1, :])
    onehot = jnp.where(hit, 1.0, 0.0).astype(bf16)
    sorted_ref[...] = jnp.dot(onehot, h_ref[...], preferred_element_type=f32)

    def chunk_copy(src_row, dst_row):
        return pltpu.make_async_copy(
            sorted_ref.at[pl.ds(pl.multiple_of(src_row, ROW_CHUNK), ROW_CHUNK), :],
            xs_ref.at[pl.ds(pl.multiple_of(dst_row, ROW_CHUNK), ROW_CHUNK), :], sem)

    def zero_copy(dst_row):
        return pltpu.make_async_copy(
            zero_ref.at[0:ROW_CHUNK, :],
            xs_ref.at[pl.ds(pl.multiple_of(dst_row, ROW_CHUNK), ROW_CHUNK), :], sem)

    def zero_tile_copy(tile):
        return pltpu.make_async_copy(
            zero_ref, xs_ref.at[pl.ds(pl.multiple_of(tile * FFN_TILE, FFN_TILE), FFN_TILE), :], sem)

    def per_expert(e, total):
        n = nchunk_s[w * N_EXPERTS + e]
        src = lstart_s[w * N_EXPERTS + e]
        dst = gbase_s[w * N_EXPERTS + e]

        def issue(j, c):
            chunk_copy(src + j * ROW_CHUNK, dst + j * ROW_CHUNK).start()
            return c
        lax.fori_loop(0, n, issue, 0)
        return total + n

    total = lax.fori_loop(0, N_EXPERTS, per_expert, 0)

    def drain(j, c):
        chunk_copy(0, 0).wait()
        return c
    lax.fori_loop(0, total, drain, 0)

    @pl.when(w == pl.num_programs(0) - 1)
    def _():
        zero_ref[...] = jnp.zeros(zero_ref.shape, f32)

        def per_tail(e, total):
            n = tail_chunks_s[e]
            dst = tail_start_s[e]

            def issue(j, c):
                zero_copy(dst + j * ROW_CHUNK).start()
                return c
            lax.fori_loop(0, n, issue, 0)
            return total + n
        tails = lax.fori_loop(0, N_EXPERTS, per_tail, 0)

        def drain_tail(j, c):
            zero_copy(0).wait()
            return c
        lax.fori_loop(0, tails, drain_tail, 0)

        n_tiles = xs_ref.shape[0] // FFN_TILE

        def issue_tile(j, c):
            zero_tile_copy(j).start()
            return c
        lax.fori_loop(n_used_s[0], n_tiles, issue_tile, 0)

        def drain_tile(j, c):
            zero_tile_copy(0).wait()
            return c
        lax.fori_loop(n_used_s[0], n_tiles, drain_tile, 0)


def _dispatch(h2, destT, lstart, nchunk, gbase, tail_start, tail_chunks, n_used, p_rows):
    N, D = h2.shape
    W = ROUTE_TILE
    nw = N // W
    R = _max_window_rows()
    grid_spec = pltpu.PrefetchScalarGridSpec(
        num_scalar_prefetch=6,
        grid=(nw,),
        in_specs=[pl.BlockSpec((W, D), lambda w, *_: (w, 0)),
                  pl.BlockSpec((TOP_K, W), lambda w, *_: (0, w))],
        out_specs=pl.BlockSpec(memory_space=pl.ANY),
        scratch_shapes=[pltpu.VMEM((R, D), f32), pltpu.VMEM((FFN_TILE, D), f32),
                        pltpu.SemaphoreType.DMA(())],
    )
    return pl.pallas_call(
        _dispatch_kernel,
        grid_spec=grid_spec,
        out_shape=jax.ShapeDtypeStruct((p_rows, D), f32),
        compiler_params=_cparams(("arbitrary",)),
        name="expert_dispatch",
    )(lstart, nchunk, gbase, tail_start, tail_chunks, n_used, h2, destT)


def _ffn_kernel(tile_expert_s, n_used_s, x_ref, w1_ref, b1_ref, w2_ref, b2_ref, y_ref):
    i = pl.program_id(0)

    @pl.when(i < n_used_s[0])
    def _():
        dff = w2_ref.shape[1]
        xb = x_ref[...].astype(bf16)
        hh = jnp.dot(xb, w1_ref[0], preferred_element_type=f32) + b1_ref[0]
        x_glu = jnp.minimum(hh[:, :dff], SWIGLU_LIMIT)
        x_lin = jnp.clip(hh[:, dff:], -SWIGLU_LIMIT, SWIGLU_LIMIT)
        act = x_glu * jax.nn.sigmoid(SWIGLU_ALPHA * x_glu) * (x_lin + 1.0)
        y_ref[...] = jnp.dot(act.astype(bf16), w2_ref[0], preferred_element_type=f32) + b2_ref[0]

    @pl.when(i >= n_used_s[0])
    def _():
        y_ref[...] = jnp.zeros(y_ref.shape, f32)


def _expert_ffn(xs, w1_bf, b1, w2_bf, b2, tile_expert, n_used):
    P, D = xs.shape
    E, _, F2 = w1_bf.shape
    tm = FFN_TILE
    nt = P // tm

    def tile(i, te, nu):
        return (jnp.minimum(i, nu[0] - 1), 0)

    def expert3(i, te, nu):
        return (te[jnp.minimum(i, nu[0] - 1)], 0, 0)

    grid_spec = pltpu.PrefetchScalarGridSpec(
        num_scalar_prefetch=2,
        grid=(nt,),
        in_specs=[pl.BlockSpec((tm, D), tile),
                  pl.BlockSpec((1, D, F2), expert3),
                  pl.BlockSpec((1, 1, F2), expert3),
                  pl.BlockSpec((1, F2 // 2, D), expert3),
                  pl.BlockSpec((1, 1, D), expert3)],
        out_specs=pl.BlockSpec((tm, D), lambda i, te, nu: (i, 0)),
    )
    return pl.pallas_call(
        _ffn_kernel,
        grid_spec=grid_spec,
        out_shape=jax.ShapeDtypeStruct((P, D), f32),
        compiler_params=_cparams(("arbitrary",)),
        name="expert_ffn",
    )(tile_expert, n_used, xs, w1_bf, b1.reshape(E, 1, F2), w2_bf, b2.reshape(E, 1, D))


def _combine_kernel(lstart_s, nchunk_s, gbase_s,
                    ys_ref, dest_ref, gate_ref, x1_ref, g2_ref, fg_ref, o_ref, local_ref, sem, *, final):
    w = pl.program_id(0)
    W = ROUTE_TILE
    R = local_ref.shape[0]

    @pl.when(w == 0)
    def _():
        local_ref[...] = jnp.zeros(local_ref.shape, f32)

    def chunk_copy(src_row, dst_row):
        return pltpu.make_async_copy(
            ys_ref.at[pl.ds(pl.multiple_of(src_row, ROW_CHUNK), ROW_CHUNK), :],
            local_ref.at[pl.ds(pl.multiple_of(dst_row, ROW_CHUNK), ROW_CHUNK), :], sem)

    def per_expert(e, total):
        n = nchunk_s[w * N_EXPERTS + e]
        dst = lstart_s[w * N_EXPERTS + e]
        src = gbase_s[w * N_EXPERTS + e]

        def issue(j, c):
            chunk_copy(src + j * ROW_CHUNK, dst + j * ROW_CHUNK).start()
            return c
        lax.fori_loop(0, n, issue, 0)
        return total + n

    total = lax.fori_loop(0, N_EXPERTS, per_expert, 0)

    def drain(j, c):
        chunk_copy(0, 0).wait()
        return c
    lax.fori_loop(0, total, drain, 0)

    cio = lax.broadcasted_iota(i32, (W, R), 1)
    weights = jnp.zeros((W, R), f32)
    for k in range(TOP_K):
        weights = jnp.where(cio == dest_ref[:, k:k + 1], gate_ref[:, k:k + 1], weights)
    moe = jnp.dot(weights.astype(bf16), local_ref[...].astype(bf16), preferred_element_type=f32)
    x2 = x1_ref[...] + g2_ref[0] * moe
    if final:
        ms = jnp.mean(x2 * x2, axis=-1, keepdims=True)
        x2 = x2 * lax.rsqrt(ms + EPS) * fg_ref[...]
    o_ref[...] = x2


def _combine(ys, dest, gates, x1, gate2, final_g, lstart, nchunk, gbase, seq, final):
    N, D = x1.shape
    W = ROUTE_TILE
    nw = N // W
    per_b = seq // W
    R = _max_window_rows()
    grid_spec = pltpu.PrefetchScalarGridSpec(
        num_scalar_prefetch=3,
        grid=(nw,),
        in_specs=[pl.BlockSpec(memory_space=pl.ANY),
                  pl.BlockSpec((W, TOP_K), lambda w, *_: (w, 0)),
                  pl.BlockSpec((W, TOP_K), lambda w, *_: (w, 0)),
                  pl.BlockSpec((W, D), lambda w, *_: (w, 0)),
                  pl.BlockSpec((1, 1, D), lambda w, *_: (w // per_b, 0, 0)),
                  pl.BlockSpec((1, D), lambda w, *_: (0, 0))],
        out_specs=pl.BlockSpec((W, D), lambda w, *_: (w, 0)),
        scratch_shapes=[pltpu.VMEM((R, D), f32), pltpu.SemaphoreType.DMA(())],
    )
    return pl.pallas_call(
        functools.partial(_combine_kernel, final=final),
        grid_spec=grid_spec,
        out_shape=jax.ShapeDtypeStruct((N, D), f32),
        compiler_params=_cparams(("arbitrary",)),
        name="expert_combine",
    )(lstart, nchunk, gbase, ys, dest, gates, x1, gate2, final_g)


def _routing_tables(cnt, idxT, rankT, n_tiles):
    nw, E = cnt.shape
    W = ROUTE_TILE
    padded = (cnt + ROW_CHUNK - 1) // ROW_CHUNK * ROW_CHUNK
    lstart = jnp.cumsum(padded, axis=1) - padded
    tot = jnp.sum(padded, axis=0)
    region = (tot + FFN_TILE - 1) // FFN_TILE * FFN_TILE
    region_end = jnp.cumsum(region)
    region_start = region_end - region
    gbase = region_start[None, :] + jnp.cumsum(padded, axis=0) - padded
    nchunk = padded // ROW_CHUNK
    tail_start = region_start + tot
    tail_chunks = (region - tot) // ROW_CHUNK
    n_used = (region_end[-1] // FFN_TILE).astype(i32).reshape(1)
    tile_expert = jnp.minimum(
        jnp.searchsorted(region_end, jnp.arange(n_tiles, dtype=i32) * FFN_TILE, side='right'),
        E - 1).astype(i32)
    win = jnp.arange(idxT.shape[1], dtype=i32) // W
    destT = lstart[win[None, :], idxT] + rankT
    flat = lambda a: a.reshape(-1).astype(i32)
    return dict(lstart=flat(lstart), nchunk=flat(nchunk), gbase=flat(gbase),
                tail_start=flat(tail_start), tail_chunks=flat(tail_chunks),
                n_used=n_used, tile_expert=tile_expert, destT=destT.astype(i32))


def _block_diag(w):
    g, a, b = w.shape
    out = jnp.zeros((g * a, g * b), w.dtype)
    for i in range(g):
        out = out.at[i * a:(i + 1) * a, i * b:(i + 1) * b].set(w[i])
    return out


def kernel(x, c, w_mod, b_mod, norm1_g, w_in, conv_w, conv_b, conv_norm_g, conv_norm_b, rel_bias, pool_w, pool_scale, sgu_norm_g, sgu_norm_b, sgu_w, sgu_b, mix_out_g, w_out, norm2_g, router_w, router_b, exp_w1, exp_b1, exp_w2, exp_b2, final_norm_g):
    B, S, D = x.shape
    L = w_mod.shape[0]
    N = B * S
    nw = N // ROUTE_TILE
    assert S % SEQ_TILE == 0 and S % MOBA_BLOCK == 0 and N % ROUTE_TILE == 0 and S % ROUTE_TILE == 0
    p_bound = N * TOP_K + nw * N_EXPERTS * (ROW_CHUNK - 1) + N_EXPERTS * (FFN_TILE - 1)
    n_tiles = -(-p_bound // FFN_TILE)
    p_rows = n_tiles * FFN_TILE

    mod = _modulation(c, w_mod, b_mod)
    bias_tiles = _bias_tiles(rel_bias)
    row = lambda a: a.reshape(1, -1)
    for l in range(L):
        m6 = mod[l].reshape(B, 6, 1, D)
        shift1, scale1, gate1, shift2, scale2, gate2 = (m6[:, j] for j in range(6))
        mg = row(mix_out_g[l])
        pc, qT, k, vT, kmean, pp, ps = _in_projection(x, shift1, scale1, row(norm1_g[l]), w_in[l].astype(bf16))
        yc, yp, ys = _local_mixers(
            pc, pp, ps, conv_w[l], row(conv_b[l]), row(conv_norm_g[l]), row(conv_norm_b[l]),
            _block_diag(pool_w[l]).astype(bf16), row(pool_scale[l]),
            row(sgu_norm_g[l]), row(sgu_norm_b[l]), sgu_w[l],
            jnp.repeat(sgu_b[l].T, HEAD_DIM, axis=1), mg)
        ya = _moba_attention(qT, k, vT, kmean.reshape(B, -1, GROUP_WIDTH), bias_tiles, mg)
        flat = lambda a: a.reshape(N, -1)
        x1, h2, idxT, gateT, rankT, cnt = _outproj_router(
            flat(yc), flat(ya), flat(yp), flat(ys), x.reshape(N, D), gate1, w_out[l].astype(bf16),
            row(norm2_g[l]), shift2, scale2, router_w[l].T, router_b[l].reshape(-1, 1), S)
        t = _routing_tables(cnt.reshape(nw, N_EXPERTS), idxT, rankT, n_tiles)
        xs = _dispatch(h2, t['destT'], t['lstart'], t['nchunk'], t['gbase'],
                       t['tail_start'], t['tail_chunks'], t['n_used'], p_rows)
        ysort = _expert_ffn(xs, exp_w1[l].astype(bf16), exp_b1[l], exp_w2[l].astype(bf16), exp_b2[l],
                            t['tile_expert'], t['n_used'])
        x = _combine(ysort, t['destT'].T, gateT.T, x1, gate2, row(final_norm_g),
                     t['lstart'], t['nchunk'], t['gbase'], S, final=(l == L - 1)).reshape(B, S, D)
    return x
```

```python
import functools
import math

import numpy as np
import jax
import jax.numpy as jnp
from jax import lax
from jax.experimental import pallas as pl
from jax.experimental.pallas import tpu as pltpu

f32, bf16, i32 = jnp.float32, jnp.bfloat16, jnp.int32

GROUP_WIDTH = 256
HEADS = 4
HEAD_DIM = 64
CONV_WIDTH = 31
MOBA_BLOCK = 256
MOBA_TOPK = 3
Q_CHUNK = 128
N_BUCKETS = 32
T5_MAX_DISTANCE = 128
POOL_WINDOWS = (2, 4, 8, 16)
SGU_CHUNK = 128
N_EXPERTS = 32
TOP_K = 4
SWIGLU_LIMIT = 7.0
SWIGLU_ALPHA = 1.702
EPS = 1e-6

HALO = 32
SEQ_TILE = 512
ROUTE_TILE = 256
ROW_CHUNK = 8
FFN_TILE = 512
NEG = -1e30
VMEM_LIMIT = 56 * 1024 * 1024


def _cparams(sem):
    return pltpu.CompilerParams(dimension_semantics=sem, vmem_limit_bytes=VMEM_LIMIT)


def _split_bf16(a):
    hi = a.astype(bf16)
    lo = (a - hi.astype(f32)).astype(bf16)
    return hi, lo


def _mod_kernel(c_ref, w_ref, b_ref, o_ref):
    c = c_ref[...]
    cond = c * jax.nn.sigmoid(c)
    o_ref[0] = jnp.dot(cond, w_ref[0], preferred_element_type=f32,
                       precision=lax.Precision.HIGHEST) + b_ref[0]


def _modulation(c, w_mod, b_mod):
    L, D, M = w_mod.shape
    B = c.shape[0]
    tn = 1536
    return pl.pallas_call(
        _mod_kernel,
        grid=(L, M // tn),
        in_specs=[pl.BlockSpec((B, D), lambda l, j: (0, 0)),
                  pl.BlockSpec((1, D, tn), lambda l, j: (l, 0, j)),
                  pl.BlockSpec((1, 1, tn), lambda l, j: (l, 0, j))],
        out_specs=pl.BlockSpec((1, B, tn), lambda l, j: (l, 0, j)),
        out_shape=jax.ShapeDtypeStruct((L, B, M), f32),
        compiler_params=_cparams(("arbitrary", "arbitrary")),
        name="modulation",
    )(c, w_mod, b_mod.reshape(L, 1, M))


def _inproj_kernel(x_ref, sh_ref, sc_ref, g_ref, w_ref,
                   pc_ref, qT_ref, k_ref, vT_ref, km_ref, pp_ref, ps_ref):
    x = x_ref[0]
    ms = jnp.mean(x * x, axis=-1, keepdims=True)
    h = x * lax.rsqrt(ms + EPS) * g_ref[...]
    h = h * (1.0 + sc_ref[0]) + sh_ref[0]
    proj = jnp.dot(h.astype(bf16), w_ref[...], preferred_element_type=f32)
    gw = GROUP_WIDTH
    pc_ref[0] = proj[:, 0:2 * gw]
    q = proj[:, 2 * gw:3 * gw] * (HEAD_DIM ** -0.5)
    qT_ref[0] = q.T.astype(bf16)
    kk = proj[:, 3 * gw:4 * gw]
    for h in range(HEADS):
        k_ref[0, h] = kk[:, h * HEAD_DIM:(h + 1) * HEAD_DIM].astype(bf16)
    for j in range(SEQ_TILE // MOBA_BLOCK):
        km_ref[0, 0, j:j + 1, :] = jnp.mean(kk[j * MOBA_BLOCK:(j + 1) * MOBA_BLOCK], axis=0, keepdims=True)
    for j in range(SEQ_TILE // MOBA_BLOCK):
        vT_ref[0, j] = proj[j * MOBA_BLOCK:(j + 1) * MOBA_BLOCK, 4 * gw:5 * gw].T.astype(bf16)
    pp_ref[0] = proj[:, 5 * gw:6 * gw]
    ps_ref[0] = proj[:, 6 * gw:8 * gw]


def _in_projection(x, shift, scale, g, w_bf):
    B, S, D = x.shape
    gw = GROUP_WIDTH
    ts = SEQ_TILE
    nt = S // ts
    row = lambda b, i: (b, i, 0)
    col = lambda b, i: (b, 0, i)
    vec = lambda b, i: (b, 0, 0)
    return pl.pallas_call(
        _inproj_kernel,
        grid=(B, nt),
        in_specs=[pl.BlockSpec((1, ts, D), row),
                  pl.BlockSpec((1, 1, D), vec),
                  pl.BlockSpec((1, 1, D), vec),
                  pl.BlockSpec((1, D), lambda b, i: (0, 0)),
                  pl.BlockSpec(w_bf.shape, lambda b, i: (0, 0))],
        out_specs=[pl.BlockSpec((1, ts, 2 * gw), row),
                   pl.BlockSpec((1, gw, ts), col),
                   pl.BlockSpec((1, HEADS, ts, HEAD_DIM), lambda b, i: (b, 0, i, 0)),
                   pl.BlockSpec((1, ts // MOBA_BLOCK, gw, MOBA_BLOCK), lambda b, i: (b, i, 0, 0)),
                   pl.BlockSpec((1, 1, ts // MOBA_BLOCK, gw), lambda b, i: (b, i, 0, 0)),
                   pl.BlockSpec((1, ts, gw), row),
                   pl.BlockSpec((1, ts, 2 * gw), row)],
        out_shape=[jax.ShapeDtypeStruct((B, S, 2 * gw), f32),
                   jax.ShapeDtypeStruct((B, gw, S), bf16),
                   jax.ShapeDtypeStruct((B, HEADS, S, HEAD_DIM), bf16),
                   jax.ShapeDtypeStruct((B, S // MOBA_BLOCK, gw, MOBA_BLOCK), bf16),
                   jax.ShapeDtypeStruct((B, nt, ts // MOBA_BLOCK, gw), f32),
                   jax.ShapeDtypeStruct((B, S, gw), f32),
                   jax.ShapeDtypeStruct((B, S, 2 * gw), f32)],
        compiler_params=_cparams(("arbitrary", "arbitrary")),
        name="in_projection",
    )(x, shift, scale, g, w_bf)


def _group_rms(y, g):
    return y * lax.rsqrt(jnp.mean(y * y, axis=-1, keepdims=True) + EPS) * g


def _local_kernel(pc_ref, pcp_ref, pp_ref, ppp_ref, ps_ref,
                  cw_ref, cb_ref, cng_ref, cnb_ref, pw_ref, psc_ref,
                  sg_ref, sb_ref, sw_ref, sbias_ref, mg_ref,
                  yc_ref, yp_ref, ys_ref,
                  gext, zext, s2, s4, s8):
    i = pl.program_id(1)
    ts = SEQ_TILE
    gw = GROUP_WIDTH
    first = i == 0
    lane = lax.broadcasted_iota(i32, (1, gw), 1)

    pc = pc_ref[0]
    g = pc[:, :gw] * jax.nn.sigmoid(pc[:, gw:])
    ph = pcp_ref[0]
    gh = ph[:, :gw] * jax.nn.sigmoid(ph[:, gw:])
    gext[0:HALO, :] = jnp.where(first, 0.0, gh)
    gext[HALO:HALO + ts, :] = g
    acc = jnp.zeros((ts, gw), f32)
    for j in range(CONV_WIDTH):
        acc = acc + cw_ref[j:j + 1, :] * gext[pl.ds(HALO - (CONV_WIDTH - 1) + j, ts), :]
    y = acc + cb_ref[...]
    r = lax.broadcasted_iota(i32, (gw, gw), 0) // HEAD_DIM
    c = lax.broadcasted_iota(i32, (gw, gw), 1) // HEAD_DIM
    avg = jnp.where(r == c, 1.0 / HEAD_DIM, 0.0).astype(bf16)

    def head_mean(t):
        hi, lo = _split_bf16(t)
        return (jnp.dot(hi, avg, preferred_element_type=f32)
                + jnp.dot(lo, avg, preferred_element_type=f32))

    mu = head_mean(y)
    yc = y - mu
    var = head_mean(yc * yc)
    yn = yc * lax.rsqrt(var + EPS) * cng_ref[...] + cnb_ref[...]
    yconv = yn * jax.nn.sigmoid(yn)
    yc_ref[0] = _group_rms(yconv, mg_ref[:, 0:gw]).astype(bf16)

    z = pp_ref[0]
    zext[0:HALO, :] = jnp.where(first, 0.0, ppp_ref[0])
    zext[HALO:HALO + ts, :] = z
    n2, n4, n8 = ts + 14, ts + 12, ts + 8
    s2[0:n2, :] = zext[pl.ds(HALO - 14, n2), :] + zext[pl.ds(HALO - 15, n2), :]
    s4[0:n4, :] = s2[pl.ds(2, n4), :] + s2[pl.ds(0, n4), :]
    s8[0:n8, :] = s4[pl.ds(4, n8), :] + s4[pl.ds(0, n8), :]
    w2 = s2[pl.ds(14, ts), :]
    w4 = s4[pl.ds(12, ts), :]
    w8 = s8[pl.ds(8, ts), :]
    w16 = w8 + s8[pl.ds(0, ts), :]
    tpos = (i * ts + lax.broadcasted_iota(i32, (ts, 1), 0) + 1).astype(f32)
    grp = lane // (gw // len(POOL_WINDOWS))
    pooled = jnp.zeros((ts, gw), f32)
    for gi, (w, sw) in enumerate(zip(POOL_WINDOWS, (w2, w4, w8, w16))):
        pooled = jnp.where(grp == gi, sw / jnp.minimum(tpos, float(w)), pooled)
    pooled = pooled - z
    yp = jnp.dot(pooled.astype(bf16), pw_ref[...], preferred_element_type=f32) * psc_ref[...]
    yp_ref[0] = _group_rms(yp, mg_ref[:, 2 * gw:3 * gw]).astype(bf16)

    zz = ps_ref[0]
    zz = 0.5 * zz * (1.0 + lax.erf(zz * (1.0 / math.sqrt(2.0))))
    u = zz[:, :gw]
    v = zz[:, gw:]
    vm = jnp.mean(v, axis=-1, keepdims=True)
    vc = v - vm
    vv = jnp.mean(vc * vc, axis=-1, keepdims=True)
    vn = (vc * lax.rsqrt(vv + EPS) * sg_ref[...] + sb_ref[...]).astype(bf16)
    li = lax.broadcasted_iota(i32, (SGU_CHUNK, SGU_CHUNK), 0)
    lj = lax.broadcasted_iota(i32, (SGU_CHUNK, SGU_CHUNK), 1)
    head_of_lane = lane // HEAD_DIM
    wts = [jnp.where(li >= lj, sw_ref[h], 0.0).astype(bf16) for h in range(HEADS)]
    outs = []
    for n in range(ts // SGU_CHUNK):
        vch = vn[n * SGU_CHUNK:(n + 1) * SGU_CHUNK]
        mixed = sbias_ref[...]
        for h in range(HEADS):
            mh = jnp.dot(wts[h], vch, preferred_element_type=f32)
            mixed = mixed + jnp.where(head_of_lane == h, mh, 0.0)
        outs.append(u[n * SGU_CHUNK:(n + 1) * SGU_CHUNK] * mixed)
    ysgu = jnp.concatenate(outs, axis=0)
    ys_ref[0] = _group_rms(ysgu, mg_ref[:, 3 * gw:4 * gw]).astype(bf16)


def _local_mixers(pc, pp, ps, cw, cb, cng, cnb, pw_bd, psc, sg, sb, sw, sbias, mg):
    B, S, _ = pc.shape
    gw = GROUP_WIDTH
    ts = SEQ_TILE
    hb = ts // HALO
    row = lambda b, i: (b, i, 0)
    prev = lambda b, i: (b, jnp.maximum(i * hb - 1, 0), 0)
    full2 = lambda b, i: (0, 0)
    full3 = lambda b, i: (0, 0, 0)
    out = jax.ShapeDtypeStruct((B, S, gw), bf16)
    return pl.pallas_call(
        _local_kernel,
        grid=(B, S // ts),
        in_specs=[pl.BlockSpec((1, ts, 2 * gw), row),
                  pl.BlockSpec((1, HALO, 2 * gw), prev),
                  pl.BlockSpec((1, ts, gw), row),
                  pl.BlockSpec((1, HALO, gw), prev),
                  pl.BlockSpec((1, ts, 2 * gw), row),
                  pl.BlockSpec(cw.shape, full2), pl.BlockSpec(cb.shape, full2),
                  pl.BlockSpec(cng.shape, full2), pl.BlockSpec(cnb.shape, full2),
                  pl.BlockSpec(pw_bd.shape, full2), pl.BlockSpec(psc.shape, full2),
                  pl.BlockSpec(sg.shape, full2), pl.BlockSpec(sb.shape, full2),
                  pl.BlockSpec(sw.shape, full3), pl.BlockSpec(sbias.shape, full2),
                  pl.BlockSpec(mg.shape, full2)],
        out_specs=[pl.BlockSpec((1, ts, gw), row)] * 3,
        out_shape=[out, out, out],
        scratch_shapes=[pltpu.VMEM((ts + HALO, gw), f32), pltpu.VMEM((ts + HALO, gw), f32),
                        pltpu.VMEM((ts + 16, gw), f32), pltpu.VMEM((ts + 16, gw), f32),
                        pltpu.VMEM((ts + 16, gw), f32)],
        compiler_params=_cparams(("arbitrary", "arbitrary")),
        name="local_mixers",
    )(pc, pc, pp, pp, ps, cw, cb, cng, cnb, pw_bd, psc, sg, sb, sw, sbias, mg)


def _t5_bucket_table(max_dist):
    d = np.arange(max_dist, dtype=np.int64)
    max_exact = N_BUCKETS // 2
    nf = np.maximum(d, 1).astype(np.float32)
    large = max_exact + (np.log(nf / np.float32(max_exact)) / np.float32(math.log(T5_MAX_DISTANCE / max_exact))
                         * np.float32(N_BUCKETS - max_exact)).astype(np.int32)
    large = np.minimum(large, N_BUCKETS - 1)
    return np.where(d < max_exact, d, large).astype(np.int32)


_TILE_BASES = (0, MOBA_BLOCK)


def _bias_kernel(tab_ref, o_ref):
    blk, qc = MOBA_BLOCK, MOBA_BLOCK
    table = _t5_bucket_table(2 * blk + qc)
    first = [int(np.argmax(table >= b)) for b in range(N_BUCKETS)]
    j = lax.broadcasted_iota(i32, (blk, qc), 0)
    q = lax.broadcasted_iota(i32, (blk, qc), 1)
    for t, base in enumerate(_TILE_BASES):
        d = base + q - j
        lo, hi = max(base - (blk - 1), 0), base + qc - 1
        for h in range(HEADS):
            val = jnp.full((blk, qc), tab_ref[h], f32)
            for b in range(1, N_BUCKETS):
                if first[b] > hi:
                    continue
                if first[b] <= lo:
                    val = jnp.full((blk, qc), tab_ref[b * HEADS + h], f32)
                else:
                    val = jnp.where(d >= first[b], tab_ref[b * HEADS + h], val)
            o_ref[t, h] = jnp.where(d >= 0, val, NEG)


def _bias_tiles(rel_bias):
    return pl.pallas_call(
        _bias_kernel,
        in_specs=[pl.BlockSpec(memory_space=pltpu.SMEM)],
        out_shape=jax.ShapeDtypeStruct((len(_TILE_BASES), HEADS, MOBA_BLOCK, MOBA_BLOCK), f32),
        name="bias_tiles",
    )(rel_bias.astype(f32).reshape(-1))


def _attn_kernel(far_ref, qT_ref, k_ref, vT_ref, km_ref, bias_ref, mg_ref, o_ref, mask_ref, s_ref):
    own = pl.program_id(1)
    blk, gw, hd = MOBA_BLOCK, GROUP_WIDTH, HEAD_DIM
    nb = km_ref.shape[1]

    nio = lax.broadcasted_iota(i32, (nb, blk), 0)
    past = nio < own
    km = km_ref[0]
    q_heads = []
    for h in range(HEADS):
        qh = qT_ref[0, h * hd:(h + 1) * hd, :]
        q_heads.append(qh)
        km_hi, km_lo = _split_bf16(km[:, h * hd:(h + 1) * hd])
        gate = (jnp.dot(km_hi, qh, preferred_element_type=f32)
                + jnp.dot(km_lo, qh, preferred_element_type=f32))
        gate = jnp.where(past, gate, -jnp.inf)
        rank = jnp.zeros((nb, blk), i32)
        for m in range(nb):
            gm = gate[m:m + 1, :]
            beats = (gm > gate) | ((gm == gate) & (m < nio))
            rank = rank + beats.astype(i32)
        sel = (rank < MOBA_TOPK) & past
        mask_ref[0, h] = jnp.where(sel, 0.0, NEG)
        mask_ref[1, h] = jnp.where(sel, far_ref[h], NEG)

    def qk(n, h):
        kb = k_ref[0, h, pl.ds(pl.multiple_of(n * blk, blk), blk), :]
        return jnp.dot(kb, q_heads[h], preferred_element_type=f32)

    def far_scores(n, h):
        return qk(n, h) + mask_ref[1, h, pl.ds(n, 1), :]

    def update(state, scores, n):
        out = []
        for h in range(HEADS):
            m, l, acc = state[3 * h:3 * h + 3]
            s = scores[h]
            m_new = jnp.maximum(m, jnp.max(s, axis=0, keepdims=True))
            alpha = jnp.exp(m - m_new)
            p = jnp.exp(s - m_new)
            l = alpha * l + jnp.sum(p, axis=0, keepdims=True)
            vb = vT_ref[0, n, h * hd:(h + 1) * hd, :]
            acc = acc * alpha + jnp.dot(vb, p.astype(bf16), preferred_element_type=f32)
            out += [m_new, l, acc]
        return tuple(out)

    adj = jnp.maximum(own - 1, 0)
    n_far = jnp.maximum(own - 1, 0)
    s_own = [qk(own, h) + bias_ref[0, h] for h in range(HEADS)]
    s_adj = [qk(adj, h) + bias_ref[1, h] + mask_ref[0, h, pl.ds(adj, 1), :] for h in range(HEADS)]
    for h in range(HEADS):
        s_ref[0, h] = far_scores(0, h)

    state = []
    for h in range(HEADS):
        m0 = jnp.max(s_own[h], axis=0, keepdims=True)
        p = jnp.exp(s_own[h] - m0)
        vb = vT_ref[0, own, h * hd:(h + 1) * hd, :]
        state += [m0, jnp.sum(p, axis=0, keepdims=True),
                  jnp.dot(vb, p.astype(bf16), preferred_element_type=f32)]
    state = update(tuple(state), s_adj, adj)

    def body(n, state):
        slot = n % 2
        cur = [s_ref[slot, h] for h in range(HEADS)]
        nxt = jnp.minimum(n + 1, jnp.maximum(n_far - 1, 0))
        for h in range(HEADS):
            s_ref[1 - slot, h] = far_scores(nxt, h)
        return update(state, cur, n)

    fin = lax.fori_loop(0, n_far, body, state)
    outT = jnp.concatenate([fin[3 * h + 2] / fin[3 * h + 1] for h in range(HEADS)], axis=0)
    o_ref[0] = _group_rms(outT.T, mg_ref[:, gw:2 * gw]).astype(bf16)


def _moba_attention(qT, k, vT, kmean, bias_tiles, far_bias, mg):
    B, _, S, _ = k.shape
    gw, blk = GROUP_WIDTH, MOBA_BLOCK
    nb = kmean.shape[1]
    return pl.pallas_call(
        _attn_kernel,
        grid=(B, nb),
        in_specs=[pl.BlockSpec(memory_space=pltpu.SMEM),
                  pl.BlockSpec((1, gw, blk), lambda b, c: (b, 0, c)),
                  pl.BlockSpec((1, HEADS, S, HEAD_DIM), lambda b, c: (b, 0, 0, 0)),
                  pl.BlockSpec((1, nb, gw, blk), lambda b, c: (b, 0, 0, 0)),
                  pl.BlockSpec((1, nb, gw), lambda b, c: (b, 0, 0)),
                  pl.BlockSpec(bias_tiles.shape, lambda b, c: (0, 0, 0, 0)),
                  pl.BlockSpec(mg.shape, lambda b, c: (0, 0))],
        out_specs=pl.BlockSpec((1, blk, gw), lambda b, c: (b, c, 0)),
        out_shape=jax.ShapeDtypeStruct((B, S, gw), bf16),
        scratch_shapes=[pltpu.VMEM((2, HEADS, nb, blk), f32), pltpu.VMEM((2, HEADS, blk, blk), f32)],
        compiler_params=_cparams(("arbitrary", "arbitrary")),
        name="moba_attention",
    )(far_bias, qT, k, vT, kmean, bias_tiles, mg)


def _outproj_router_kernel(yc_ref, ya_ref, yp_ref, ys_ref, x_ref, g1_ref, wo_ref, n2_ref, sh_ref, sc_ref,
                           rw_ref, rb_ref,
                           x1_ref, h2_ref, dest_ref, gate_ref, pad_ref):
    gw = GROUP_WIDTH
    W = ROUTE_TILE
    acc = jnp.dot(yc_ref[...], wo_ref[0:gw, :], preferred_element_type=f32)
    acc = acc + jnp.dot(ya_ref[...], wo_ref[gw:2 * gw, :], preferred_element_type=f32)
    acc = acc + jnp.dot(yp_ref[...], wo_ref[2 * gw:3 * gw, :], preferred_element_type=f32)
    acc = acc + jnp.dot(ys_ref[...], wo_ref[3 * gw:4 * gw, :], preferred_element_type=f32)
    x1 = x_ref[...] + g1_ref[0] * acc
    x1_ref[...] = x1
    ms = jnp.mean(x1 * x1, axis=-1, keepdims=True)
    h = x1 * lax.rsqrt(ms + EPS) * n2_ref[...]
    h = h * (1.0 + sc_ref[0]) + sh_ref[0]
    h_hi, h_lo = _split_bf16(h)
    h2_ref[...] = h_hi

    nt = (((1,), (1,)), ((), ()))
    rw_hi, rw_lo = _split_bf16(rw_ref[...])
    logits = (lax.dot_general(rw_hi, h_hi, nt, preferred_element_type=f32)
              + lax.dot_general(rw_hi, h_lo, nt, preferred_element_type=f32)
              + lax.dot_general(rw_lo, h_hi, nt, preferred_element_type=f32)) + rb_ref[...]
    eio = lax.broadcasted_iota(i32, (N_EXPERTS, W), 0)
    work = logits
    vals, sels = [], []
    for k in range(TOP_K):
        m = jnp.max(work, axis=0, keepdims=True)
        idx = jnp.min(jnp.where(work == m, eio, N_EXPERTS), axis=0, keepdims=True)
        sel = eio == idx
        vals.append(m)
        sels.append(sel)
        work = jnp.where(sel, -jnp.inf, work)
    exps = [jnp.exp(v - vals[0]) for v in vals]
    denom = exps[0] + exps[1] + exps[2] + exps[3]
    for k in range(TOP_K):
        gate_ref[k:k + 1, :] = exps[k] / denom

    multi = jnp.zeros((N_EXPERTS, W), f32)
    for sel in sels:
        multi = multi + sel.astype(f32)
    before = (lax.broadcasted_iota(i32, (W, W), 0) < lax.broadcasted_iota(i32, (W, W), 1)).astype(bf16)
    earlier = jnp.dot(multi.astype(bf16), before, preferred_element_type=f32)
    cnt = jnp.sum(multi, axis=1, keepdims=True).astype(i32)
    padded = (cnt + (ROW_CHUNK - 1)) // ROW_CHUNK * ROW_CHUNK
    pad_ref[0] = padded
    lower = (lax.broadcasted_iota(i32, (N_EXPERTS, N_EXPERTS), 1)
             < lax.broadcasted_iota(i32, (N_EXPERTS, N_EXPERTS), 0)).astype(bf16)
    seg_start = jnp.dot(lower, jnp.broadcast_to(padded.astype(f32), (N_EXPERTS, W)).astype(bf16),
                        preferred_element_type=f32)
    row = seg_start + earlier
    for k in range(TOP_K):
        dest_ref[k:k + 1, :] = jnp.sum(jnp.where(sels[k], row, 0.0), axis=0, keepdims=True).astype(i32)


def _outproj_router(yc, ya, yp, ys, x, gate1, wo_bf, n2g, shift2, scale2, rwT, rb, seq):
    N, D = x.shape
    gw = GROUP_WIDTH
    W = ROUTE_TILE
    nw = N // W
    per_b = seq // W
    row = lambda i: (i, 0)
    vec = lambda i: (i // per_b, 0, 0)
    full = lambda i: (0, 0)
    colblk = lambda i: (0, i)
    return pl.pallas_call(
        _outproj_router_kernel,
        grid=(nw,),
        in_specs=[pl.BlockSpec((W, gw), row)] * 4 + [
            pl.BlockSpec((W, D), row),
            pl.BlockSpec((1, 1, D), vec),
            pl.BlockSpec(wo_bf.shape, full),
            pl.BlockSpec((1, D), full),
            pl.BlockSpec((1, 1, D), vec),
            pl.BlockSpec((1, 1, D), vec),
            pl.BlockSpec(rwT.shape, full),
            pl.BlockSpec(rb.shape, full)],
        out_specs=[pl.BlockSpec((W, D), row),
                   pl.BlockSpec((W, D), row),
                   pl.BlockSpec((TOP_K, W), colblk),
                   pl.BlockSpec((TOP_K, W), colblk),
                   pl.BlockSpec((1, N_EXPERTS, 1), lambda i: (i, 0, 0))],
        out_shape=[jax.ShapeDtypeStruct((N, D), f32),
                   jax.ShapeDtypeStruct((N, D), bf16),
                   jax.ShapeDtypeStruct((TOP_K, N), i32),
                   jax.ShapeDtypeStruct((TOP_K, N), f32),
                   jax.ShapeDtypeStruct((nw, N_EXPERTS, 1), i32)],
        compiler_params=_cparams(("arbitrary",)),
        name="outproj_router",
    )(yc, ya, yp, ys, x, gate1, wo_bf, n2g, shift2, scale2, rwT, rb)


def _max_window_rows():
    return -(-(ROUTE_TILE * TOP_K + N_EXPERTS * (ROW_CHUNK - 1)) // 128) * 128


def _dispatch_kernel(lstart_s, nchunk_s, gbase_s, tail_start_s, tail_chunks_s, n_used_s,
                     h_ref, dest_ref, xs_ref, sorted_ref, zero_ref, sem):
    w = pl.program_id(0)
    W = ROUTE_TILE
    R = sorted_ref.shape[0]
    rio = lax.broadcasted_iota(i32, (R, W), 0)
    hit = rio == dest_ref[0:1, :]
    for k in range(1, TOP_K):
        hit = hit | (rio == dest_ref[k:k + 1, :])
    onehot = jnp.where(hit, 1.0, 0.0).astype(bf16)
    sorted_ref[...] = jnp.dot(onehot, h_ref[...], preferred_element_type=f32)

    def chunk_copy(src_row, dst_row):
        return pltpu.make_async_copy(
            sorted_ref.at[pl.ds(pl.multiple_of(src_row, ROW_CHUNK), ROW_CHUNK), :],
            xs_ref.at[pl.ds(pl.multiple_of(dst_row, ROW_CHUNK), ROW_CHUNK), :], sem)

    def zero_copy(dst_row):
        return pltpu.make_async_copy(
            zero_ref.at[0:ROW_CHUNK, :],
            xs_ref.at[pl.ds(pl.multiple_of(dst_row, ROW_CHUNK), ROW_CHUNK), :], sem)

    def zero_tile_copy(tile):
        return pltpu.make_async_copy(
            zero_ref, xs_ref.at[pl.ds(pl.multiple_of(tile * FFN_TILE, FFN_TILE), FFN_TILE), :], sem)

    def per_expert(e, total):
        n = nchunk_s[w * N_EXPERTS + e]
        src = lstart_s[w * N_EXPERTS + e]
        dst = gbase_s[w * N_EXPERTS + e]

        def issue(j, c):
            chunk_copy(src + j * ROW_CHUNK, dst + j * ROW_CHUNK).start()
            return c
        lax.fori_loop(0, n, issue, 0)
        return total + n

    total = lax.fori_loop(0, N_EXPERTS, per_expert, 0)

    def drain(j, c):
        chunk_copy(0, 0).wait()
        return c
    lax.fori_loop(0, total, drain, 0)

    @pl.when(w == pl.num_programs(0) - 1)
    def _():
        zero_ref[...] = jnp.zeros(zero_ref.shape, f32)

        def per_tail(e, total):
            n = tail_chunks_s[e]
            dst = tail_start_s[e]

            def issue(j, c):
                zero_copy(dst + j * ROW_CHUNK).start()
                return c
            lax.fori_loop(0, n, issue, 0)
            return total + n
        tails = lax.fori_loop(0, N_EXPERTS, per_tail, 0)

        def drain_tail(j, c):
            zero_copy(0).wait()
            return c
        lax.fori_loop(0, tails, drain_tail, 0)

        n_tiles = xs_ref.shape[0] // FFN_TILE

        def issue_tile(j, c):
            zero_tile_copy(j).start()
            return c
        lax.fori_loop(n_used_s[0], n_tiles, issue_tile, 0)

        def drain_tile(j, c):
            zero_tile_copy(0).wait()
            return c
        lax.fori_loop(n_used_s[0], n_tiles, drain_tile, 0)


def _dispatch(h2, destT, lstart, nchunk, gbase, tail_start, tail_chunks, n_used, p_rows):
    N, D = h2.shape
    W = ROUTE_TILE
    nw = N // W
    R = _max_window_rows()
    grid_spec = pltpu.PrefetchScalarGridSpec(
        num_scalar_prefetch=6,
        grid=(nw,),
        in_specs=[pl.BlockSpec((W, D), lambda w, *_: (w, 0)),
                  pl.BlockSpec((TOP_K, W), lambda w, *_: (0, w))],
        out_specs=pl.BlockSpec(memory_space=pl.ANY),
        scratch_shapes=[pltpu.VMEM((R, D), f32), pltpu.VMEM((FFN_TILE, D), f32),
                        pltpu.SemaphoreType.DMA(())],
    )
    return pl.pallas_call(
        _dispatch_kernel,
        grid_spec=grid_spec,
        out_shape=jax.ShapeDtypeStruct((p_rows, D), f32),
        compiler_params=_cparams(("arbitrary",)),
        name="expert_dispatch",
    )(lstart, nchunk, gbase, tail_start, tail_chunks, n_used, h2, destT)


def _ffn_kernel(tile_expert_s, n_used_s, x_ref, w1_ref, b1_ref, w2_ref, b2_ref, y_ref, w1b_ref, w2b_ref):
    i = pl.program_id(0)
    last = n_used_s[0] - 1
    expert = tile_expert_s[jnp.minimum(i, last)]
    prev_expert = tile_expert_s[jnp.minimum(jnp.maximum(i - 1, 0), last)]

    @pl.when((i == 0) | (expert != prev_expert))
    def _():
        w1b_ref[...] = w1_ref[0, 0].astype(bf16)
        w2b_ref[...] = w2_ref[0, 0].astype(bf16)

    @pl.when(i < n_used_s[0])
    def _():
        dff = w2b_ref.shape[0]
        xb = x_ref[...].astype(bf16)
        hh = jnp.dot(xb, w1b_ref[...], preferred_element_type=f32) + b1_ref[0, 0]
        x_glu = jnp.minimum(hh[:, :dff], SWIGLU_LIMIT)
        x_lin = jnp.clip(hh[:, dff:], -SWIGLU_LIMIT, SWIGLU_LIMIT)
        act = x_glu * jax.nn.sigmoid(SWIGLU_ALPHA * x_glu) * (x_lin + 1.0)
        y_ref[...] = jnp.dot(act.astype(bf16), w2b_ref[...], preferred_element_type=f32) + b2_ref[0, 0]

    @pl.when(i >= n_used_s[0])
    def _():
        y_ref[...] = jnp.zeros(y_ref.shape, f32)


def _expert_ffn(xs, w1, b1, w2, b2, tile_expert, n_used, layer):
    P, D = xs.shape
    L, E, _, F2 = w1.shape
    tm = FFN_TILE
    nt = P // tm

    def tile(i, te, nu):
        return (jnp.minimum(i, nu[0] - 1), 0)

    def expert4(i, te, nu):
        return (layer, te[jnp.minimum(i, nu[0] - 1)], 0, 0)

    grid_spec = pltpu.PrefetchScalarGridSpec(
        num_scalar_prefetch=2,
        grid=(nt,),
        in_specs=[pl.BlockSpec((tm, D), tile),
                  pl.BlockSpec((1, 1, D, F2), expert4),
                  pl.BlockSpec((1, 1, 1, F2), expert4),
                  pl.BlockSpec((1, 1, F2 // 2, D), expert4),
                  pl.BlockSpec((1, 1, 1, D), expert4)],
        out_specs=pl.BlockSpec((tm, D), lambda i, te, nu: (i, 0)),
        scratch_shapes=[pltpu.VMEM((D, F2), bf16), pltpu.VMEM((F2 // 2, D), bf16)],
    )
    return pl.pallas_call(
        _ffn_kernel,
        grid_spec=grid_spec,
        out_shape=jax.ShapeDtypeStruct((P, D), f32),
        compiler_params=_cparams(("arbitrary",)),
        name="expert_ffn",
    )(tile_expert, n_used, xs, w1, b1.reshape(L, E, 1, F2), w2, b2.reshape(L, E, 1, D))


def _combine_kernel(lstart_s, nchunk_s, gbase_s,
                    ys_ref, dest_ref, gate_ref, x1_ref, g2_ref, fg_ref, o_ref, local_ref, sem, *, final):
    w = pl.program_id(0)
    W = ROUTE_TILE
    R = local_ref.shape[0]

    @pl.when(w == 0)
    def _():
        local_ref[...] = jnp.zeros(local_ref.shape, f32)

    def chunk_copy(src_row, dst_row):
        return pltpu.make_async_copy(
            ys_ref.at[pl.ds(pl.multiple_of(src_row, ROW_CHUNK), ROW_CHUNK), :],
            local_ref.at[pl.ds(pl.multiple_of(dst_row, ROW_CHUNK), ROW_CHUNK), :], sem)

    def per_expert(e, total):
        n = nchunk_s[w * N_EXPERTS + e]
        dst = lstart_s[w * N_EXPERTS + e]
        src = gbase_s[w * N_EXPERTS + e]

        def issue(j, c):
            chunk_copy(src + j * ROW_CHUNK, dst + j * ROW_CHUNK).start()
            return c
        lax.fori_loop(0, n, issue, 0)
        return total + n

    total = lax.fori_loop(0, N_EXPERTS, per_expert, 0)

    def drain(j, c):
        chunk_copy(0, 0).wait()
        return c
    lax.fori_loop(0, total, drain, 0)

    rio = lax.broadcasted_iota(i32, (R, W), 0)
    weights = jnp.zeros((R, W), f32)
    for k in range(TOP_K):
        weights = jnp.where(rio == dest_ref[k:k + 1, :], gate_ref[k:k + 1, :], weights)
    moe = lax.dot_general(weights.astype(bf16), local_ref[...].astype(bf16), (((0,), (0,)), ((), ())),
                          preferred_element_type=f32)
    x2 = x1_ref[...] + g2_ref[0] * moe
    if final:
        ms = jnp.mean(x2 * x2, axis=-1, keepdims=True)
        x2 = x2 * lax.rsqrt(ms + EPS) * fg_ref[...]
    o_ref[...] = x2


def _combine(ys, dest, gates, x1, gate2, final_g, lstart, nchunk, gbase, seq, final):
    N, D = x1.shape
    W = ROUTE_TILE
    nw = N // W
    per_b = seq // W
    R = _max_window_rows()
    grid_spec = pltpu.PrefetchScalarGridSpec(
        num_scalar_prefetch=3,
        grid=(nw,),
        in_specs=[pl.BlockSpec(memory_space=pl.ANY),
                  pl.BlockSpec((TOP_K, W), lambda w, *_: (0, w)),
                  pl.BlockSpec((TOP_K, W), lambda w, *_: (0, w)),
                  pl.BlockSpec((W, D), lambda w, *_: (w, 0)),
                  pl.BlockSpec((1, 1, D), lambda w, *_: (w // per_b, 0, 0)),
                  pl.BlockSpec((1, D), lambda w, *_: (0, 0))],
        out_specs=pl.BlockSpec((W, D), lambda w, *_: (w, 0)),
        scratch_shapes=[pltpu.VMEM((R, D), f32), pltpu.SemaphoreType.DMA(())],
    )
    return pl.pallas_call(
        functools.partial(_combine_kernel, final=final),
        grid_spec=grid_spec,
        out_shape=jax.ShapeDtypeStruct((N, D), f32),
        compiler_params=_cparams(("arbitrary",)),
        name="expert_combine",
    )(lstart, nchunk, gbase, ys, dest, gates, x1, gate2, final_g)


def _routing_tables(padded, n_tiles):
    nw, E = padded.shape
    lstart = jnp.cumsum(padded, axis=1) - padded
    tot = jnp.sum(padded, axis=0)
    region = (tot + FFN_TILE - 1) // FFN_TILE * FFN_TILE
    region_end = jnp.cumsum(region)
    region_start = region_end - region
    gbase = region_start[None, :] + jnp.cumsum(padded, axis=0) - padded
    nchunk = padded // ROW_CHUNK
    tail_start = region_start + tot
    tail_chunks = (region - tot) // ROW_CHUNK
    n_used = (region_end[-1] // FFN_TILE).astype(i32).reshape(1)
    tile_row = jnp.arange(n_tiles, dtype=i32) * FFN_TILE
    tile_expert = jnp.minimum(
        jnp.sum((region_end[None, :] <= tile_row[:, None]).astype(i32), axis=1), E - 1).astype(i32)
    flat = lambda a: a.reshape(-1).astype(i32)
    return dict(lstart=flat(lstart), nchunk=flat(nchunk), gbase=flat(gbase),
                tail_start=flat(tail_start), tail_chunks=flat(tail_chunks),
                n_used=n_used, tile_expert=tile_expert)


def _block_diag(w):
    g, a, b = w.shape
    out = jnp.zeros((g * a, g * b), w.dtype)
    for i in range(g):
        out = out.at[i * a:(i + 1) * a, i * b:(i + 1) * b].set(w[i])
    return out


def kernel(x, c, w_mod, b_mod, norm1_g, w_in, conv_w, conv_b, conv_norm_g, conv_norm_b, rel_bias, pool_w, pool_scale, sgu_norm_g, sgu_norm_b, sgu_w, sgu_b, mix_out_g, w_out, norm2_g, router_w, router_b, exp_w1, exp_b1, exp_w2, exp_b2, final_norm_g):
    B, S, D = x.shape
    L = w_mod.shape[0]
    N = B * S
    nw = N // ROUTE_TILE
    assert S % SEQ_TILE == 0 and S % MOBA_BLOCK == 0 and N % ROUTE_TILE == 0 and S % ROUTE_TILE == 0
    p_bound = N * TOP_K + nw * N_EXPERTS * (ROW_CHUNK - 1) + N_EXPERTS * (FFN_TILE - 1)
    n_tiles = -(-p_bound // FFN_TILE)
    p_rows = n_tiles * FFN_TILE

    mod = _modulation(c, w_mod, b_mod)
    bias_tiles = _bias_tiles(rel_bias)
    far_bucket = int(_t5_bucket_table(MOBA_BLOCK + 2)[MOBA_BLOCK + 1])
    assert far_bucket == int(_t5_bucket_table(S + 1)[S])
    far_bias = rel_bias[far_bucket].astype(f32)
    row = lambda a: a.reshape(1, -1)
    for l in range(L):
        m6 = mod[l].reshape(B, 6, 1, D)
        shift1, scale1, gate1, shift2, scale2, gate2 = (m6[:, j] for j in range(6))
        mg = row(mix_out_g[l])
        pc, qT, k, vT, kmean, pp, ps = _in_projection(x, shift1, scale1, row(norm1_g[l]), w_in[l].astype(bf16))
        yc, yp, ys = _local_mixers(
            pc, pp, ps, conv_w[l], row(conv_b[l]), row(conv_norm_g[l]), row(conv_norm_b[l]),
            _block_diag(pool_w[l]).astype(bf16), row(pool_scale[l]),
            row(sgu_norm_g[l]), row(sgu_norm_b[l]), sgu_w[l],
            jnp.repeat(sgu_b[l].T, HEAD_DIM, axis=1), mg)
        ya = _moba_attention(qT, k, vT, kmean.reshape(B, -1, GROUP_WIDTH), bias_tiles, far_bias, mg)
        flat = lambda a: a.reshape(N, -1)
        x1, h2, destT, gateT, padded = _outproj_router(
            flat(yc), flat(ya), flat(yp), flat(ys), x.reshape(N, D), gate1, w_out[l].astype(bf16),
            row(norm2_g[l]), shift2, scale2, router_w[l].T, router_b[l].reshape(-1, 1), S)
        t = _routing_tables(padded.reshape(nw, N_EXPERTS), n_tiles)
        xs = _dispatch(h2, destT, t['lstart'], t['nchunk'], t['gbase'],
                       t['tail_start'], t['tail_chunks'], t['n_used'], p_rows)
        ysort = _expert_ffn(xs, exp_w1, exp_b1, exp_w2, exp_b2, t['tile_expert'], t['n_used'], l)
        x = _combine(ysort, destT, gateT, x1, gate2, row(final_norm_g),
                     t['lstart'], t['nchunk'], t['gbase'], S, final=(l == L - 1)).reshape(B, S, D)
    return x
```

```python
import functools
import math

import numpy as np
import jax
import jax.numpy as jnp
from jax import lax
from jax.experimental import pallas as pl
from jax.experimental.pallas import tpu as pltpu

f32, bf16, i32 = jnp.float32, jnp.bfloat16, jnp.int32

GROUP_WIDTH = 256
HEADS = 4
HEAD_DIM = 64
CONV_WIDTH = 31
MOBA_BLOCK = 256
MOBA_TOPK = 3
Q_CHUNK = 128
N_BUCKETS = 32
T5_MAX_DISTANCE = 128
POOL_WINDOWS = (2, 4, 8, 16)
SGU_CHUNK = 128
N_EXPERTS = 32
TOP_K = 4
SWIGLU_LIMIT = 7.0
SWIGLU_ALPHA = 1.702
EPS = 1e-6

SUBLANES = 8
HALO = 32
SEQ_TILE = 512
ROUTE_TILE = 256
ROUTE_WINDOWS_PER_STEP = 2
ROW_CHUNK = 8
FFN_TILE = 512
NEG = -1e30
VMEM_LIMIT = 56 * 1024 * 1024


def _cparams(sem):
    return pltpu.CompilerParams(dimension_semantics=sem, vmem_limit_bytes=VMEM_LIMIT)


def _split_bf16(a):
    hi = a.astype(bf16)
    lo = (a - hi.astype(f32)).astype(bf16)
    return hi, lo


def _mod_kernel(c_ref, w_ref, b_ref, o_ref):
    c = c_ref[...]
    cond = c * jax.nn.sigmoid(c)
    o_ref[0] = jnp.dot(cond, w_ref[0], preferred_element_type=f32,
                       precision=lax.Precision.HIGHEST) + b_ref[0]


def _modulation(c, w_mod, b_mod):
    L, D, M = w_mod.shape
    B = c.shape[0]
    tn = 1536
    return pl.pallas_call(
        _mod_kernel,
        grid=(L, M // tn),
        in_specs=[pl.BlockSpec((B, D), lambda l, j: (0, 0)),
                  pl.BlockSpec((1, D, tn), lambda l, j: (l, 0, j)),
                  pl.BlockSpec((1, 1, tn), lambda l, j: (l, 0, j))],
        out_specs=pl.BlockSpec((1, B, tn), lambda l, j: (l, 0, j)),
        out_shape=jax.ShapeDtypeStruct((L, B, M), f32),
        compiler_params=_cparams(("arbitrary", "arbitrary")),
        name="modulation",
    )(c, w_mod, b_mod.reshape(L, 1, M))


def _inproj_kernel(x_ref, sh_ref, sc_ref, g_ref, w_ref,
                   pc_ref, qT_ref, k_ref, vT_ref, km_ref, pp_ref, ps_ref):
    x = x_ref[0]
    ms = jnp.mean(x * x, axis=-1, keepdims=True)
    h = x * lax.rsqrt(ms + EPS) * g_ref[...]
    h = h * (1.0 + sc_ref[0]) + sh_ref[0]
    proj = jnp.dot(h.astype(bf16), w_ref[...], preferred_element_type=f32)
    gw = GROUP_WIDTH
    pc_ref[0] = proj[:, 0:2 * gw]
    q = proj[:, 2 * gw:3 * gw] * (HEAD_DIM ** -0.5)
    qT_ref[0] = q.T.astype(bf16)
    kk = proj[:, 3 * gw:4 * gw]
    for h in range(HEADS):
        k_ref[0, h] = kk[:, h * HEAD_DIM:(h + 1) * HEAD_DIM].astype(bf16)
    for j in range(SEQ_TILE // MOBA_BLOCK):
        km_ref[0, 0, j:j + 1, :] = jnp.mean(kk[j * MOBA_BLOCK:(j + 1) * MOBA_BLOCK], axis=0, keepdims=True)
    for j in range(SEQ_TILE // MOBA_BLOCK):
        vT_ref[0, j] = proj[j * MOBA_BLOCK:(j + 1) * MOBA_BLOCK, 4 * gw:5 * gw].T.astype(bf16)
    pp_ref[0] = proj[:, 5 * gw:6 * gw]
    ps_ref[0] = proj[:, 6 * gw:8 * gw]


def _in_projection(x, shift, scale, g, w_bf):
    B, S, D = x.shape
    gw = GROUP_WIDTH
    ts = SEQ_TILE
    nt = S // ts
    row = lambda b, i: (b, i, 0)
    col = lambda b, i: (b, 0, i)
    vec = lambda b, i: (b, 0, 0)
    return pl.pallas_call(
        _inproj_kernel,
        grid=(B, nt),
        in_specs=[pl.BlockSpec((1, ts, D), row),
                  pl.BlockSpec((1, 1, D), vec),
                  pl.BlockSpec((1, 1, D), vec),
                  pl.BlockSpec((1, D), lambda b, i: (0, 0)),
                  pl.BlockSpec(w_bf.shape, lambda b, i: (0, 0))],
        out_specs=[pl.BlockSpec((1, ts, 2 * gw), row),
                   pl.BlockSpec((1, gw, ts), col),
                   pl.BlockSpec((1, HEADS, ts, HEAD_DIM), lambda b, i: (b, 0, i, 0)),
                   pl.BlockSpec((1, ts // MOBA_BLOCK, gw, MOBA_BLOCK), lambda b, i: (b, i, 0, 0)),
                   pl.BlockSpec((1, 1, ts // MOBA_BLOCK, gw), lambda b, i: (b, i, 0, 0)),
                   pl.BlockSpec((1, ts, gw), row),
                   pl.BlockSpec((1, ts, 2 * gw), row)],
        out_shape=[jax.ShapeDtypeStruct((B, S, 2 * gw), f32),
                   jax.ShapeDtypeStruct((B, gw, S), bf16),
                   jax.ShapeDtypeStruct((B, HEADS, S, HEAD_DIM), bf16),
                   jax.ShapeDtypeStruct((B, S // MOBA_BLOCK, gw, MOBA_BLOCK), bf16),
                   jax.ShapeDtypeStruct((B, nt, ts // MOBA_BLOCK, gw), f32),
                   jax.ShapeDtypeStruct((B, S, gw), f32),
                   jax.ShapeDtypeStruct((B, S, 2 * gw), f32)],
        compiler_params=_cparams(("arbitrary", "arbitrary")),
        name="in_projection",
    )(x, shift, scale, g, w_bf)


def _group_rms(y, g):
    return y * lax.rsqrt(jnp.mean(y * y, axis=-1, keepdims=True) + EPS) * g


def _local_kernel(pc_ref, pcp_ref, pp_ref, ppp_ref, ps_ref,
                  cw_ref, cb_ref, cng_ref, cnb_ref, pw_ref, psc_ref,
                  sg_ref, sb_ref, sw_ref, sbias_ref, mg_ref,
                  yc_ref, yp_ref, ys_ref,
                  gext, gshift, zext, s2, s4, s8):
    i = pl.program_id(1)
    ts = SEQ_TILE
    gw = GROUP_WIDTH
    first = i == 0
    lane = lax.broadcasted_iota(i32, (1, gw), 1)

    pc = pc_ref[0]
    g = pc[:, :gw] * jax.nn.sigmoid(pc[:, gw:])
    ph = pcp_ref[0]
    gh = ph[:, :gw] * jax.nn.sigmoid(ph[:, gw:])
    gext[0:HALO, :] = jnp.where(first, 0.0, gh)
    gext[HALO:HALO + ts, :] = g
    span = ts + HALO - SUBLANES
    for a in range(1, SUBLANES):
        gshift[a - 1, 0:span, :] = gext[pl.ds(a, span), :]
    acc = jnp.zeros((ts, gw), f32)
    for j in range(CONV_WIDTH):
        b, a = divmod(HALO - (CONV_WIDTH - 1) + j, SUBLANES)
        tap = gext[pl.ds(SUBLANES * b, ts), :] if a == 0 else gshift[a - 1, pl.ds(SUBLANES * b, ts), :]
        acc = acc + cw_ref[j:j + 1, :] * tap
    y = acc + cb_ref[...]
    r = lax.broadcasted_iota(i32, (gw, gw), 0) // HEAD_DIM
    c = lax.broadcasted_iota(i32, (gw, gw), 1) // HEAD_DIM
    avg = jnp.where(r == c, 1.0 / HEAD_DIM, 0.0).astype(bf16)

    def head_mean(t):
        hi, lo = _split_bf16(t)
        return (jnp.dot(hi, avg, preferred_element_type=f32)
                + jnp.dot(lo, avg, preferred_element_type=f32))

    mu = head_mean(y)
    yc = y - mu
    var = head_mean(yc * yc)
    yn = yc * lax.rsqrt(var + EPS) * cng_ref[...] + cnb_ref[...]
    yconv = yn * jax.nn.sigmoid(yn)
    yc_ref[0] = _group_rms(yconv, mg_ref[:, 0:gw]).astype(bf16)

    z = pp_ref[0]
    zext[0:HALO, :] = jnp.where(first, 0.0, ppp_ref[0])
    zext[HALO:HALO + ts, :] = z
    n2, n4, n8 = ts + 14, ts + 12, ts + 8
    s2[0:n2, :] = zext[pl.ds(HALO - 14, n2), :] + zext[pl.ds(HALO - 15, n2), :]
    s4[0:n4, :] = s2[pl.ds(2, n4), :] + s2[pl.ds(0, n4), :]
    s8[0:n8, :] = s4[pl.ds(4, n8), :] + s4[pl.ds(0, n8), :]
    w2 = s2[pl.ds(14, ts), :]
    w4 = s4[pl.ds(12, ts), :]
    w8 = s8[pl.ds(8, ts), :]
    w16 = w8 + s8[pl.ds(0, ts), :]
    tpos = (i * ts + lax.broadcasted_iota(i32, (ts, 1), 0) + 1).astype(f32)
    grp = lane // (gw // len(POOL_WINDOWS))
    pooled = jnp.zeros((ts, gw), f32)
    for gi, (w, sw) in enumerate(zip(POOL_WINDOWS, (w2, w4, w8, w16))):
        pooled = jnp.where(grp == gi, sw / jnp.minimum(tpos, float(w)), pooled)
    pooled = pooled - z
    yp = jnp.dot(pooled.astype(bf16), pw_ref[...], preferred_element_type=f32) * psc_ref[...]
    yp_ref[0] = _group_rms(yp, mg_ref[:, 2 * gw:3 * gw]).astype(bf16)

    zz = ps_ref[0]
    zz = 0.5 * zz * (1.0 + lax.erf(zz * (1.0 / math.sqrt(2.0))))
    u = zz[:, :gw]
    v = zz[:, gw:]
    vm = jnp.mean(v, axis=-1, keepdims=True)
    vc = v - vm
    vv = jnp.mean(vc * vc, axis=-1, keepdims=True)
    vn = (vc * lax.rsqrt(vv + EPS) * sg_ref[...] + sb_ref[...]).astype(bf16)
    li = lax.broadcasted_iota(i32, (SGU_CHUNK, SGU_CHUNK), 0)
    lj = lax.broadcasted_iota(i32, (SGU_CHUNK, SGU_CHUNK), 1)
    head_of_lane = lane // HEAD_DIM
    wts = [jnp.where(li >= lj, sw_ref[h], 0.0).astype(bf16) for h in range(HEADS)]
    outs = []
    for n in range(ts // SGU_CHUNK):
        vch = vn[n * SGU_CHUNK:(n + 1) * SGU_CHUNK]
        mixed = sbias_ref[...]
        for h in range(HEADS):
            mh = jnp.dot(wts[h], vch, preferred_element_type=f32)
            mixed = mixed + jnp.where(head_of_lane == h, mh, 0.0)
        outs.append(u[n * SGU_CHUNK:(n + 1) * SGU_CHUNK] * mixed)
    ysgu = jnp.concatenate(outs, axis=0)
    ys_ref[0] = _group_rms(ysgu, mg_ref[:, 3 * gw:4 * gw]).astype(bf16)


def _local_mixers(pc, pp, ps, cw, cb, cng, cnb, pw_bd, psc, sg, sb, sw, sbias, mg):
    B, S, _ = pc.shape
    gw = GROUP_WIDTH
    ts = SEQ_TILE
    hb = ts // HALO
    row = lambda b, i: (b, i, 0)
    prev = lambda b, i: (b, jnp.maximum(i * hb - 1, 0), 0)
    full2 = lambda b, i: (0, 0)
    full3 = lambda b, i: (0, 0, 0)
    out = jax.ShapeDtypeStruct((B, S, gw), bf16)
    return pl.pallas_call(
        _local_kernel,
        grid=(B, S // ts),
        in_specs=[pl.BlockSpec((1, ts, 2 * gw), row),
                  pl.BlockSpec((1, HALO, 2 * gw), prev),
                  pl.BlockSpec((1, ts, gw), row),
                  pl.BlockSpec((1, HALO, gw), prev),
                  pl.BlockSpec((1, ts, 2 * gw), row),
                  pl.BlockSpec(cw.shape, full2), pl.BlockSpec(cb.shape, full2),
                  pl.BlockSpec(cng.shape, full2), pl.BlockSpec(cnb.shape, full2),
                  pl.BlockSpec(pw_bd.shape, full2), pl.BlockSpec(psc.shape, full2),
                  pl.BlockSpec(sg.shape, full2), pl.BlockSpec(sb.shape, full2),
                  pl.BlockSpec(sw.shape, full3), pl.BlockSpec(sbias.shape, full2),
                  pl.BlockSpec(mg.shape, full2)],
        out_specs=[pl.BlockSpec((1, ts, gw), row)] * 3,
        out_shape=[out, out, out],
        scratch_shapes=[pltpu.VMEM((ts + HALO, gw), f32), pltpu.VMEM((SUBLANES - 1, ts + HALO, gw), f32),
                        pltpu.VMEM((ts + HALO, gw), f32),
                        pltpu.VMEM((ts + 16, gw), f32), pltpu.VMEM((ts + 16, gw), f32),
                        pltpu.VMEM((ts + 16, gw), f32)],
        compiler_params=_cparams(("arbitrary", "arbitrary")),
        name="local_mixers",
    )(pc, pc, pp, pp, ps, cw, cb, cng, cnb, pw_bd, psc, sg, sb, sw, sbias, mg)


def _t5_bucket_table(max_dist):
    d = np.arange(max_dist, dtype=np.int64)
    max_exact = N_BUCKETS // 2
    nf = np.maximum(d, 1).astype(np.float32)
    large = max_exact + (np.log(nf / np.float32(max_exact)) / np.float32(math.log(T5_MAX_DISTANCE / max_exact))
                         * np.float32(N_BUCKETS - max_exact)).astype(np.int32)
    large = np.minimum(large, N_BUCKETS - 1)
    return np.where(d < max_exact, d, large).astype(np.int32)


_TILE_BASES = (0, MOBA_BLOCK)


def _bias_kernel(tab_ref, o_ref):
    blk, qc = MOBA_BLOCK, MOBA_BLOCK
    table = _t5_bucket_table(2 * blk + qc)
    first = [int(np.argmax(table >= b)) for b in range(N_BUCKETS)]
    j = lax.broadcasted_iota(i32, (blk, qc), 0)
    q = lax.broadcasted_iota(i32, (blk, qc), 1)
    for t, base in enumerate(_TILE_BASES):
        d = base + q - j
        lo, hi = max(base - (blk - 1), 0), base + qc - 1
        for h in range(HEADS):
            val = jnp.full((blk, qc), tab_ref[h], f32)
            for b in range(1, N_BUCKETS):
                if first[b] > hi:
                    continue
                if first[b] <= lo:
                    val = jnp.full((blk, qc), tab_ref[b * HEADS + h], f32)
                else:
                    val = jnp.where(d >= first[b], tab_ref[b * HEADS + h], val)
            o_ref[t, h] = jnp.where(d >= 0, val, NEG)


def _bias_tiles(rel_bias):
    return pl.pallas_call(
        _bias_kernel,
        in_specs=[pl.BlockSpec(memory_space=pltpu.SMEM)],
        out_shape=jax.ShapeDtypeStruct((len(_TILE_BASES), HEADS, MOBA_BLOCK, MOBA_BLOCK), f32),
        name="bias_tiles",
    )(rel_bias.astype(f32).reshape(-1))


def _attn_kernel(far_ref, qT_ref, k_ref, vT_ref, km_ref, bias_ref, mg_ref, o_ref, mask_ref, s_ref):
    own = pl.program_id(1)
    blk, gw, hd = MOBA_BLOCK, GROUP_WIDTH, HEAD_DIM
    nb = km_ref.shape[1]

    nio = lax.broadcasted_iota(i32, (nb, blk), 0)
    past = nio < own
    km = km_ref[0]
    q_heads = []
    for h in range(HEADS):
        qh = qT_ref[0, h * hd:(h + 1) * hd, :]
        q_heads.append(qh)
        km_hi, km_lo = _split_bf16(km[:, h * hd:(h + 1) * hd])
        gate = (jnp.dot(km_hi, qh, preferred_element_type=f32)
                + jnp.dot(km_lo, qh, preferred_element_type=f32))
        gate = jnp.where(past, gate, -jnp.inf)
        rank = jnp.zeros((nb, blk), i32)
        for m in range(nb):
            gm = gate[m:m + 1, :]
            beats = (gm > gate) | ((gm == gate) & (m < nio))
            rank = rank + beats.astype(i32)
        sel = (rank < MOBA_TOPK) & past
        mask_ref[0, h] = jnp.where(sel, 0.0, NEG)
        mask_ref[1, h] = jnp.where(sel, far_ref[h], NEG)

    def qk(n, h):
        kb = k_ref[0, h, pl.ds(pl.multiple_of(n * blk, blk), blk), :]
        return jnp.dot(kb, q_heads[h], preferred_element_type=f32)

    def far_scores(n, h):
        return qk(n, h) + mask_ref[1, h, pl.ds(n, 1), :]

    def update(state, scores, n):
        out = []
        for h in range(HEADS):
            m, l, acc = state[3 * h:3 * h + 3]
            s = scores[h]
            m_new = jnp.maximum(m, jnp.max(s, axis=0, keepdims=True))
            alpha = jnp.exp(m - m_new)
            p = jnp.exp(s - m_new)
            l = alpha * l + jnp.sum(p, axis=0, keepdims=True)
            vb = vT_ref[0, n, h * hd:(h + 1) * hd, :]
            acc = acc * alpha + jnp.dot(vb, p.astype(bf16), preferred_element_type=f32)
            out += [m_new, l, acc]
        return tuple(out)

    adj = jnp.maximum(own - 1, 0)
    n_far = jnp.maximum(own - 1, 0)
    s_own = [qk(own, h) + bias_ref[0, h] for h in range(HEADS)]
    s_adj = [qk(adj, h) + bias_ref[1, h] + mask_ref[0, h, pl.ds(adj, 1), :] for h in range(HEADS)]
    for h in range(HEADS):
        s_ref[0, h] = far_scores(0, h)

    state = []
    for h in range(HEADS):
        m0 = jnp.max(s_own[h], axis=0, keepdims=True)
        p = jnp.exp(s_own[h] - m0)
        vb = vT_ref[0, own, h * hd:(h + 1) * hd, :]
        state += [m0, jnp.sum(p, axis=0, keepdims=True),
                  jnp.dot(vb, p.astype(bf16), preferred_element_type=f32)]
    state = update(tuple(state), s_adj, adj)

    def body(n, state):
        slot = n % 2
        cur = [s_ref[slot, h] for h in range(HEADS)]
        nxt = jnp.minimum(n + 1, jnp.maximum(n_far - 1, 0))
        for h in range(HEADS):
            s_ref[1 - slot, h] = far_scores(nxt, h)
        return update(state, cur, n)

    fin = lax.fori_loop(0, n_far, body, state)
    outT = jnp.concatenate([fin[3 * h + 2] / fin[3 * h + 1] for h in range(HEADS)], axis=0)
    o_ref[0] = _group_rms(outT.T, mg_ref[:, gw:2 * gw]).astype(bf16)


def _moba_attention(qT, k, vT, kmean, bias_tiles, far_bias, mg):
    B, _, S, _ = k.shape
    gw, blk = GROUP_WIDTH, MOBA_BLOCK
    nb = kmean.shape[1]
    return pl.pallas_call(
        _attn_kernel,
        grid=(B, nb),
        in_specs=[pl.BlockSpec(memory_space=pltpu.SMEM),
                  pl.BlockSpec((1, gw, blk), lambda b, c: (b, 0, c)),
                  pl.BlockSpec((1, HEADS, S, HEAD_DIM), lambda b, c: (b, 0, 0, 0)),
                  pl.BlockSpec((1, nb, gw, blk), lambda b, c: (b, 0, 0, 0)),
                  pl.BlockSpec((1, nb, gw), lambda b, c: (b, 0, 0)),
                  pl.BlockSpec(bias_tiles.shape, lambda b, c: (0, 0, 0, 0)),
                  pl.BlockSpec(mg.shape, lambda b, c: (0, 0))],
        out_specs=pl.BlockSpec((1, blk, gw), lambda b, c: (b, c, 0)),
        out_shape=jax.ShapeDtypeStruct((B, S, gw), bf16),
        scratch_shapes=[pltpu.VMEM((2, HEADS, nb, blk), f32), pltpu.VMEM((2, HEADS, blk, blk), f32)],
        compiler_params=_cparams(("arbitrary", "arbitrary")),
        name="moba_attention",
    )(far_bias, qT, k, vT, kmean, bias_tiles, mg)


def _outproj_router_kernel(yc_ref, ya_ref, yp_ref, ys_ref, x_ref, g1_ref, wo_ref, n2_ref, sh_ref, sc_ref,
                           rw_ref, rb_ref,
                           x1_ref, h2_ref, dest_ref, gate_ref, pad_ref):
    W = ROUTE_TILE
    mixed = [jnp.concatenate([r[u * W:(u + 1) * W, :] for r in (yc_ref, ya_ref, yp_ref, ys_ref)], axis=1)
             for u in range(ROUTE_WINDOWS_PER_STEP)]
    projected = [jnp.dot(m, wo_ref[...], preferred_element_type=f32) for m in mixed]
    for u in range(ROUTE_WINDOWS_PER_STEP):
        _route_window(u, projected[u], x_ref, g1_ref, n2_ref, sh_ref, sc_ref, rw_ref, rb_ref,
                      x1_ref, h2_ref, dest_ref, gate_ref, pad_ref)


def _route_window(u, projected, x_ref, g1_ref, n2_ref, sh_ref, sc_ref, rw_ref, rb_ref,
                  x1_ref, h2_ref, dest_ref, gate_ref, pad_ref):
    W = ROUTE_TILE
    rows = slice(u * W, (u + 1) * W)
    x1 = x_ref[rows, :] + g1_ref[0] * projected
    x1_ref[rows, :] = x1
    ms = jnp.mean(x1 * x1, axis=-1, keepdims=True)
    h = x1 * lax.rsqrt(ms + EPS) * n2_ref[...]
    h = h * (1.0 + sc_ref[0]) + sh_ref[0]
    h_hi, h_lo = _split_bf16(h)
    h2_ref[rows, :] = h_hi

    nt = (((1,), (1,)), ((), ()))
    rw_hi, rw_lo = _split_bf16(rw_ref[...])
    logits = (lax.dot_general(rw_hi, h_hi, nt, preferred_element_type=f32)
              + lax.dot_general(rw_hi, h_lo, nt, preferred_element_type=f32)
              + lax.dot_general(rw_lo, h_hi, nt, preferred_element_type=f32)) + rb_ref[...]
    eio = lax.broadcasted_iota(i32, (N_EXPERTS, W), 0)
    work = logits
    vals, sels = [], []
    for k in range(TOP_K):
        m = jnp.max(work, axis=0, keepdims=True)
        idx = jnp.min(jnp.where(work == m, eio, N_EXPERTS), axis=0, keepdims=True)
        sel = eio == idx
        vals.append(m)
        sels.append(sel)
        work = jnp.where(sel, -jnp.inf, work)
    exps = [jnp.exp(v - vals[0]) for v in vals]
    denom = exps[0] + exps[1] + exps[2] + exps[3]
    for k in range(TOP_K):
        gate_ref[k:k + 1, rows] = exps[k] / denom

    multi = jnp.zeros((N_EXPERTS, W), f32)
    for sel in sels:
        multi = multi + sel.astype(f32)
    before = (lax.broadcasted_iota(i32, (W, W), 0) < lax.broadcasted_iota(i32, (W, W), 1)).astype(bf16)
    earlier = jnp.dot(multi.astype(bf16), before, preferred_element_type=f32)
    cnt = jnp.sum(multi, axis=1, keepdims=True).astype(i32)
    padded = (cnt + (ROW_CHUNK - 1)) // ROW_CHUNK * ROW_CHUNK
    pad_ref[u] = padded
    lower = (lax.broadcasted_iota(i32, (N_EXPERTS, N_EXPERTS), 1)
             < lax.broadcasted_iota(i32, (N_EXPERTS, N_EXPERTS), 0)).astype(bf16)
    seg_start = jnp.dot(lower, jnp.broadcast_to(padded.astype(f32), (N_EXPERTS, W)).astype(bf16),
                        preferred_element_type=f32)
    row = seg_start + earlier
    for k in range(TOP_K):
        dest_ref[k:k + 1, rows] = jnp.sum(jnp.where(sels[k], row, 0.0), axis=0, keepdims=True).astype(i32)


def _outproj_router(yc, ya, yp, ys, x, gate1, wo_bf, n2g, shift2, scale2, rwT, rb, seq):
    N, D = x.shape
    gw = GROUP_WIDTH
    nw = N // ROUTE_TILE
    wps = ROUTE_WINDOWS_PER_STEP
    W = ROUTE_TILE * wps
    per_b = seq // W
    row = lambda i: (i, 0)
    vec = lambda i: (i // per_b, 0, 0)
    full = lambda i: (0, 0)
    colblk = lambda i: (0, i)
    return pl.pallas_call(
        _outproj_router_kernel,
        grid=(nw // wps,),
        in_specs=[pl.BlockSpec((W, gw), row)] * 4 + [
            pl.BlockSpec((W, D), row),
            pl.BlockSpec((1, 1, D), vec),
            pl.BlockSpec(wo_bf.shape, full),
            pl.BlockSpec((1, D), full),
            pl.BlockSpec((1, 1, D), vec),
            pl.BlockSpec((1, 1, D), vec),
            pl.BlockSpec(rwT.shape, full),
            pl.BlockSpec(rb.shape, full)],
        out_specs=[pl.BlockSpec((W, D), row),
                   pl.BlockSpec((W, D), row),
                   pl.BlockSpec((TOP_K, W), colblk),
                   pl.BlockSpec((TOP_K, W), colblk),
                   pl.BlockSpec((wps, N_EXPERTS, 1), lambda i: (i, 0, 0))],
        out_shape=[jax.ShapeDtypeStruct((N, D), f32),
                   jax.ShapeDtypeStruct((N, D), bf16),
                   jax.ShapeDtypeStruct((TOP_K, N), i32),
                   jax.ShapeDtypeStruct((TOP_K, N), f32),
                   jax.ShapeDtypeStruct((nw, N_EXPERTS, 1), i32)],
        compiler_params=_cparams(("arbitrary",)),
        name="outproj_router",
    )(yc, ya, yp, ys, x, gate1, wo_bf, n2g, shift2, scale2, rwT, rb)


def _max_window_rows():
    return -(-(ROUTE_TILE * TOP_K + N_EXPERTS * (ROW_CHUNK - 1)) // 128) * 128


def _pack_pairs(lo, hi):
    lo_bits = lax.bitcast_convert_type(lo, jnp.uint32)
    hi_bits = lax.bitcast_convert_type(hi, jnp.uint32)
    return lax.bitcast_convert_type(hi_bits | (lo_bits >> 16), i32)


def _unpack_pairs(words):
    bits = lax.bitcast_convert_type(words, jnp.uint32)
    lo = lax.bitcast_convert_type(bits << 16, f32).astype(bf16)
    hi = lax.bitcast_convert_type(bits & jnp.uint32(0xFFFF0000), f32).astype(bf16)
    return lo, hi


def _dispatch_kernel(lstart_s, nchunk_s, gbase_s, wchunks_s, tail_start_s, tail_chunks_s, n_used_s,
                     h_ref, dest_ref, xs_ref, sorted_ref, zero_ref, sem):
    w = pl.program_id(0)
    nw = pl.num_programs(0)
    W = ROUTE_TILE
    R = sorted_ref.shape[1]
    half = sorted_ref.shape[2]
    slot = w % 2
    rio = lax.broadcasted_iota(i32, (R, W), 0)
    hit = rio == dest_ref[0:1, :]
    for k in range(1, TOP_K):
        hit = hit | (rio == dest_ref[k:k + 1, :])
    onehot = jnp.where(hit, 1.0, 0.0).astype(bf16)
    rows = jnp.dot(onehot, h_ref[...], preferred_element_type=f32)
    sorted_ref[slot] = _pack_pairs(rows[:, :half], rows[:, half:])

    def chunk_copy(s, src_row, dst_row):
        return pltpu.make_async_copy(
            sorted_ref.at[s, pl.ds(pl.multiple_of(src_row, ROW_CHUNK), ROW_CHUNK), :],
            xs_ref.at[pl.ds(pl.multiple_of(dst_row, ROW_CHUNK), ROW_CHUNK), :], sem.at[s])

    def zero_copy(dst_row):
        return pltpu.make_async_copy(
            zero_ref.at[0:ROW_CHUNK, :],
            xs_ref.at[pl.ds(pl.multiple_of(dst_row, ROW_CHUNK), ROW_CHUNK), :], sem.at[0])

    def zero_tile_copy(tile):
        return pltpu.make_async_copy(
            zero_ref, xs_ref.at[pl.ds(pl.multiple_of(tile * FFN_TILE, FFN_TILE), FFN_TILE), :], sem.at[0])

    def drain(s, count):
        def one(j, c):
            chunk_copy(s, 0, 0).wait()
            return c
        lax.fori_loop(0, count, one, 0)

    @pl.when(w > 0)
    def _():
        drain(1 - slot, wchunks_s[jnp.maximum(w - 1, 0)])

    def per_expert(e, c):
        n = nchunk_s[w * N_EXPERTS + e]
        src = lstart_s[w * N_EXPERTS + e]
        dst = gbase_s[w * N_EXPERTS + e]

        def issue(j, c):
            chunk_copy(slot, src + j * ROW_CHUNK, dst + j * ROW_CHUNK).start()
            return c
        lax.fori_loop(0, n, issue, 0)
        return c

    lax.fori_loop(0, N_EXPERTS, per_expert, 0)

    @pl.when(w == nw - 1)
    def _():
        drain(slot, wchunks_s[w])
        zero_ref[...] = jnp.zeros(zero_ref.shape, i32)

        def per_tail(e, total):
            n = tail_chunks_s[e]
            dst = tail_start_s[e]

            def issue(j, c):
                zero_copy(dst + j * ROW_CHUNK).start()
                return c
            lax.fori_loop(0, n, issue, 0)
            return total + n
        tails = lax.fori_loop(0, N_EXPERTS, per_tail, 0)

        def drain_tail(j, c):
            zero_copy(0).wait()
            return c
        lax.fori_loop(0, tails, drain_tail, 0)

        n_tiles = xs_ref.shape[0] // FFN_TILE

        def issue_tile(j, c):
            zero_tile_copy(j).start()
            return c
        lax.fori_loop(n_used_s[0], n_tiles, issue_tile, 0)

        def drain_tile(j, c):
            zero_tile_copy(0).wait()
            return c
        lax.fori_loop(n_used_s[0], n_tiles, drain_tile, 0)


def _dispatch(h2, destT, t, p_rows):
    N, D = h2.shape
    W = ROUTE_TILE
    nw = N // W
    R = _max_window_rows()
    grid_spec = pltpu.PrefetchScalarGridSpec(
        num_scalar_prefetch=7,
        grid=(nw,),
        in_specs=[pl.BlockSpec((W, D), lambda w, *_: (w, 0)),
                  pl.BlockSpec((TOP_K, W), lambda w, *_: (0, w))],
        out_specs=pl.BlockSpec(memory_space=pl.ANY),
        scratch_shapes=[pltpu.VMEM((2, R, D // 2), i32), pltpu.VMEM((FFN_TILE, D // 2), i32),
                        pltpu.SemaphoreType.DMA((2,))],
    )
    return pl.pallas_call(
        _dispatch_kernel,
        grid_spec=grid_spec,
        out_shape=jax.ShapeDtypeStruct((p_rows, D // 2), i32),
        compiler_params=_cparams(("arbitrary",)),
        name="expert_dispatch",
    )(t['lstart'], t['nchunk'], t['gbase'], t['wchunks'], t['tail_start'], t['tail_chunks'], t['n_used'],
      h2, destT)


def _ffn_kernel(tile_expert_s, n_used_s, x_ref, w1_ref, b1_ref, w2_ref, b2_ref, y_ref, w1b_ref, w2b_ref):
    i = pl.program_id(0)
    last = n_used_s[0] - 1
    expert = tile_expert_s[jnp.minimum(i, last)]
    prev_expert = tile_expert_s[jnp.minimum(jnp.maximum(i - 1, 0), last)]

    @pl.when((i == 0) | (expert != prev_expert))
    def _():
        w1b_ref[...] = w1_ref[0, 0].astype(bf16)
        w2b_ref[...] = w2_ref[0, 0].astype(bf16)

    @pl.when(i < n_used_s[0])
    def _():
        dff = w2b_ref.shape[0]
        half = x_ref.shape[1]
        x_lo, x_hi = _unpack_pairs(x_ref[...])
        hh = (jnp.dot(x_lo, w1b_ref[0:half, :], preferred_element_type=f32)
              + jnp.dot(x_hi, w1b_ref[half:2 * half, :], preferred_element_type=f32)) + b1_ref[0, 0]
        x_glu = jnp.minimum(hh[:, :dff], SWIGLU_LIMIT)
        x_lin = jnp.clip(hh[:, dff:], -SWIGLU_LIMIT, SWIGLU_LIMIT)
        act = x_glu * jax.nn.sigmoid(SWIGLU_ALPHA * x_glu) * (x_lin + 1.0)
        y = jnp.dot(act.astype(bf16), w2b_ref[...], preferred_element_type=f32) + b2_ref[0, 0]
        y = y.astype(bf16).astype(f32)
        y_ref[...] = _pack_pairs(y[:, :half], y[:, half:])

    @pl.when(i >= n_used_s[0])
    def _():
        y_ref[...] = jnp.zeros(y_ref.shape, i32)


def _expert_ffn(xs, w1, b1, w2, b2, tile_expert, n_used, layer):
    P, half = xs.shape
    L, E, D, F2 = w1.shape
    tm = FFN_TILE
    nt = P // tm

    def tile(i, te, nu):
        return (jnp.minimum(i, nu[0] - 1), 0)

    def expert4(i, te, nu):
        return (layer, te[jnp.minimum(i, nu[0] - 1)], 0, 0)

    grid_spec = pltpu.PrefetchScalarGridSpec(
        num_scalar_prefetch=2,
        grid=(nt,),
        in_specs=[pl.BlockSpec((tm, half), tile),
                  pl.BlockSpec((1, 1, D, F2), expert4),
                  pl.BlockSpec((1, 1, 1, F2), expert4),
                  pl.BlockSpec((1, 1, F2 // 2, D), expert4),
                  pl.BlockSpec((1, 1, 1, D), expert4)],
        out_specs=pl.BlockSpec((tm, half), lambda i, te, nu: (i, 0)),
        scratch_shapes=[pltpu.VMEM((D, F2), bf16), pltpu.VMEM((F2 // 2, D), bf16)],
    )
    return pl.pallas_call(
        _ffn_kernel,
        grid_spec=grid_spec,
        out_shape=jax.ShapeDtypeStruct((P, half), i32),
        compiler_params=_cparams(("arbitrary",)),
        name="expert_ffn",
    )(tile_expert, n_used, xs, w1, b1.reshape(L, E, 1, F2), w2, b2.reshape(L, E, 1, D))


def _combine_kernel(lstart_s, nchunk_s, gbase_s, wchunks_s,
                    ys_ref, dest_ref, gate_ref, x1_ref, g2_ref, fg_ref, o_ref, local_ref, sem, *, final):
    w = pl.program_id(0)
    nw = pl.num_programs(0)
    W = ROUTE_TILE
    R = local_ref.shape[1]
    slot = w % 2

    def chunk_copy(s, src_row, dst_row):
        return pltpu.make_async_copy(
            ys_ref.at[pl.ds(pl.multiple_of(src_row, ROW_CHUNK), ROW_CHUNK), :],
            local_ref.at[s, pl.ds(pl.multiple_of(dst_row, ROW_CHUNK), ROW_CHUNK), :], sem.at[s])

    def fetch(win, s):
        def per_expert(e, c):
            n = nchunk_s[win * N_EXPERTS + e]
            dst = lstart_s[win * N_EXPERTS + e]
            src = gbase_s[win * N_EXPERTS + e]

            def issue(j, c):
                chunk_copy(s, src + j * ROW_CHUNK, dst + j * ROW_CHUNK).start()
                return c
            lax.fori_loop(0, n, issue, 0)
            return c
        lax.fori_loop(0, N_EXPERTS, per_expert, 0)

    @pl.when(w == 0)
    def _():
        local_ref[...] = jnp.zeros(local_ref.shape, i32)
        fetch(0, 0)

    @pl.when(w + 1 < nw)
    def _():
        fetch(jnp.minimum(w + 1, nw - 1), 1 - slot)

    def drain(j, c):
        chunk_copy(slot, 0, 0).wait()
        return c
    lax.fori_loop(0, wchunks_s[w], drain, 0)

    rio = lax.broadcasted_iota(i32, (R, W), 0)
    weights = jnp.zeros((R, W), f32)
    for k in range(TOP_K):
        weights = jnp.where(rio == dest_ref[k:k + 1, :], gate_ref[k:k + 1, :], weights)
    weights = weights.astype(bf16)
    y_lo, y_hi = _unpack_pairs(local_ref[slot])
    tn = (((0,), (0,)), ((), ()))
    moe = jnp.concatenate([lax.dot_general(weights, y_lo, tn, preferred_element_type=f32),
                           lax.dot_general(weights, y_hi, tn, preferred_element_type=f32)], axis=1)
    x2 = x1_ref[...] + g2_ref[0] * moe
    if final:
        ms = jnp.mean(x2 * x2, axis=-1, keepdims=True)
        x2 = x2 * lax.rsqrt(ms + EPS) * fg_ref[...]
    o_ref[...] = x2


def _combine(ys, dest, gates, x1, gate2, final_g, t, seq, final):
    N, D = x1.shape
    W = ROUTE_TILE
    nw = N // W
    per_b = seq // W
    R = _max_window_rows()
    grid_spec = pltpu.PrefetchScalarGridSpec(
        num_scalar_prefetch=4,
        grid=(nw,),
        in_specs=[pl.BlockSpec(memory_space=pl.ANY),
                  pl.BlockSpec((TOP_K, W), lambda w, *_: (0, w)),
                  pl.BlockSpec((TOP_K, W), lambda w, *_: (0, w)),
                  pl.BlockSpec((W, D), lambda w, *_: (w, 0)),
                  pl.BlockSpec((1, 1, D), lambda w, *_: (w // per_b, 0, 0)),
                  pl.BlockSpec((1, D), lambda w, *_: (0, 0))],
        out_specs=pl.BlockSpec((W, D), lambda w, *_: (w, 0)),
        scratch_shapes=[pltpu.VMEM((2, R, D // 2), i32), pltpu.SemaphoreType.DMA((2,))],
    )
    return pl.pallas_call(
        functools.partial(_combine_kernel, final=final),
        grid_spec=grid_spec,
        out_shape=jax.ShapeDtypeStruct((N, D), f32),
        compiler_params=_cparams(("arbitrary",)),
        name="expert_combine",
    )(t['lstart'], t['nchunk'], t['gbase'], t['wchunks'], ys, dest, gates, x1, gate2, final_g)


def _routing_tables(padded, n_tiles):
    nw, E = padded.shape
    lstart = jnp.cumsum(padded, axis=1) - padded
    tot = jnp.sum(padded, axis=0)
    region = (tot + FFN_TILE - 1) // FFN_TILE * FFN_TILE
    region_end = jnp.cumsum(region)
    region_start = region_end - region
    gbase = region_start[None, :] + jnp.cumsum(padded, axis=0) - padded
    nchunk = padded // ROW_CHUNK
    tail_start = region_start + tot
    tail_chunks = (region - tot) // ROW_CHUNK
    n_used = (region_end[-1] // FFN_TILE).astype(i32).reshape(1)
    tile_row = jnp.arange(n_tiles, dtype=i32) * FFN_TILE
    tile_expert = jnp.minimum(
        jnp.sum((region_end[None, :] <= tile_row[:, None]).astype(i32), axis=1), E - 1).astype(i32)
    flat = lambda a: a.reshape(-1).astype(i32)
    return dict(lstart=flat(lstart), nchunk=flat(nchunk), gbase=flat(gbase),
                wchunks=flat(jnp.sum(nchunk, axis=1)),
                tail_start=flat(tail_start), tail_chunks=flat(tail_chunks),
                n_used=n_used, tile_expert=tile_expert)


def _block_diag(w):
    g, a, b = w.shape
    out = jnp.zeros((g * a, g * b), w.dtype)
    for i in range(g):
        out = out.at[i * a:(i + 1) * a, i * b:(i + 1) * b].set(w[i])
    return out


def kernel(x, c, w_mod, b_mod, norm1_g, w_in, conv_w, conv_b, conv_norm_g, conv_norm_b, rel_bias, pool_w, pool_scale, sgu_norm_g, sgu_norm_b, sgu_w, sgu_b, mix_out_g, w_out, norm2_g, router_w, router_b, exp_w1, exp_b1, exp_w2, exp_b2, final_norm_g):
    B, S, D = x.shape
    L = w_mod.shape[0]
    N = B * S
    nw = N // ROUTE_TILE
    assert S % SEQ_TILE == 0 and S % MOBA_BLOCK == 0 and N % ROUTE_TILE == 0 and S % ROUTE_TILE == 0
    p_bound = N * TOP_K + nw * N_EXPERTS * (ROW_CHUNK - 1) + N_EXPERTS * (FFN_TILE - 1)
    n_tiles = -(-p_bound // FFN_TILE)
    p_rows = n_tiles * FFN_TILE

    mod = _modulation(c, w_mod, b_mod)
    bias_tiles = _bias_tiles(rel_bias)
    far_bucket = int(_t5_bucket_table(MOBA_BLOCK + 2)[MOBA_BLOCK + 1])
    assert far_bucket == int(_t5_bucket_table(S + 1)[S])
    far_bias = rel_bias[far_bucket].astype(f32)
    row = lambda a: a.reshape(1, -1)
    for l in range(L):
        m6 = mod[l].reshape(B, 6, 1, D)
        shift1, scale1, gate1, shift2, scale2, gate2 = (m6[:, j] for j in range(6))
        mg = row(mix_out_g[l])
        pc, qT, k, vT, kmean, pp, ps = _in_projection(x, shift1, scale1, row(norm1_g[l]), w_in[l].astype(bf16))
        yc, yp, ys = _local_mixers(
            pc, pp, ps, conv_w[l], row(conv_b[l]), row(conv_norm_g[l]), row(conv_norm_b[l]),
            _block_diag(pool_w[l]).astype(bf16), row(pool_scale[l]),
            row(sgu_norm_g[l]), row(sgu_norm_b[l]), sgu_w[l],
            jnp.repeat(sgu_b[l].T, HEAD_DIM, axis=1), mg)
        ya = _moba_attention(qT, k, vT, kmean.reshape(B, -1, GROUP_WIDTH), bias_tiles, far_bias, mg)
        flat = lambda a: a.reshape(N, -1)
        x1, h2, destT, gateT, padded = _outproj_router(
            flat(yc), flat(ya), flat(yp), flat(ys), x.reshape(N, D), gate1, w_out[l].astype(bf16),
            row(norm2_g[l]), shift2, scale2, router_w[l].T, router_b[l].reshape(-1, 1), S)
        t = _routing_tables(padded.reshape(nw, N_EXPERTS), n_tiles)
        xs = _dispatch(h2, destT, t, p_rows)
        ysort = _expert_ffn(xs, exp_w1, exp_b1, exp_w2, exp_b2, t['tile_expert'], t['n_used'], l)
        x = _combine(ysort, destT, gateT, x1, gate2, row(final_norm_g), t, S,
                     final=(l == L - 1)).reshape(B, S, D)
    return x
```

```python
import functools
import math

import numpy as np
import jax
import jax.numpy as jnp
from jax import lax
from jax.experimental import pallas as pl
from jax.experimental.pallas import tpu as pltpu

f32, bf16, i32 = jnp.float32, jnp.bfloat16, jnp.int32

GROUP_WIDTH = 256
HEADS = 4
HEAD_DIM = 64
CONV_WIDTH = 31
MOBA_BLOCK = 256
MOBA_TOPK = 3
Q_CHUNK = 128
N_BUCKETS = 32
T5_MAX_DISTANCE = 128
POOL_WINDOWS = (2, 4, 8, 16)
SGU_CHUNK = 128
N_EXPERTS = 32
TOP_K = 4
SWIGLU_LIMIT = 7.0
SWIGLU_ALPHA = 1.702
EPS = 1e-6

SUBLANES = 8
HALO = 32
SEQ_TILE = 512
ROUTE_TILE = 256
ROUTE_WINDOWS_PER_STEP = 2
ROW_CHUNK = 8
FFN_TILE = 512
FFN_SUBTILES = 2
NEG = -1e30
VMEM_LIMIT = 56 * 1024 * 1024


def _cparams(sem):
    return pltpu.CompilerParams(dimension_semantics=sem, vmem_limit_bytes=VMEM_LIMIT)


def _split_bf16(a):
    hi = a.astype(bf16)
    lo = (a - hi.astype(f32)).astype(bf16)
    return hi, lo


def _mod_kernel(c_ref, w_ref, b_ref, o_ref):
    c = c_ref[...]
    cond = c * jax.nn.sigmoid(c)
    o_ref[0] = jnp.dot(cond, w_ref[0], preferred_element_type=f32,
                       precision=lax.Precision.HIGHEST) + b_ref[0]


def _modulation(c, w_mod, b_mod):
    L, D, M = w_mod.shape
    B = c.shape[0]
    tn = 1536
    return pl.pallas_call(
        _mod_kernel,
        grid=(L, M // tn),
        in_specs=[pl.BlockSpec((B, D), lambda l, j: (0, 0)),
                  pl.BlockSpec((1, D, tn), lambda l, j: (l, 0, j)),
                  pl.BlockSpec((1, 1, tn), lambda l, j: (l, 0, j))],
        out_specs=pl.BlockSpec((1, B, tn), lambda l, j: (l, 0, j)),
        out_shape=jax.ShapeDtypeStruct((L, B, M), f32),
        compiler_params=_cparams(("arbitrary", "arbitrary")),
        name="modulation",
    )(c, w_mod, b_mod.reshape(L, 1, M))


def _inproj_kernel(x_ref, sh_ref, sc_ref, g_ref, w_ref,
                   pc_ref, qT_ref, k_ref, vT_ref, km_ref, pp_ref, ps_ref):
    x = x_ref[0]
    ms = jnp.mean(x * x, axis=-1, keepdims=True)
    h = x * lax.rsqrt(ms + EPS) * g_ref[...]
    h = h * (1.0 + sc_ref[0]) + sh_ref[0]
    proj = jnp.dot(h.astype(bf16), w_ref[...], preferred_element_type=f32)
    gw = GROUP_WIDTH
    pc_ref[0] = proj[:, 0:2 * gw]
    q = proj[:, 2 * gw:3 * gw] * (HEAD_DIM ** -0.5)
    qT_ref[0] = q.T.astype(bf16)
    kk = proj[:, 3 * gw:4 * gw]
    for h in range(HEADS):
        k_ref[0, h] = kk[:, h * HEAD_DIM:(h + 1) * HEAD_DIM].astype(bf16)
    for j in range(SEQ_TILE // MOBA_BLOCK):
        km_ref[0, 0, j:j + 1, :] = jnp.mean(kk[j * MOBA_BLOCK:(j + 1) * MOBA_BLOCK], axis=0, keepdims=True)
    for j in range(SEQ_TILE // MOBA_BLOCK):
        vT_ref[0, j] = proj[j * MOBA_BLOCK:(j + 1) * MOBA_BLOCK, 4 * gw:5 * gw].T.astype(bf16)
    pp_ref[0] = proj[:, 5 * gw:6 * gw]
    ps_ref[0] = proj[:, 6 * gw:8 * gw]


def _in_projection(x, shift, scale, g, w_bf):
    B, S, D = x.shape
    gw = GROUP_WIDTH
    ts = SEQ_TILE
    nt = S // ts
    row = lambda b, i: (b, i, 0)
    col = lambda b, i: (b, 0, i)
    vec = lambda b, i: (b, 0, 0)
    return pl.pallas_call(
        _inproj_kernel,
        grid=(B, nt),
        in_specs=[pl.BlockSpec((1, ts, D), row),
                  pl.BlockSpec((1, 1, D), vec),
                  pl.BlockSpec((1, 1, D), vec),
                  pl.BlockSpec((1, D), lambda b, i: (0, 0)),
                  pl.BlockSpec(w_bf.shape, lambda b, i: (0, 0))],
        out_specs=[pl.BlockSpec((1, ts, 2 * gw), row),
                   pl.BlockSpec((1, gw, ts), col),
                   pl.BlockSpec((1, HEADS, ts, HEAD_DIM), lambda b, i: (b, 0, i, 0)),
                   pl.BlockSpec((1, ts // MOBA_BLOCK, gw, MOBA_BLOCK), lambda b, i: (b, i, 0, 0)),
                   pl.BlockSpec((1, 1, ts // MOBA_BLOCK, gw), lambda b, i: (b, i, 0, 0)),
                   pl.BlockSpec((1, ts, gw), row),
                   pl.BlockSpec((1, ts, 2 * gw), row)],
        out_shape=[jax.ShapeDtypeStruct((B, S, 2 * gw), f32),
                   jax.ShapeDtypeStruct((B, gw, S), bf16),
                   jax.ShapeDtypeStruct((B, HEADS, S, HEAD_DIM), bf16),
                   jax.ShapeDtypeStruct((B, S // MOBA_BLOCK, gw, MOBA_BLOCK), bf16),
                   jax.ShapeDtypeStruct((B, nt, ts // MOBA_BLOCK, gw), f32),
                   jax.ShapeDtypeStruct((B, S, gw), f32),
                   jax.ShapeDtypeStruct((B, S, 2 * gw), f32)],
        compiler_params=_cparams(("arbitrary", "arbitrary")),
        name="in_projection",
    )(x, shift, scale, g, w_bf)


def _group_rms(y, g):
    return y * lax.rsqrt(jnp.mean(y * y, axis=-1, keepdims=True) + EPS) * g


def _local_kernel(pc_ref, pcp_ref, pp_ref, ppp_ref, ps_ref,
                  cw_ref, cb_ref, cng_ref, cnb_ref, pw_ref, psc_ref,
                  sg_ref, sb_ref, sw_ref, sbias_ref, mg_ref,
                  yc_ref, yp_ref, ys_ref,
                  gext, gshift, zext, s2, s4, s8):
    i = pl.program_id(1)
    ts = SEQ_TILE
    gw = GROUP_WIDTH
    first = i == 0
    lane = lax.broadcasted_iota(i32, (1, gw), 1)

    pc = pc_ref[0]
    g = pc[:, :gw] * jax.nn.sigmoid(pc[:, gw:])
    ph = pcp_ref[0]
    gh = ph[:, :gw] * jax.nn.sigmoid(ph[:, gw:])
    gext[0:HALO, :] = jnp.where(first, 0.0, gh)
    gext[HALO:HALO + ts, :] = g
    span = ts + HALO - SUBLANES
    for a in range(1, SUBLANES):
        gshift[a - 1, 0:span, :] = gext[pl.ds(a, span), :]
    acc = jnp.zeros((ts, gw), f32)
    for j in range(CONV_WIDTH):
        b, a = divmod(HALO - (CONV_WIDTH - 1) + j, SUBLANES)
        tap = gext[pl.ds(SUBLANES * b, ts), :] if a == 0 else gshift[a - 1, pl.ds(SUBLANES * b, ts), :]
        acc = acc + cw_ref[j:j + 1, :] * tap
    y = acc + cb_ref[...]
    r = lax.broadcasted_iota(i32, (gw, gw), 0) // HEAD_DIM
    c = lax.broadcasted_iota(i32, (gw, gw), 1) // HEAD_DIM
    avg = jnp.where(r == c, 1.0 / HEAD_DIM, 0.0).astype(bf16)

    def head_mean(t):
        hi, lo = _split_bf16(t)
        return (jnp.dot(hi, avg, preferred_element_type=f32)
                + jnp.dot(lo, avg, preferred_element_type=f32))

    mu = head_mean(y)
    yc = y - mu
    var = head_mean(yc * yc)
    yn = yc * lax.rsqrt(var + EPS) * cng_ref[...] + cnb_ref[...]
    yconv = yn * jax.nn.sigmoid(yn)
    yc_ref[0] = _group_rms(yconv, mg_ref[:, 0:gw]).astype(bf16)

    z = pp_ref[0]
    zext[0:HALO, :] = jnp.where(first, 0.0, ppp_ref[0])
    zext[HALO:HALO + ts, :] = z
    n2, n4, n8 = ts + 14, ts + 12, ts + 8
    s2[0:n2, :] = zext[pl.ds(HALO - 14, n2), :] + zext[pl.ds(HALO - 15, n2), :]
    s4[0:n4, :] = s2[pl.ds(2, n4), :] + s2[pl.ds(0, n4), :]
    s8[0:n8, :] = s4[pl.ds(4, n8), :] + s4[pl.ds(0, n8), :]
    w2 = s2[pl.ds(14, ts), :]
    w4 = s4[pl.ds(12, ts), :]
    w8 = s8[pl.ds(8, ts), :]
    w16 = w8 + s8[pl.ds(0, ts), :]
    tpos = (i * ts + lax.broadcasted_iota(i32, (ts, 1), 0) + 1).astype(f32)
    grp = lane // (gw // len(POOL_WINDOWS))
    pooled = jnp.zeros((ts, gw), f32)
    for gi, (w, sw) in enumerate(zip(POOL_WINDOWS, (w2, w4, w8, w16))):
        pooled = jnp.where(grp == gi, sw / jnp.minimum(tpos, float(w)), pooled)
    pooled = pooled - z
    yp = jnp.dot(pooled.astype(bf16), pw_ref[...], preferred_element_type=f32) * psc_ref[...]
    yp_ref[0] = _group_rms(yp, mg_ref[:, 2 * gw:3 * gw]).astype(bf16)

    zz = ps_ref[0]
    zz = 0.5 * zz * (1.0 + lax.erf(zz * (1.0 / math.sqrt(2.0))))
    u = zz[:, :gw]
    v = zz[:, gw:]
    vm = jnp.mean(v, axis=-1, keepdims=True)
    vc = v - vm
    vv = jnp.mean(vc * vc, axis=-1, keepdims=True)
    vn = (vc * lax.rsqrt(vv + EPS) * sg_ref[...] + sb_ref[...]).astype(bf16)
    li = lax.broadcasted_iota(i32, (SGU_CHUNK, SGU_CHUNK), 0)
    lj = lax.broadcasted_iota(i32, (SGU_CHUNK, SGU_CHUNK), 1)
    head_of_lane = lane // HEAD_DIM
    wts = [jnp.where(li >= lj, sw_ref[h], 0.0).astype(bf16) for h in range(HEADS)]
    outs = []
    for n in range(ts // SGU_CHUNK):
        vch = vn[n * SGU_CHUNK:(n + 1) * SGU_CHUNK]
        mixed = sbias_ref[...]
        for h in range(HEADS):
            mh = jnp.dot(wts[h], vch, preferred_element_type=f32)
            mixed = mixed + jnp.where(head_of_lane == h, mh, 0.0)
        outs.append(u[n * SGU_CHUNK:(n + 1) * SGU_CHUNK] * mixed)
    ysgu = jnp.concatenate(outs, axis=0)
    ys_ref[0] = _group_rms(ysgu, mg_ref[:, 3 * gw:4 * gw]).astype(bf16)


def _local_mixers(pc, pp, ps, cw, cb, cng, cnb, pw_bd, psc, sg, sb, sw, sbias, mg):
    B, S, _ = pc.shape
    gw = GROUP_WIDTH
    ts = SEQ_TILE
    hb = ts // HALO
    row = lambda b, i: (b, i, 0)
    prev = lambda b, i: (b, jnp.maximum(i * hb - 1, 0), 0)
    full2 = lambda b, i: (0, 0)
    full3 = lambda b, i: (0, 0, 0)
    out = jax.ShapeDtypeStruct((B, S, gw), bf16)
    return pl.pallas_call(
        _local_kernel,
        grid=(B, S // ts),
        in_specs=[pl.BlockSpec((1, ts, 2 * gw), row),
                  pl.BlockSpec((1, HALO, 2 * gw), prev),
                  pl.BlockSpec((1, ts, gw), row),
                  pl.BlockSpec((1, HALO, gw), prev),
                  pl.BlockSpec((1, ts, 2 * gw), row),
                  pl.BlockSpec(cw.shape, full2), pl.BlockSpec(cb.shape, full2),
                  pl.BlockSpec(cng.shape, full2), pl.BlockSpec(cnb.shape, full2),
                  pl.BlockSpec(pw_bd.shape, full2), pl.BlockSpec(psc.shape, full2),
                  pl.BlockSpec(sg.shape, full2), pl.BlockSpec(sb.shape, full2),
                  pl.BlockSpec(sw.shape, full3), pl.BlockSpec(sbias.shape, full2),
                  pl.BlockSpec(mg.shape, full2)],
        out_specs=[pl.BlockSpec((1, ts, gw), row)] * 3,
        out_shape=[out, out, out],
        scratch_shapes=[pltpu.VMEM((ts + HALO, gw), f32), pltpu.VMEM((SUBLANES - 1, ts + HALO, gw), f32),
                        pltpu.VMEM((ts + HALO, gw), f32),
                        pltpu.VMEM((ts + 16, gw), f32), pltpu.VMEM((ts + 16, gw), f32),
                        pltpu.VMEM((ts + 16, gw), f32)],
        compiler_params=_cparams(("arbitrary", "arbitrary")),
        name="local_mixers",
    )(pc, pc, pp, pp, ps, cw, cb, cng, cnb, pw_bd, psc, sg, sb, sw, sbias, mg)


def _t5_bucket_table(max_dist):
    d = np.arange(max_dist, dtype=np.int64)
    max_exact = N_BUCKETS // 2
    nf = np.maximum(d, 1).astype(np.float32)
    large = max_exact + (np.log(nf / np.float32(max_exact)) / np.float32(math.log(T5_MAX_DISTANCE / max_exact))
                         * np.float32(N_BUCKETS - max_exact)).astype(np.int32)
    large = np.minimum(large, N_BUCKETS - 1)
    return np.where(d < max_exact, d, large).astype(np.int32)


_TILE_BASES = (0, MOBA_BLOCK)


def _bias_kernel(tab_ref, o_ref):
    blk, qc = MOBA_BLOCK, MOBA_BLOCK
    table = _t5_bucket_table(2 * blk + qc)
    first = [int(np.argmax(table >= b)) for b in range(N_BUCKETS)]
    j = lax.broadcasted_iota(i32, (blk, qc), 0)
    q = lax.broadcasted_iota(i32, (blk, qc), 1)
    for t, base in enumerate(_TILE_BASES):
        d = base + q - j
        lo, hi = max(base - (blk - 1), 0), base + qc - 1
        for h in range(HEADS):
            val = jnp.full((blk, qc), tab_ref[h], f32)
            for b in range(1, N_BUCKETS):
                if first[b] > hi:
                    continue
                if first[b] <= lo:
                    val = jnp.full((blk, qc), tab_ref[b * HEADS + h], f32)
                else:
                    val = jnp.where(d >= first[b], tab_ref[b * HEADS + h], val)
            o_ref[t, h] = jnp.where(d >= 0, val, NEG)


def _bias_tiles(rel_bias):
    return pl.pallas_call(
        _bias_kernel,
        in_specs=[pl.BlockSpec(memory_space=pltpu.SMEM)],
        out_shape=jax.ShapeDtypeStruct((len(_TILE_BASES), HEADS, MOBA_BLOCK, MOBA_BLOCK), f32),
        name="bias_tiles",
    )(rel_bias.astype(f32).reshape(-1))


def _attn_kernel(far_ref, qT_ref, k_ref, vT_ref, km_ref, bias_ref, mg_ref, o_ref, mask_ref, s_ref):
    own = pl.program_id(1)
    blk, gw, hd = MOBA_BLOCK, GROUP_WIDTH, HEAD_DIM
    nb = km_ref.shape[1]

    nio = lax.broadcasted_iota(i32, (nb, blk), 0)
    past = nio < own
    km = km_ref[0]
    q_heads = []
    for h in range(HEADS):
        qh = qT_ref[0, h * hd:(h + 1) * hd, :]
        q_heads.append(qh)
        km_hi, km_lo = _split_bf16(km[:, h * hd:(h + 1) * hd])
        gate = (jnp.dot(km_hi, qh, preferred_element_type=f32)
                + jnp.dot(km_lo, qh, preferred_element_type=f32))
        gate = jnp.where(past, gate, -jnp.inf)
        rank = jnp.zeros((nb, blk), i32)
        for m in range(nb):
            gm = gate[m:m + 1, :]
            beats = (gm > gate) | ((gm == gate) & (m < nio))
            rank = rank + beats.astype(i32)
        sel = (rank < MOBA_TOPK) & past
        mask_ref[0, h] = jnp.where(sel, 0.0, NEG)
        mask_ref[1, h] = jnp.where(sel, far_ref[h], NEG)

    def qk(n, h):
        kb = k_ref[0, h, pl.ds(pl.multiple_of(n * blk, blk), blk), :]
        return jnp.dot(kb, q_heads[h], preferred_element_type=f32)

    def far_scores(n, h):
        return qk(n, h) + mask_ref[1, h, pl.ds(n, 1), :]

    def update(state, scores, n):
        out = []
        for h in range(HEADS):
            m, l, acc = state[3 * h:3 * h + 3]
            s = scores[h]
            m_new = jnp.maximum(m, jnp.max(s, axis=0, keepdims=True))
            alpha = jnp.exp(m - m_new)
            p = jnp.exp(s - m_new)
            l = alpha * l + jnp.sum(p, axis=0, keepdims=True)
            vb = vT_ref[0, n, h * hd:(h + 1) * hd, :]
            acc = acc * alpha + jnp.dot(vb, p.astype(bf16), preferred_element_type=f32)
            out += [m_new, l, acc]
        return tuple(out)

    adj = jnp.maximum(own - 1, 0)
    n_far = jnp.maximum(own - 1, 0)
    s_own = [qk(own, h) + bias_ref[0, h] for h in range(HEADS)]
    s_adj = [qk(adj, h) + bias_ref[1, h] + mask_ref[0, h, pl.ds(adj, 1), :] for h in range(HEADS)]
    for h in range(HEADS):
        s_ref[0, h] = far_scores(0, h)

    state = []
    for h in range(HEADS):
        m0 = jnp.max(s_own[h], axis=0, keepdims=True)
        p = jnp.exp(s_own[h] - m0)
        vb = vT_ref[0, own, h * hd:(h + 1) * hd, :]
        state += [m0, jnp.sum(p, axis=0, keepdims=True),
                  jnp.dot(vb, p.astype(bf16), preferred_element_type=f32)]
    state = update(tuple(state), s_adj, adj)

    def body(n, state):
        slot = n % 2
        cur = [s_ref[slot, h] for h in range(HEADS)]
        nxt = jnp.minimum(n + 1, jnp.maximum(n_far - 1, 0))
        for h in range(HEADS):
            s_ref[1 - slot, h] = far_scores(nxt, h)
        return update(state, cur, n)

    fin = lax.fori_loop(0, n_far, body, state)
    outT = jnp.concatenate([fin[3 * h + 2] / fin[3 * h + 1] for h in range(HEADS)], axis=0)
    o_ref[0] = _group_rms(outT.T, mg_ref[:, gw:2 * gw]).astype(bf16)


def _moba_attention(qT, k, vT, kmean, bias_tiles, far_bias, mg):
    B, _, S, _ = k.shape
    gw, blk = GROUP_WIDTH, MOBA_BLOCK
    nb = kmean.shape[1]
    return pl.pallas_call(
        _attn_kernel,
        grid=(B, nb),
        in_specs=[pl.BlockSpec(memory_space=pltpu.SMEM),
                  pl.BlockSpec((1, gw, blk), lambda b, c: (b, 0, c)),
                  pl.BlockSpec((1, HEADS, S, HEAD_DIM), lambda b, c: (b, 0, 0, 0)),
                  pl.BlockSpec((1, nb, gw, blk), lambda b, c: (b, 0, 0, 0)),
                  pl.BlockSpec((1, nb, gw), lambda b, c: (b, 0, 0)),
                  pl.BlockSpec(bias_tiles.shape, lambda b, c: (0, 0, 0, 0)),
                  pl.BlockSpec(mg.shape, lambda b, c: (0, 0))],
        out_specs=pl.BlockSpec((1, blk, gw), lambda b, c: (b, c, 0)),
        out_shape=jax.ShapeDtypeStruct((B, S, gw), bf16),
        scratch_shapes=[pltpu.VMEM((2, HEADS, nb, blk), f32), pltpu.VMEM((2, HEADS, blk, blk), f32)],
        compiler_params=_cparams(("arbitrary", "arbitrary")),
        name="moba_attention",
    )(far_bias, qT, k, vT, kmean, bias_tiles, mg)


def _outproj_router_kernel(yc_ref, ya_ref, yp_ref, ys_ref, x_ref, g1_ref, wo_ref, n2_ref, sh_ref, sc_ref,
                           rw_ref, rb_ref,
                           x1_ref, h2_ref, dest_ref, gate_ref, pad_ref):
    W = ROUTE_TILE
    mixed = [jnp.concatenate([r[u * W:(u + 1) * W, :] for r in (yc_ref, ya_ref, yp_ref, ys_ref)], axis=1)
             for u in range(ROUTE_WINDOWS_PER_STEP)]
    projected = [jnp.dot(m, wo_ref[...], preferred_element_type=f32) for m in mixed]
    for u in range(ROUTE_WINDOWS_PER_STEP):
        _route_window(u, projected[u], x_ref, g1_ref, n2_ref, sh_ref, sc_ref, rw_ref, rb_ref,
                      x1_ref, h2_ref, dest_ref, gate_ref, pad_ref)


def _route_window(u, projected, x_ref, g1_ref, n2_ref, sh_ref, sc_ref, rw_ref, rb_ref,
                  x1_ref, h2_ref, dest_ref, gate_ref, pad_ref):
    W = ROUTE_TILE
    rows = slice(u * W, (u + 1) * W)
    x1 = x_ref[rows, :] + g1_ref[0] * projected
    x1_ref[rows, :] = x1
    ms = jnp.mean(x1 * x1, axis=-1, keepdims=True)
    h = x1 * lax.rsqrt(ms + EPS) * n2_ref[...]
    h = h * (1.0 + sc_ref[0]) + sh_ref[0]
    h_hi, h_lo = _split_bf16(h)
    h2_ref[rows, :] = h_hi

    nt = (((1,), (1,)), ((), ()))
    rw_hi, rw_lo = _split_bf16(rw_ref[...])
    logits = (lax.dot_general(rw_hi, h_hi, nt, preferred_element_type=f32)
              + lax.dot_general(rw_hi, h_lo, nt, preferred_element_type=f32)
              + lax.dot_general(rw_lo, h_hi, nt, preferred_element_type=f32)) + rb_ref[...]
    eio = lax.broadcasted_iota(i32, (N_EXPERTS, W), 0)
    work = logits
    vals, sels = [], []
    for k in range(TOP_K):
        m = jnp.max(work, axis=0, keepdims=True)
        idx = jnp.min(jnp.where(work == m, eio, N_EXPERTS), axis=0, keepdims=True)
        sel = eio == idx
        vals.append(m)
        sels.append(sel)
        work = jnp.where(sel, -jnp.inf, work)
    exps = [jnp.exp(v - vals[0]) for v in vals]
    denom = exps[0] + exps[1] + exps[2] + exps[3]
    for k in range(TOP_K):
        gate_ref[k:k + 1, rows] = exps[k] / denom

    multi = jnp.zeros((N_EXPERTS, W), f32)
    for sel in sels:
        multi = multi + sel.astype(f32)
    before = (lax.broadcasted_iota(i32, (W, W), 0) < lax.broadcasted_iota(i32, (W, W), 1)).astype(bf16)
    earlier = jnp.dot(multi.astype(bf16), before, preferred_element_type=f32)
    cnt = jnp.sum(multi, axis=1, keepdims=True).astype(i32)
    padded = (cnt + (ROW_CHUNK - 1)) // ROW_CHUNK * ROW_CHUNK
    pad_ref[u] = padded
    lower = (lax.broadcasted_iota(i32, (N_EXPERTS, N_EXPERTS), 1)
             < lax.broadcasted_iota(i32, (N_EXPERTS, N_EXPERTS), 0)).astype(bf16)
    seg_start = jnp.dot(lower, jnp.broadcast_to(padded.astype(f32), (N_EXPERTS, W)).astype(bf16),
                        preferred_element_type=f32)
    row = seg_start + earlier
    for k in range(TOP_K):
        dest_ref[k:k + 1, rows] = jnp.sum(jnp.where(sels[k], row, 0.0), axis=0, keepdims=True).astype(i32)


def _outproj_router(yc, ya, yp, ys, x, gate1, wo_bf, n2g, shift2, scale2, rwT, rb, seq):
    N, D = x.shape
    gw = GROUP_WIDTH
    nw = N // ROUTE_TILE
    wps = ROUTE_WINDOWS_PER_STEP
    W = ROUTE_TILE * wps
    per_b = seq // W
    row = lambda i: (i, 0)
    vec = lambda i: (i // per_b, 0, 0)
    full = lambda i: (0, 0)
    colblk = lambda i: (0, i)
    return pl.pallas_call(
        _outproj_router_kernel,
        grid=(nw // wps,),
        in_specs=[pl.BlockSpec((W, gw), row)] * 4 + [
            pl.BlockSpec((W, D), row),
            pl.BlockSpec((1, 1, D), vec),
            pl.BlockSpec(wo_bf.shape, full),
            pl.BlockSpec((1, D), full),
            pl.BlockSpec((1, 1, D), vec),
            pl.BlockSpec((1, 1, D), vec),
            pl.BlockSpec(rwT.shape, full),
            pl.BlockSpec(rb.shape, full)],
        out_specs=[pl.BlockSpec((W, D), row),
                   pl.BlockSpec((W, D), row),
                   pl.BlockSpec((TOP_K, W), colblk),
                   pl.BlockSpec((TOP_K, W), colblk),
                   pl.BlockSpec((wps, N_EXPERTS, 1), lambda i: (i, 0, 0))],
        out_shape=[jax.ShapeDtypeStruct((N, D), f32),
                   jax.ShapeDtypeStruct((N, D), bf16),
                   jax.ShapeDtypeStruct((TOP_K, N), i32),
                   jax.ShapeDtypeStruct((TOP_K, N), f32),
                   jax.ShapeDtypeStruct((nw, N_EXPERTS, 1), i32)],
        compiler_params=_cparams(("arbitrary",)),
        name="outproj_router",
    )(yc, ya, yp, ys, x, gate1, wo_bf, n2g, shift2, scale2, rwT, rb)


def _max_window_rows():
    return -(-(ROUTE_TILE * TOP_K + N_EXPERTS * (ROW_CHUNK - 1)) // 128) * 128


def _pack_pairs(lo, hi):
    lo_bits = lax.bitcast_convert_type(lo, jnp.uint32)
    hi_bits = lax.bitcast_convert_type(hi, jnp.uint32)
    return lax.bitcast_convert_type(hi_bits | (lo_bits >> 16), i32)


def _unpack_pairs(words):
    bits = lax.bitcast_convert_type(words, jnp.uint32)
    lo = lax.bitcast_convert_type(bits << 16, f32).astype(bf16)
    hi = lax.bitcast_convert_type(bits & jnp.uint32(0xFFFF0000), f32).astype(bf16)
    return lo, hi


def _dispatch_kernel(lstart_s, nchunk_s, gbase_s, wchunks_s, tail_start_s, tail_chunks_s, n_used_s,
                     h_ref, dest_ref, xs_ref, sorted_ref, zero_ref, sem):
    w = pl.program_id(0)
    nw = pl.num_programs(0)
    W = ROUTE_TILE
    R = sorted_ref.shape[1]
    half = sorted_ref.shape[2]
    slot = w % 2
    rio = lax.broadcasted_iota(i32, (R, W), 0)
    hit = rio == dest_ref[0:1, :]
    for k in range(1, TOP_K):
        hit = hit | (rio == dest_ref[k:k + 1, :])
    onehot = jnp.where(hit, 1.0, 0.0).astype(bf16)
    rows = jnp.dot(onehot, h_ref[...], preferred_element_type=f32)
    sorted_ref[slot] = _pack_pairs(rows[:, :half], rows[:, half:])

    def chunk_copy(s, src_row, dst_row):
        return pltpu.make_async_copy(
            sorted_ref.at[s, pl.ds(pl.multiple_of(src_row, ROW_CHUNK), ROW_CHUNK), :],
            xs_ref.at[pl.ds(pl.multiple_of(dst_row, ROW_CHUNK), ROW_CHUNK), :], sem.at[s])

    def zero_copy(dst_row):
        return pltpu.make_async_copy(
            zero_ref.at[0:ROW_CHUNK, :],
            xs_ref.at[pl.ds(pl.multiple_of(dst_row, ROW_CHUNK), ROW_CHUNK), :], sem.at[0])

    def zero_tile_copy(tile):
        return pltpu.make_async_copy(
            zero_ref, xs_ref.at[pl.ds(pl.multiple_of(tile * FFN_TILE, FFN_TILE), FFN_TILE), :], sem.at[0])

    def drain(s, count):
        @pl.when(count > 0)
        def _():
            rows = pl.multiple_of(count * ROW_CHUNK, ROW_CHUNK)
            pltpu.make_async_copy(sorted_ref.at[s, pl.ds(0, rows), :], xs_ref.at[pl.ds(0, rows), :],
                                  sem.at[s]).wait()

    @pl.when(w > 0)
    def _():
        drain(1 - slot, wchunks_s[jnp.maximum(w - 1, 0)])

    def per_expert(e, c):
        n = nchunk_s[w * N_EXPERTS + e]
        src = lstart_s[w * N_EXPERTS + e]
        dst = gbase_s[w * N_EXPERTS + e]

        def issue(j, c):
            chunk_copy(slot, src + j * ROW_CHUNK, dst + j * ROW_CHUNK).start()
            return c
        lax.fori_loop(0, n, issue, 0)
        return c

    lax.fori_loop(0, N_EXPERTS, per_expert, 0)

    @pl.when(w == nw - 1)
    def _():
        drain(slot, wchunks_s[w])
        zero_ref[...] = jnp.zeros(zero_ref.shape, i32)

        def per_tail(e, total):
            n = tail_chunks_s[e]
            dst = tail_start_s[e]

            def issue(j, c):
                zero_copy(dst + j * ROW_CHUNK).start()
                return c
            lax.fori_loop(0, n, issue, 0)
            return total + n
        tails = lax.fori_loop(0, N_EXPERTS, per_tail, 0)

        def drain_tail(j, c):
            zero_copy(0).wait()
            return c
        lax.fori_loop(0, tails, drain_tail, 0)

        n_tiles = xs_ref.shape[0] // FFN_TILE

        def issue_tile(j, c):
            zero_tile_copy(j).start()
            return c
        lax.fori_loop(n_used_s[0], n_tiles, issue_tile, 0)

        def drain_tile(j, c):
            zero_tile_copy(0).wait()
            return c
        lax.fori_loop(n_used_s[0], n_tiles, drain_tile, 0)


def _dispatch(h2, destT, t, p_rows):
    N, D = h2.shape
    W = ROUTE_TILE
    nw = N // W
    R = _max_window_rows()
    grid_spec = pltpu.PrefetchScalarGridSpec(
        num_scalar_prefetch=7,
        grid=(nw,),
        in_specs=[pl.BlockSpec((W, D), lambda w, *_: (w, 0)),
                  pl.BlockSpec((TOP_K, W), lambda w, *_: (0, w))],
        out_specs=pl.BlockSpec(memory_space=pl.ANY),
        scratch_shapes=[pltpu.VMEM((2, R, D // 2), i32), pltpu.VMEM((FFN_TILE, D // 2), i32),
                        pltpu.SemaphoreType.DMA((2,))],
    )
    return pl.pallas_call(
        _dispatch_kernel,
        grid_spec=grid_spec,
        out_shape=jax.ShapeDtypeStruct((p_rows, D // 2), i32),
        compiler_params=_cparams(("arbitrary",)),
        name="expert_dispatch",
    )(t['lstart'], t['nchunk'], t['gbase'], t['wchunks'], t['tail_start'], t['tail_chunks'], t['n_used'],
      h2, destT)


def _ffn_kernel(tile_expert_s, next_expert_s, n_used_s, x_ref, w1_hbm, b1_ref, w2_hbm, b2_ref, y_ref,
                w1f_ref, w2f_ref, w1b_ref, w2b_ref, sem, *, layer):
    i = pl.program_id(0)
    last = n_used_s[0] - 1
    expert = tile_expert_s[jnp.minimum(i, last)]
    prev_expert = tile_expert_s[jnp.minimum(jnp.maximum(i - 1, 0), last)]

    def weight_copies(e):
        return (pltpu.make_async_copy(w1_hbm.at[layer, e], w1f_ref, sem.at[0]),
                pltpu.make_async_copy(w2_hbm.at[layer, e], w2f_ref, sem.at[1]))

    @pl.when(i == 0)
    def _():
        for c in weight_copies(expert):
            c.start()

    @pl.when((i == 0) | (expert != prev_expert))
    def _():
        for c in weight_copies(expert):
            c.wait()
        w1b_ref[...] = w1f_ref[...].astype(bf16)
        w2b_ref[...] = w2f_ref[...].astype(bf16)
        nxt = next_expert_s[expert]

        @pl.when(nxt < N_EXPERTS)
        def _():
            for c in weight_copies(nxt):
                c.start()

    @pl.when(i < n_used_s[0])
    def _():
        dff = w2b_ref.shape[0]
        half = x_ref.shape[1]
        sub = x_ref.shape[0] // FFN_SUBTILES
        hidden = []
        for r in range(FFN_SUBTILES):
            x_lo, x_hi = _unpack_pairs(x_ref[r * sub:(r + 1) * sub, :])
            hidden.append(jnp.dot(x_lo, w1b_ref[0:half, :], preferred_element_type=f32)
                          + jnp.dot(x_hi, w1b_ref[half:2 * half, :], preferred_element_type=f32))
        for r in range(FFN_SUBTILES):
            hh = hidden[r] + b1_ref[0, 0]
            x_glu = jnp.minimum(hh[:, :dff], SWIGLU_LIMIT)
            x_lin = jnp.clip(hh[:, dff:], -SWIGLU_LIMIT, SWIGLU_LIMIT)
            act = x_glu * jax.nn.sigmoid(SWIGLU_ALPHA * x_glu) * (x_lin + 1.0)
            y = jnp.dot(act.astype(bf16), w2b_ref[...], preferred_element_type=f32) + b2_ref[0, 0]
            y = y.astype(bf16).astype(f32)
            y_ref[r * sub:(r + 1) * sub, :] = _pack_pairs(y[:, :half], y[:, half:])

    @pl.when(i >= n_used_s[0])
    def _():
        y_ref[...] = jnp.zeros(y_ref.shape, i32)


def _expert_ffn(xs, w1, b1, w2, b2, t, layer):
    P, half = xs.shape
    L, E, D, F2 = w1.shape
    tm = FFN_TILE
    nt = P // tm

    def tile(i, te, ne, nu):
        return (jnp.minimum(i, nu[0] - 1), 0)

    def expert4(i, te, ne, nu):
        return (layer, te[jnp.minimum(i, nu[0] - 1)], 0, 0)

    grid_spec = pltpu.PrefetchScalarGridSpec(
        num_scalar_prefetch=3,
        grid=(nt,),
        in_specs=[pl.BlockSpec((tm, half), tile),
                  pl.BlockSpec(memory_space=pl.ANY),
                  pl.BlockSpec((1, 1, 1, F2), expert4),
                  pl.BlockSpec(memory_space=pl.ANY),
                  pl.BlockSpec((1, 1, 1, D), expert4)],
        out_specs=pl.BlockSpec((tm, half), lambda i, te, ne, nu: (i, 0)),
        scratch_shapes=[pltpu.VMEM((D, F2), f32), pltpu.VMEM((F2 // 2, D), f32),
                        pltpu.VMEM((D, F2), bf16), pltpu.VMEM((F2 // 2, D), bf16),
                        pltpu.SemaphoreType.DMA((2,))],
    )
    return pl.pallas_call(
        functools.partial(_ffn_kernel, layer=layer),
        grid_spec=grid_spec,
        out_shape=jax.ShapeDtypeStruct((P, half), i32),
        compiler_params=_cparams(("arbitrary",)),
        name="expert_ffn",
    )(t['tile_expert'], t['next_expert'], t['n_used'], xs, w1, b1.reshape(L, E, 1, F2), w2,
      b2.reshape(L, E, 1, D))


def _combine_kernel(lstart_s, nchunk_s, gbase_s, wchunks_s,
                    ys_ref, dest_ref, gate_ref, x1_ref, g2_ref, fg_ref, o_ref, local_ref, sem, *, final):
    w = pl.program_id(0)
    nw = pl.num_programs(0)
    W = ROUTE_TILE
    R = local_ref.shape[1]
    slot = w % 2

    def chunk_copy(s, src_row, dst_row):
        return pltpu.make_async_copy(
            ys_ref.at[pl.ds(pl.multiple_of(src_row, ROW_CHUNK), ROW_CHUNK), :],
            local_ref.at[s, pl.ds(pl.multiple_of(dst_row, ROW_CHUNK), ROW_CHUNK), :], sem.at[s])

    def fetch(win, s):
        def per_expert(e, c):
            n = nchunk_s[win * N_EXPERTS + e]
            dst = lstart_s[win * N_EXPERTS + e]
            src = gbase_s[win * N_EXPERTS + e]

            def issue(j, c):
                chunk_copy(s, src + j * ROW_CHUNK, dst + j * ROW_CHUNK).start()
                return c
            lax.fori_loop(0, n, issue, 0)
            return c
        lax.fori_loop(0, N_EXPERTS, per_expert, 0)

    @pl.when(w == 0)
    def _():
        local_ref[...] = jnp.zeros(local_ref.shape, i32)
        fetch(0, 0)

    @pl.when(w + 1 < nw)
    def _():
        fetch(jnp.minimum(w + 1, nw - 1), 1 - slot)

    @pl.when(wchunks_s[w] > 0)
    def _():
        rows = pl.multiple_of(wchunks_s[w] * ROW_CHUNK, ROW_CHUNK)
        pltpu.make_async_copy(ys_ref.at[pl.ds(0, rows), :], local_ref.at[slot, pl.ds(0, rows), :],
                              sem.at[slot]).wait()

    rio = lax.broadcasted_iota(i32, (R, W), 0)
    weights = jnp.zeros((R, W), f32)
    for k in range(TOP_K):
        weights = jnp.where(rio == dest_ref[k:k + 1, :], gate_ref[k:k + 1, :], weights)
    weights = weights.astype(bf16)
    y_lo, y_hi = _unpack_pairs(local_ref[slot])
    tn = (((0,), (0,)), ((), ()))
    moe = jnp.concatenate([lax.dot_general(weights, y_lo, tn, preferred_element_type=f32),
                           lax.dot_general(weights, y_hi, tn, preferred_element_type=f32)], axis=1)
    x2 = x1_ref[...] + g2_ref[0] * moe
    if final:
        ms = jnp.mean(x2 * x2, axis=-1, keepdims=True)
        x2 = x2 * lax.rsqrt(ms + EPS) * fg_ref[...]
    o_ref[...] = x2


def _combine(ys, dest, gates, x1, gate2, final_g, t, seq, final):
    N, D = x1.shape
    W = ROUTE_TILE
    nw = N // W
    per_b = seq // W
    R = _max_window_rows()
    grid_spec = pltpu.PrefetchScalarGridSpec(
        num_scalar_prefetch=4,
        grid=(nw,),
        in_specs=[pl.BlockSpec(memory_space=pl.ANY),
                  pl.BlockSpec((TOP_K, W), lambda w, *_: (0, w)),
                  pl.BlockSpec((TOP_K, W), lambda w, *_: (0, w)),
                  pl.BlockSpec((W, D), lambda w, *_: (w, 0)),
                  pl.BlockSpec((1, 1, D), lambda w, *_: (w // per_b, 0, 0)),
                  pl.BlockSpec((1, D), lambda w, *_: (0, 0))],
        out_specs=pl.BlockSpec((W, D), lambda w, *_: (w, 0)),
        scratch_shapes=[pltpu.VMEM((2, R, D // 2), i32), pltpu.SemaphoreType.DMA((2,))],
    )
    return pl.pallas_call(
        functools.partial(_combine_kernel, final=final),
        grid_spec=grid_spec,
        out_shape=jax.ShapeDtypeStruct((N, D), f32),
        compiler_params=_cparams(("arbitrary",)),
        name="expert_combine",
    )(t['lstart'], t['nchunk'], t['gbase'], t['wchunks'], ys, dest, gates, x1, gate2, final_g)


def _routing_tables(padded, n_tiles):
    nw, E = padded.shape
    lstart = jnp.cumsum(padded, axis=1) - padded
    tot = jnp.sum(padded, axis=0)
    region = (tot + FFN_TILE - 1) // FFN_TILE * FFN_TILE
    region_end = jnp.cumsum(region)
    region_start = region_end - region
    gbase = region_start[None, :] + jnp.cumsum(padded, axis=0) - padded
    nchunk = padded // ROW_CHUNK
    tail_start = region_start + tot
    tail_chunks = (region - tot) // ROW_CHUNK
    n_used = (region_end[-1] // FFN_TILE).astype(i32).reshape(1)
    tile_row = jnp.arange(n_tiles, dtype=i32) * FFN_TILE
    tile_expert = jnp.minimum(
        jnp.sum((region_end[None, :] <= tile_row[:, None]).astype(i32), axis=1), E - 1).astype(i32)
    eidx = jnp.arange(E, dtype=i32)
    later_nonempty = (eidx[None, :] > eidx[:, None]) & (region[None, :] > 0)
    next_expert = jnp.min(jnp.where(later_nonempty, eidx[None, :], E), axis=1)
    flat = lambda a: a.reshape(-1).astype(i32)
    return dict(lstart=flat(lstart), nchunk=flat(nchunk), gbase=flat(gbase), next_expert=flat(next_expert),
                wchunks=flat(jnp.sum(nchunk, axis=1)),
                tail_start=flat(tail_start), tail_chunks=flat(tail_chunks),
                n_used=n_used, tile_expert=tile_expert)


def _block_diag(w):
    g, a, b = w.shape
    out = jnp.zeros((g * a, g * b), w.dtype)
    for i in range(g):
        out = out.at[i * a:(i + 1) * a, i * b:(i + 1) * b].set(w[i])
    return out


def kernel(x, c, w_mod, b_mod, norm1_g, w_in, conv_w, conv_b, conv_norm_g, conv_norm_b, rel_bias, pool_w, pool_scale, sgu_norm_g, sgu_norm_b, sgu_w, sgu_b, mix_out_g, w_out, norm2_g, router_w, router_b, exp_w1, exp_b1, exp_w2, exp_b2, final_norm_g):
    B, S, D = x.shape
    L = w_mod.shape[0]
    N = B * S
    nw = N // ROUTE_TILE
    assert S % SEQ_TILE == 0 and S % MOBA_BLOCK == 0 and N % ROUTE_TILE == 0 and S % ROUTE_TILE == 0
    p_bound = N * TOP_K + nw * N_EXPERTS * (ROW_CHUNK - 1) + N_EXPERTS * (FFN_TILE - 1)
    n_tiles = -(-p_bound // FFN_TILE)
    p_rows = n_tiles * FFN_TILE

    mod = _modulation(c, w_mod, b_mod)
    bias_tiles = _bias_tiles(rel_bias)
    far_bucket = int(_t5_bucket_table(MOBA_BLOCK + 2)[MOBA_BLOCK + 1])
    assert far_bucket == int(_t5_bucket_table(S + 1)[S])
    far_bias = rel_bias[far_bucket].astype(f32)
    row = lambda a: a.reshape(1, -1)
    for l in range(L):
        m6 = mod[l].reshape(B, 6, 1, D)
        shift1, scale1, gate1, shift2, scale2, gate2 = (m6[:, j] for j in range(6))
        mg = row(mix_out_g[l])
        pc, qT, k, vT, kmean, pp, ps = _in_projection(x, shift1, scale1, row(norm1_g[l]), w_in[l].astype(bf16))
        yc, yp, ys = _local_mixers(
            pc, pp, ps, conv_w[l], row(conv_b[l]), row(conv_norm_g[l]), row(conv_norm_b[l]),
            _block_diag(pool_w[l]).astype(bf16), row(pool_scale[l]),
            row(sgu_norm_g[l]), row(sgu_norm_b[l]), sgu_w[l],
            jnp.repeat(sgu_b[l].T, HEAD_DIM, axis=1), mg)
        ya = _moba_attention(qT, k, vT, kmean.reshape(B, -1, GROUP_WIDTH), bias_tiles, far_bias, mg)
        flat = lambda a: a.reshape(N, -1)
        x1, h2, destT, gateT, padded = _outproj_router(
            flat(yc), flat(ya), flat(yp), flat(ys), x.reshape(N, D), gate1, w_out[l].astype(bf16),
            row(norm2_g[l]), shift2, scale2, router_w[l].T, router_b[l].reshape(-1, 1), S)
        t = _routing_tables(padded.reshape(nw, N_EXPERTS), n_tiles)
        xs = _dispatch(h2, destT, t, p_rows)
        ysort = _expert_ffn(xs, exp_w1, exp_b1, exp_w2, exp_b2, t, l)
        x = _combine(ysort, destT, gateT, x1, gate2, row(final_norm_g), t, S,
                     final=(l == L - 1)).reshape(B, S, D)
    return x
```

```python
import functools
import math

import numpy as np
import jax
import jax.numpy as jnp
from jax import lax
from jax.experimental import pallas as pl
from jax.experimental.pallas import tpu as pltpu

f32, bf16, i32 = jnp.float32, jnp.bfloat16, jnp.int32

GROUP_WIDTH = 256
HEADS = 4
HEAD_DIM = 64
V_ROWS = HEAD_DIM + 16
LOG2E = math.log2(math.e)
CONV_WIDTH = 31
MOBA_BLOCK = 256
MOBA_TOPK = 3
Q_CHUNK = 128
N_BUCKETS = 32
T5_MAX_DISTANCE = 128
POOL_WINDOWS = (2, 4, 8, 16)
SGU_CHUNK = 128
N_EXPERTS = 32
TOP_K = 4
SWIGLU_LIMIT = 7.0
SWIGLU_ALPHA = 1.702
EPS = 1e-6

SUBLANES = 8
LANES = 128
HALO = 32
SEQ_TILE = 512
ROUTE_TILE = 256
ROUTE_WINDOWS_PER_STEP = 2
ROW_CHUNK = 8
DISPATCH_COPY_ROWS = 32
FFN_TILE = 512
FFN_SUBTILES = 2
NEG = -1e30
VMEM_LIMIT = 56 * 1024 * 1024


def _cparams(sem):
    return pltpu.CompilerParams(dimension_semantics=sem, vmem_limit_bytes=VMEM_LIMIT)


def _split_bf16(a):
    hi = a.astype(bf16)
    lo = (a - hi.astype(f32)).astype(bf16)
    return hi, lo


def _mod_kernel(c_ref, w_ref, b_ref, o_ref):
    c = c_ref[...]
    cond = c * jax.nn.sigmoid(c)
    o_ref[0] = jnp.dot(cond, w_ref[0], preferred_element_type=f32,
                       precision=lax.Precision.HIGHEST) + b_ref[0]


def _modulation(c, w_mod, b_mod):
    L, D, M = w_mod.shape
    B = c.shape[0]
    tn = 1536
    return pl.pallas_call(
        _mod_kernel,
        grid=(L, M // tn),
        in_specs=[pl.BlockSpec((B, D), lambda l, j: (0, 0)),
                  pl.BlockSpec((1, D, tn), lambda l, j: (l, 0, j)),
                  pl.BlockSpec((1, 1, tn), lambda l, j: (l, 0, j))],
        out_specs=pl.BlockSpec((1, B, tn), lambda l, j: (l, 0, j)),
        out_shape=jax.ShapeDtypeStruct((L, B, M), f32),
        compiler_params=_cparams(("arbitrary", "arbitrary")),
        name="modulation",
    )(c, w_mod, b_mod.reshape(L, 1, M))


def _inproj_kernel(x_ref, sh_ref, sc_ref, g_ref, w_ref,
                   pc_ref, qT_ref, k_ref, vT_ref, km_ref, pp_ref, ps_ref):
    x = x_ref[0]
    ms = jnp.mean(x * x, axis=-1, keepdims=True)
    h = x * lax.rsqrt(ms + EPS) * g_ref[...]
    h = h * (1.0 + sc_ref[0]) + sh_ref[0]
    proj = jnp.dot(h.astype(bf16), w_ref[...], preferred_element_type=f32)
    gw = GROUP_WIDTH
    pc_ref[0] = proj[:, 0:2 * gw]
    q = proj[:, 2 * gw:3 * gw] * (HEAD_DIM ** -0.5 * LOG2E)
    qT_ref[0] = q.T.astype(bf16)
    kk = proj[:, 3 * gw:4 * gw]
    for h in range(HEADS):
        k_ref[0, h] = kk[:, h * HEAD_DIM:(h + 1) * HEAD_DIM].astype(bf16)
    for j in range(SEQ_TILE // MOBA_BLOCK):
        km_ref[0, 0, j:j + 1, :] = jnp.mean(kk[j * MOBA_BLOCK:(j + 1) * MOBA_BLOCK], axis=0, keepdims=True)
    for j in range(SEQ_TILE // MOBA_BLOCK):
        vT = proj[j * MOBA_BLOCK:(j + 1) * MOBA_BLOCK, 4 * gw:5 * gw].T.astype(bf16)
        for h in range(HEADS):
            vT_ref[0, j, h] = jnp.concatenate(
                [vT[h * HEAD_DIM:(h + 1) * HEAD_DIM, :], jnp.ones((V_ROWS - HEAD_DIM, MOBA_BLOCK), bf16)], axis=0)
    pp_ref[0] = proj[:, 5 * gw:6 * gw]
    ps_ref[0] = proj[:, 6 * gw:8 * gw]


def _in_projection(x, shift, scale, g, w_bf):
    B, S, D = x.shape
    gw = GROUP_WIDTH
    ts = SEQ_TILE
    nt = S // ts
    row = lambda b, i: (b, i, 0)
    col = lambda b, i: (b, 0, i)
    vec = lambda b, i: (b, 0, 0)
    return pl.pallas_call(
        _inproj_kernel,
        grid=(B, nt),
        in_specs=[pl.BlockSpec((1, ts, D), row),
                  pl.BlockSpec((1, 1, D), vec),
                  pl.BlockSpec((1, 1, D), vec),
                  pl.BlockSpec((1, D), lambda b, i: (0, 0)),
                  pl.BlockSpec(w_bf.shape, lambda b, i: (0, 0))],
        out_specs=[pl.BlockSpec((1, ts, 2 * gw), row),
                   pl.BlockSpec((1, gw, ts), col),
                   pl.BlockSpec((1, HEADS, ts, HEAD_DIM), lambda b, i: (b, 0, i, 0)),
                   pl.BlockSpec((1, ts // MOBA_BLOCK, HEADS, V_ROWS, MOBA_BLOCK), lambda b, i: (b, i, 0, 0, 0)),
                   pl.BlockSpec((1, 1, ts // MOBA_BLOCK, gw), lambda b, i: (b, i, 0, 0)),
                   pl.BlockSpec((1, ts, gw), row),
                   pl.BlockSpec((1, ts, 2 * gw), row)],
        out_shape=[jax.ShapeDtypeStruct((B, S, 2 * gw), f32),
                   jax.ShapeDtypeStruct((B, gw, S), bf16),
                   jax.ShapeDtypeStruct((B, HEADS, S, HEAD_DIM), bf16),
                   jax.ShapeDtypeStruct((B, S // MOBA_BLOCK, HEADS, V_ROWS, MOBA_BLOCK), bf16),
                   jax.ShapeDtypeStruct((B, nt, ts // MOBA_BLOCK, gw), f32),
                   jax.ShapeDtypeStruct((B, S, gw), f32),
                   jax.ShapeDtypeStruct((B, S, 2 * gw), f32)],
        compiler_params=_cparams(("arbitrary", "arbitrary")),
        name="in_projection",
    )(x, shift, scale, g, w_bf)


def _group_rms(y, g):
    return y * lax.rsqrt(jnp.mean(y * y, axis=-1, keepdims=True) + EPS) * g


def _local_kernel(pc_ref, pcp_ref, pp_ref, ppp_ref, ps_ref,
                  cw_ref, cb_ref, cng_ref, cnb_ref, pw_ref, psc_ref,
                  sg_ref, sb_ref, sw_ref, sbias_ref, mg_ref,
                  yc_ref, yp_ref, ys_ref,
                  gext, gshift, zext, s2, s4, s8):
    i = pl.program_id(1)
    ts = SEQ_TILE
    gw = GROUP_WIDTH
    first = i == 0
    lane = lax.broadcasted_iota(i32, (1, gw), 1)

    pc = pc_ref[0]
    g = pc[:, :gw] * jax.nn.sigmoid(pc[:, gw:])
    ph = pcp_ref[0]
    gh = ph[:, :gw] * jax.nn.sigmoid(ph[:, gw:])
    gext[0:HALO, :] = jnp.where(first, 0.0, gh)
    gext[HALO:HALO + ts, :] = g
    span = ts + HALO - SUBLANES
    for a in range(1, SUBLANES):
        gshift[a - 1, 0:span, :] = gext[pl.ds(a, span), :]
    acc = jnp.zeros((ts, gw), f32)
    for j in range(CONV_WIDTH):
        b, a = divmod(HALO - (CONV_WIDTH - 1) + j, SUBLANES)
        tap = gext[pl.ds(SUBLANES * b, ts), :] if a == 0 else gshift[a - 1, pl.ds(SUBLANES * b, ts), :]
        acc = acc + cw_ref[j:j + 1, :] * tap
    y = acc + cb_ref[...]
    r = lax.broadcasted_iota(i32, (gw, gw), 0) // HEAD_DIM
    c = lax.broadcasted_iota(i32, (gw, gw), 1) // HEAD_DIM
    avg = jnp.where(r == c, 1.0 / HEAD_DIM, 0.0).astype(bf16)

    def head_mean(t):
        hi, lo = _split_bf16(t)
        return (jnp.dot(hi, avg, preferred_element_type=f32)
                + jnp.dot(lo, avg, preferred_element_type=f32))

    mu = head_mean(y)
    yc = y - mu
    var = head_mean(yc * yc)
    yn = yc * lax.rsqrt(var + EPS) * cng_ref[...] + cnb_ref[...]
    yconv = yn * jax.nn.sigmoid(yn)
    yc_ref[0] = _group_rms(yconv, mg_ref[:, 0:gw]).astype(bf16)

    z = pp_ref[0]
    zext[0:HALO, :] = jnp.where(first, 0.0, ppp_ref[0])
    zext[HALO:HALO + ts, :] = z
    n2, n4, n8 = ts + 14, ts + 12, ts + 8
    s2[0:n2, :] = zext[pl.ds(HALO - 14, n2), :] + zext[pl.ds(HALO - 15, n2), :]
    s4[0:n4, :] = s2[pl.ds(2, n4), :] + s2[pl.ds(0, n4), :]
    s8[0:n8, :] = s4[pl.ds(4, n8), :] + s4[pl.ds(0, n8), :]
    w2 = s2[pl.ds(14, ts), :]
    w4 = s4[pl.ds(12, ts), :]
    w8 = s8[pl.ds(8, ts), :]
    w16 = w8 + s8[pl.ds(0, ts), :]
    tpos = (i * ts + lax.broadcasted_iota(i32, (ts, 1), 0) + 1).astype(f32)
    grp = lane // (gw // len(POOL_WINDOWS))
    pooled = jnp.zeros((ts, gw), f32)
    for gi, (w, sw) in enumerate(zip(POOL_WINDOWS, (w2, w4, w8, w16))):
        pooled = jnp.where(grp == gi, sw / jnp.minimum(tpos, float(w)), pooled)
    pooled = pooled - z
    yp = jnp.dot(pooled.astype(bf16), pw_ref[...], preferred_element_type=f32) * psc_ref[...]
    yp_ref[0] = _group_rms(yp, mg_ref[:, 2 * gw:3 * gw]).astype(bf16)

    zz = ps_ref[0]
    zz = 0.5 * zz * (1.0 + lax.erf(zz * (1.0 / math.sqrt(2.0))))
    u = zz[:, :gw]
    v = zz[:, gw:]
    vm = jnp.mean(v, axis=-1, keepdims=True)
    vc = v - vm
    vv = jnp.mean(vc * vc, axis=-1, keepdims=True)
    vn = (vc * lax.rsqrt(vv + EPS) * sg_ref[...] + sb_ref[...]).astype(bf16)
    li = lax.broadcasted_iota(i32, (SGU_CHUNK, SGU_CHUNK), 0)
    lj = lax.broadcasted_iota(i32, (SGU_CHUNK, SGU_CHUNK), 1)
    head_of_lane = lane // HEAD_DIM
    wts = [jnp.where(li >= lj, sw_ref[h], 0.0).astype(bf16) for h in range(HEADS)]
    outs = []
    for n in range(ts // SGU_CHUNK):
        vch = vn[n * SGU_CHUNK:(n + 1) * SGU_CHUNK]
        mixed = sbias_ref[...]
        for h in range(HEADS):
            mh = jnp.dot(wts[h], vch, preferred_element_type=f32)
            mixed = mixed + jnp.where(head_of_lane == h, mh, 0.0)
        outs.append(u[n * SGU_CHUNK:(n + 1) * SGU_CHUNK] * mixed)
    ysgu = jnp.concatenate(outs, axis=0)
    ys_ref[0] = _group_rms(ysgu, mg_ref[:, 3 * gw:4 * gw]).astype(bf16)


def _local_mixers(pc, pp, ps, cw, cb, cng, cnb, pw_bd, psc, sg, sb, sw, sbias, mg):
    B, S, _ = pc.shape
    gw = GROUP_WIDTH
    ts = SEQ_TILE
    hb = ts // HALO
    row = lambda b, i: (b, i, 0)
    prev = lambda b, i: (b, jnp.maximum(i * hb - 1, 0), 0)
    full2 = lambda b, i: (0, 0)
    full3 = lambda b, i: (0, 0, 0)
    out = jax.ShapeDtypeStruct((B, S, gw), bf16)
    return pl.pallas_call(
        _local_kernel,
        grid=(B, S // ts),
        in_specs=[pl.BlockSpec((1, ts, 2 * gw), row),
                  pl.BlockSpec((1, HALO, 2 * gw), prev),
                  pl.BlockSpec((1, ts, gw), row),
                  pl.BlockSpec((1, HALO, gw), prev),
                  pl.BlockSpec((1, ts, 2 * gw), row),
                  pl.BlockSpec(cw.shape, full2), pl.BlockSpec(cb.shape, full2),
                  pl.BlockSpec(cng.shape, full2), pl.BlockSpec(cnb.shape, full2),
                  pl.BlockSpec(pw_bd.shape, full2), pl.BlockSpec(psc.shape, full2),
                  pl.BlockSpec(sg.shape, full2), pl.BlockSpec(sb.shape, full2),
                  pl.BlockSpec(sw.shape, full3), pl.BlockSpec(sbias.shape, full2),
                  pl.BlockSpec(mg.shape, full2)],
        out_specs=[pl.BlockSpec((1, ts, gw), row)] * 3,
        out_shape=[out, out, out],
        scratch_shapes=[pltpu.VMEM((ts + HALO, gw), f32), pltpu.VMEM((SUBLANES - 1, ts + HALO, gw), f32),
                        pltpu.VMEM((ts + HALO, gw), f32),
                        pltpu.VMEM((ts + 16, gw), f32), pltpu.VMEM((ts + 16, gw), f32),
                        pltpu.VMEM((ts + 16, gw), f32)],
        compiler_params=_cparams(("arbitrary", "arbitrary")),
        name="local_mixers",
    )(pc, pc, pp, pp, ps, cw, cb, cng, cnb, pw_bd, psc, sg, sb, sw, sbias, mg)


def _t5_bucket_table(max_dist):
    d = np.arange(max_dist, dtype=np.int64)
    max_exact = N_BUCKETS // 2
    nf = np.maximum(d, 1).astype(np.float32)
    large = max_exact + (np.log(nf / np.float32(max_exact)) / np.float32(math.log(T5_MAX_DISTANCE / max_exact))
                         * np.float32(N_BUCKETS - max_exact)).astype(np.int32)
    large = np.minimum(large, N_BUCKETS - 1)
    return np.where(d < max_exact, d, large).astype(np.int32)


_TILE_BASES = (0, MOBA_BLOCK)


def _bias_kernel(tab_ref, o_ref):
    blk, qc = MOBA_BLOCK, MOBA_BLOCK
    table = _t5_bucket_table(2 * blk + qc)
    first = [int(np.argmax(table >= b)) for b in range(N_BUCKETS)]
    j = lax.broadcasted_iota(i32, (blk, qc), 0)
    q = lax.broadcasted_iota(i32, (blk, qc), 1)
    for t, base in enumerate(_TILE_BASES):
        d = base + q - j
        lo, hi = max(base - (blk - 1), 0), base + qc - 1
        for h in range(HEADS):
            val = jnp.full((blk, qc), tab_ref[h], f32)
            for b in range(1, N_BUCKETS):
                if first[b] > hi:
                    continue
                if first[b] <= lo:
                    val = jnp.full((blk, qc), tab_ref[b * HEADS + h], f32)
                else:
                    val = jnp.where(d >= first[b], tab_ref[b * HEADS + h], val)
            o_ref[t, h] = jnp.where(d >= 0, val * LOG2E, NEG)


def _bias_tiles(rel_bias):
    return pl.pallas_call(
        _bias_kernel,
        in_specs=[pl.BlockSpec(memory_space=pltpu.SMEM)],
        out_shape=jax.ShapeDtypeStruct((len(_TILE_BASES), HEADS, MOBA_BLOCK, MOBA_BLOCK), f32),
        name="bias_tiles",
    )(rel_bias.astype(f32).reshape(-1))


def _attn_kernel(far_ref, qT_ref, k_ref, vT_ref, km_ref, bias_ref, mg_ref, o_ref, mask_ref, sa_ref, sb_ref):
    own = pl.program_id(1)
    blk, gw, hd = MOBA_BLOCK, GROUP_WIDTH, HEAD_DIM
    nb = km_ref.shape[1]

    nio = lax.broadcasted_iota(i32, (nb, blk), 0)
    past = nio < own
    km = km_ref[0]
    q_heads = []
    for h in range(HEADS):
        qh = qT_ref[0, h * hd:(h + 1) * hd, :]
        q_heads.append(qh)
        km_hi, km_lo = _split_bf16(km[:, h * hd:(h + 1) * hd])
        gate = (jnp.dot(km_hi, qh, preferred_element_type=f32)
                + jnp.dot(km_lo, qh, preferred_element_type=f32))
        gate = jnp.where(past, gate, -jnp.inf)
        rank = jnp.zeros((nb, blk), i32)
        for m in range(nb):
            gm = gate[m:m + 1, :]
            beats = (gm > gate) | ((gm == gate) & (m < nio))
            rank = rank + beats.astype(i32)
        sel = (rank < MOBA_TOPK) & past
        mask_ref[0, h] = jnp.where(sel, 0.0, NEG)
        mask_ref[1, h] = jnp.where(sel & (nio < own - 1), far_ref[h] * LOG2E, NEG)

    def qk(n, h):
        kb = k_ref[0, h, pl.ds(pl.multiple_of(n * blk, blk), blk), :]
        return jnp.dot(kb, q_heads[h], preferred_element_type=f32)

    def far_scores(n, h):
        return qk(n, h) + mask_ref[1, h, pl.ds(n, 1), :]

    lane_groups = blk // LANES

    def update(state, scores, n):
        out = []
        for h in range(HEADS):
            vb = vT_ref[0, n, h]
            for c in range(lane_groups):
                u = h * lane_groups + c
                m, acc = state[2 * u:2 * u + 2]
                s = scores[h][:, c * LANES:(c + 1) * LANES]
                m_new = jnp.maximum(m, jnp.max(s, axis=0, keepdims=True))
                alpha = jnp.exp2(m - m_new)
                p = jnp.exp2(s - m_new)
                acc = acc * alpha + jnp.dot(vb, p.astype(bf16), preferred_element_type=f32)
                out += [m_new, acc]
        return tuple(out)

    adj = jnp.maximum(own - 1, 0)
    n_far = jnp.maximum(own - 1, 0)
    s_own = [qk(own, h) + bias_ref[0, h] for h in range(HEADS)]
    s_adj = [qk(adj, h) + bias_ref[1, h] + mask_ref[0, h, pl.ds(adj, 1), :] for h in range(HEADS)]
    for h in range(HEADS):
        sa_ref[h] = far_scores(0, h)

    state = []
    for h in range(HEADS):
        vb = vT_ref[0, own, h]
        for c in range(lane_groups):
            s = s_own[h][:, c * LANES:(c + 1) * LANES]
            m0 = jnp.max(s, axis=0, keepdims=True)
            p = jnp.exp2(s - m0)
            state += [m0, jnp.dot(vb, p.astype(bf16), preferred_element_type=f32)]
    state = update(tuple(state), s_adj, adj)

    def body(i, state):
        first = jnp.minimum(2 * i, nb - 1)
        second = jnp.minimum(2 * i + 1, nb - 1)
        third = jnp.minimum(2 * i + 2, nb - 1)
        for h in range(HEADS):
            sb_ref[h] = far_scores(second, h)
        state = update(state, [sa_ref[h] for h in range(HEADS)], first)
        for h in range(HEADS):
            sa_ref[h] = far_scores(third, h)
        return update(state, [sb_ref[h] for h in range(HEADS)], second)

    fin = lax.fori_loop(0, (n_far + 1) // 2, body, state)

    def normalised(acc):
        return acc[0:hd, :] / acc[hd:hd + 1, :]

    outT = jnp.concatenate(
        [jnp.concatenate([normalised(fin[2 * (h * lane_groups + c) + 1]) for c in range(lane_groups)], axis=1)
         for h in range(HEADS)], axis=0)
    o_ref[0] = _group_rms(outT.T, mg_ref[:, gw:2 * gw]).astype(bf16)


def _moba_attention(qT, k, vT, kmean, bias_tiles, far_bias, mg):
    B, _, S, _ = k.shape
    gw, blk = GROUP_WIDTH, MOBA_BLOCK
    nb = kmean.shape[1]
    return pl.pallas_call(
        _attn_kernel,
        grid=(B, nb),
        in_specs=[pl.BlockSpec(memory_space=pltpu.SMEM),
                  pl.BlockSpec((1, gw, blk), lambda b, c: (b, 0, c)),
                  pl.BlockSpec((1, HEADS, S, HEAD_DIM), lambda b, c: (b, 0, 0, 0)),
                  pl.BlockSpec((1, nb, HEADS, V_ROWS, blk), lambda b, c: (b, 0, 0, 0, 0)),
                  pl.BlockSpec((1, nb, gw), lambda b, c: (b, 0, 0)),
                  pl.BlockSpec(bias_tiles.shape, lambda b, c: (0, 0, 0, 0)),
                  pl.BlockSpec(mg.shape, lambda b, c: (0, 0))],
        out_specs=pl.BlockSpec((1, blk, gw), lambda b, c: (b, c, 0)),
        out_shape=jax.ShapeDtypeStruct((B, S, gw), bf16),
        scratch_shapes=[pltpu.VMEM((2, HEADS, nb, blk), f32),
                        pltpu.VMEM((HEADS, blk, blk), f32), pltpu.VMEM((HEADS, blk, blk), f32)],
        compiler_params=_cparams(("arbitrary", "arbitrary")),
        name="moba_attention",
    )(far_bias, qT, k, vT, kmean, bias_tiles, mg)


def _outproj_router_kernel(yc_ref, ya_ref, yp_ref, ys_ref, x_ref, g1_ref, wo_ref, n2_ref, sh_ref, sc_ref,
                           rw_ref, rb_ref,
                           x1_ref, h2_ref, dest_ref, gate_ref, pad_ref):
    W = ROUTE_TILE
    mixed = [jnp.concatenate([r[u * W:(u + 1) * W, :] for r in (yc_ref, ya_ref, yp_ref, ys_ref)], axis=1)
             for u in range(ROUTE_WINDOWS_PER_STEP)]
    projected = [jnp.dot(m, wo_ref[...], preferred_element_type=f32) for m in mixed]
    for u in range(ROUTE_WINDOWS_PER_STEP):
        _route_window(u, projected[u], x_ref, g1_ref, n2_ref, sh_ref, sc_ref, rw_ref, rb_ref,
                      x1_ref, h2_ref, dest_ref, gate_ref, pad_ref)


def _route_window(u, projected, x_ref, g1_ref, n2_ref, sh_ref, sc_ref, rw_ref, rb_ref,
                  x1_ref, h2_ref, dest_ref, gate_ref, pad_ref):
    W = ROUTE_TILE
    rows = slice(u * W, (u + 1) * W)
    x1 = x_ref[rows, :] + g1_ref[0] * projected
    x1_ref[rows, :] = x1
    ms = jnp.mean(x1 * x1, axis=-1, keepdims=True)
    h = x1 * lax.rsqrt(ms + EPS) * n2_ref[...]
    h = h * (1.0 + sc_ref[0]) + sh_ref[0]
    h_hi, h_lo = _split_bf16(h)
    h2_ref[rows, :] = h_hi

    nt = (((1,), (1,)), ((), ()))
    rw_hi, rw_lo = _split_bf16(rw_ref[...])
    logits = (lax.dot_general(rw_hi, h_hi, nt, preferred_element_type=f32)
              + lax.dot_general(rw_hi, h_lo, nt, preferred_element_type=f32)
              + lax.dot_general(rw_lo, h_hi, nt, preferred_element_type=f32)) + rb_ref[...]
    eio = lax.broadcasted_iota(i32, (N_EXPERTS, W), 0)
    work = logits
    vals, sels = [], []
    for k in range(TOP_K):
        m = jnp.max(work, axis=0, keepdims=True)
        idx = jnp.min(jnp.where(work == m, eio, N_EXPERTS), axis=0, keepdims=True)
        sel = eio == idx
        vals.append(m)
        sels.append(sel)
        work = jnp.where(sel, -jnp.inf, work)
    exps = [jnp.exp(v - vals[0]) for v in vals]
    denom = exps[0] + exps[1] + exps[2] + exps[3]
    for k in range(TOP_K):
        gate_ref[k:k + 1, rows] = exps[k] / denom

    multi = jnp.zeros((N_EXPERTS, W), f32)
    for sel in sels:
        multi = multi + sel.astype(f32)
    before = (lax.broadcasted_iota(i32, (W, W), 0) < lax.broadcasted_iota(i32, (W, W), 1)).astype(bf16)
    earlier = jnp.dot(multi.astype(bf16), before, preferred_element_type=f32)
    cnt = jnp.sum(multi, axis=1, keepdims=True).astype(i32)
    padded = (cnt + (ROW_CHUNK - 1)) // ROW_CHUNK * ROW_CHUNK
    pad_ref[u] = padded
    lower = (lax.broadcasted_iota(i32, (N_EXPERTS, N_EXPERTS), 1)
             < lax.broadcasted_iota(i32, (N_EXPERTS, N_EXPERTS), 0)).astype(bf16)
    seg_start = jnp.dot(lower, jnp.broadcast_to(padded.astype(f32), (N_EXPERTS, W)).astype(bf16),
                        preferred_element_type=f32)
    row = seg_start + earlier
    for k in range(TOP_K):
        dest_ref[k:k + 1, rows] = jnp.sum(jnp.where(sels[k], row, 0.0), axis=0, keepdims=True).astype(i32)


def _outproj_router(yc, ya, yp, ys, x, gate1, wo_bf, n2g, shift2, scale2, rwT, rb, seq):
    N, D = x.shape
    gw = GROUP_WIDTH
    nw = N // ROUTE_TILE
    wps = ROUTE_WINDOWS_PER_STEP
    W = ROUTE_TILE * wps
    per_b = seq // W
    row = lambda i: (i, 0)
    vec = lambda i: (i // per_b, 0, 0)
    full = lambda i: (0, 0)
    colblk = lambda i: (0, i)
    return pl.pallas_call(
        _outproj_router_kernel,
        grid=(nw // wps,),
        in_specs=[pl.BlockSpec((W, gw), row)] * 4 + [
            pl.BlockSpec((W, D), row),
            pl.BlockSpec((1, 1, D), vec),
            pl.BlockSpec(wo_bf.shape, full),
            pl.BlockSpec((1, D), full),
            pl.BlockSpec((1, 1, D), vec),
            pl.BlockSpec((1, 1, D), vec),
            pl.BlockSpec(rwT.shape, full),
            pl.BlockSpec(rb.shape, full)],
        out_specs=[pl.BlockSpec((W, D), row),
                   pl.BlockSpec((W, D), row),
                   pl.BlockSpec((TOP_K, W), colblk),
                   pl.BlockSpec((TOP_K, W), colblk),
                   pl.BlockSpec((wps, N_EXPERTS, 1), lambda i: (i, 0, 0))],
        out_shape=[jax.ShapeDtypeStruct((N, D), f32),
                   jax.ShapeDtypeStruct((N, D), bf16),
                   jax.ShapeDtypeStruct((TOP_K, N), i32),
                   jax.ShapeDtypeStruct((TOP_K, N), f32),
                   jax.ShapeDtypeStruct((nw, N_EXPERTS, 1), i32)],
        compiler_params=_cparams(("arbitrary",)),
        name="outproj_router",
    )(yc, ya, yp, ys, x, gate1, wo_bf, n2g, shift2, scale2, rwT, rb)


def _max_window_rows():
    return -(-(ROUTE_TILE * TOP_K + N_EXPERTS * (ROW_CHUNK - 1)) // 128) * 128


def _pack_pairs(lo, hi):
    lo_bits = lax.bitcast_convert_type(lo, jnp.uint32)
    hi_bits = lax.bitcast_convert_type(hi, jnp.uint32)
    return lax.bitcast_convert_type(hi_bits | (lo_bits >> 16), i32)


def _unpack_pairs(words):
    bits = lax.bitcast_convert_type(words, jnp.uint32)
    lo = lax.bitcast_convert_type(bits << 16, f32).astype(bf16)
    hi = lax.bitcast_convert_type(bits & jnp.uint32(0xFFFF0000), f32).astype(bf16)
    return lo, hi


def _dispatch_kernel(lstart_s, ncopy_s, gbase_s, wcopies_s, tail_start_s, tail_chunks_s, n_used_s,
                     h_ref, dest_ref, xs_ref, sorted_ref, zero_ref, sem):
    w = pl.program_id(0)
    nw = pl.num_programs(0)
    W = ROUTE_TILE
    R = sorted_ref.shape[1]
    half = sorted_ref.shape[2]
    slot = w % 2
    rio = lax.broadcasted_iota(i32, (R, W), 0)
    hit = rio == dest_ref[0:1, :]
    for k in range(1, TOP_K):
        hit = hit | (rio == dest_ref[k:k + 1, :])
    onehot = jnp.where(hit, 1.0, 0.0).astype(bf16)
    rows = jnp.dot(onehot, h_ref[...], preferred_element_type=f32)
    sorted_ref[slot] = _pack_pairs(rows[:, :half], rows[:, half:])

    def piece_copy(s, src_row, dst_row):
        return pltpu.make_async_copy(
            sorted_ref.at[s, pl.ds(pl.multiple_of(src_row, ROW_CHUNK), DISPATCH_COPY_ROWS), :],
            xs_ref.at[pl.ds(pl.multiple_of(dst_row, ROW_CHUNK), DISPATCH_COPY_ROWS), :], sem.at[s])

    def zero_copy(dst_row):
        return pltpu.make_async_copy(
            zero_ref.at[0:ROW_CHUNK, :],
            xs_ref.at[pl.ds(pl.multiple_of(dst_row, ROW_CHUNK), ROW_CHUNK), :], sem.at[0])

    def zero_tile_copy(tile):
        return pltpu.make_async_copy(
            zero_ref, xs_ref.at[pl.ds(pl.multiple_of(tile * FFN_TILE, FFN_TILE), FFN_TILE), :], sem.at[0])

    def drain(s, count):
        @pl.when(count > 0)
        def _():
            rows = pl.multiple_of(count * DISPATCH_COPY_ROWS, DISPATCH_COPY_ROWS)
            pltpu.make_async_copy(xs_ref.at[pl.ds(0, rows), :], xs_ref.at[pl.ds(0, rows), :], sem.at[s]).wait()

    @pl.when(w > 0)
    def _():
        drain(1 - slot, wcopies_s[jnp.maximum(w - 1, 0)])

    def per_expert(e, c):
        n = ncopy_s[w * N_EXPERTS + e]
        src = lstart_s[w * N_EXPERTS + e]
        dst = gbase_s[w * N_EXPERTS + e]

        def issue(j, c):
            piece_copy(slot, src + j * DISPATCH_COPY_ROWS, dst + j * DISPATCH_COPY_ROWS).start()
            return c
        lax.fori_loop(0, n, issue, 0)
        return c

    lax.fori_loop(0, N_EXPERTS, per_expert, 0)

    @pl.when(w == nw - 1)
    def _():
        drain(slot, wcopies_s[w])
        zero_ref[...] = jnp.zeros(zero_ref.shape, i32)

        def per_tail(e, total):
            n = tail_chunks_s[e]
            dst = tail_start_s[e]

            def issue(j, c):
                zero_copy(dst + j * ROW_CHUNK).start()
                return c
            lax.fori_loop(0, n, issue, 0)
            return total + n
        tails = lax.fori_loop(0, N_EXPERTS, per_tail, 0)

        def drain_tail(j, c):
            zero_copy(0).wait()
            return c
        lax.fori_loop(0, tails, drain_tail, 0)

        n_tiles = xs_ref.shape[0] // FFN_TILE

        def issue_tile(j, c):
            zero_tile_copy(j).start()
            return c
        lax.fori_loop(n_used_s[0], n_tiles, issue_tile, 0)

        def drain_tile(j, c):
            zero_tile_copy(0).wait()
            return c
        lax.fori_loop(n_used_s[0], n_tiles, drain_tile, 0)


def _dispatch(h2, destT, t, p_rows):
    N, D = h2.shape
    W = ROUTE_TILE
    nw = N // W
    R = _max_window_rows()
    assert R >= W * TOP_K + N_EXPERTS * (ROW_CHUNK - 1) + DISPATCH_COPY_ROWS - ROW_CHUNK
    grid_spec = pltpu.PrefetchScalarGridSpec(
        num_scalar_prefetch=7,
        grid=(nw,),
        in_specs=[pl.BlockSpec((W, D), lambda w, *_: (w, 0)),
                  pl.BlockSpec((TOP_K, W), lambda w, *_: (0, w))],
        out_specs=pl.BlockSpec(memory_space=pl.ANY),
        scratch_shapes=[pltpu.VMEM((2, R, D // 2), i32), pltpu.VMEM((FFN_TILE, D // 2), i32),
                        pltpu.SemaphoreType.DMA((2,))],
    )
    return pl.pallas_call(
        _dispatch_kernel,
        grid_spec=grid_spec,
        out_shape=jax.ShapeDtypeStruct((p_rows, D // 2), i32),
        compiler_params=_cparams(("arbitrary",)),
        name="expert_dispatch",
    )(t['lstart'], t['ncopy'], t['gbase'], t['wcopies'], t['tail_start'], t['tail_chunks'], t['n_used'],
      h2, destT)


def _ffn_kernel(tile_expert_s, next_expert_s, n_used_s, x_ref, w1_hbm, b1_ref, w2_hbm, b2_ref, y_ref,
                w1f_ref, w2f_ref, w1b_ref, w2b_ref, sem, *, layer):
    i = pl.program_id(0)
    last = n_used_s[0] - 1
    expert = tile_expert_s[jnp.minimum(i, last)]
    prev_expert = tile_expert_s[jnp.minimum(jnp.maximum(i - 1, 0), last)]

    def weight_copies(e):
        return (pltpu.make_async_copy(w1_hbm.at[layer, e], w1f_ref, sem.at[0]),
                pltpu.make_async_copy(w2_hbm.at[layer, e], w2f_ref, sem.at[1]))

    @pl.when(i == 0)
    def _():
        for c in weight_copies(expert):
            c.start()

    @pl.when((i == 0) | (expert != prev_expert))
    def _():
        for c in weight_copies(expert):
            c.wait()
        w1b_ref[...] = w1f_ref[...].astype(bf16)
        w2b_ref[...] = w2f_ref[...].astype(bf16)
        nxt = next_expert_s[expert]

        @pl.when(nxt < N_EXPERTS)
        def _():
            for c in weight_copies(nxt):
                c.start()

    @pl.when(i < n_used_s[0])
    def _():
        dff = w2b_ref.shape[0]
        half = x_ref.shape[1]
        sub = x_ref.shape[0] // FFN_SUBTILES
        hidden = []
        for r in range(FFN_SUBTILES):
            x_lo, x_hi = _unpack_pairs(x_ref[r * sub:(r + 1) * sub, :])
            hidden.append(jnp.dot(x_lo, w1b_ref[0:half, :], preferred_element_type=f32)
                          + jnp.dot(x_hi, w1b_ref[half:2 * half, :], preferred_element_type=f32))
        for r in range(FFN_SUBTILES):
            hh = hidden[r] + b1_ref[0, 0]
            x_glu = jnp.minimum(hh[:, :dff], SWIGLU_LIMIT)
            x_lin = jnp.clip(hh[:, dff:], -SWIGLU_LIMIT, SWIGLU_LIMIT)
            act = x_glu * jax.nn.sigmoid(SWIGLU_ALPHA * x_glu) * (x_lin + 1.0)
            y = jnp.dot(act.astype(bf16), w2b_ref[...], preferred_element_type=f32) + b2_ref[0, 0]
            y = y.astype(bf16).astype(f32)
            y_ref[r * sub:(r + 1) * sub, :] = _pack_pairs(y[:, :half], y[:, half:])

    @pl.when(i >= n_used_s[0])
    def _():
        y_ref[...] = jnp.zeros(y_ref.shape, i32)


def _expert_ffn(xs, w1, b1, w2, b2, t, layer):
    P, half = xs.shape
    L, E, D, F2 = w1.shape
    tm = FFN_TILE
    nt = P // tm

    def tile(i, te, ne, nu):
        return (jnp.minimum(i, nu[0] - 1), 0)

    def expert4(i, te, ne, nu):
        return (layer, te[jnp.minimum(i, nu[0] - 1)], 0, 0)

    grid_spec = pltpu.PrefetchScalarGridSpec(
        num_scalar_prefetch=3,
        grid=(nt,),
        in_specs=[pl.BlockSpec((tm, half), tile),
                  pl.BlockSpec(memory_space=pl.ANY),
                  pl.BlockSpec((1, 1, 1, F2), expert4),
                  pl.BlockSpec(memory_space=pl.ANY),
                  pl.BlockSpec((1, 1, 1, D), expert4)],
        out_specs=pl.BlockSpec((tm, half), lambda i, te, ne, nu: (i, 0)),
        scratch_shapes=[pltpu.VMEM((D, F2), f32), pltpu.VMEM((F2 // 2, D), f32),
                        pltpu.VMEM((D, F2), bf16), pltpu.VMEM((F2 // 2, D), bf16),
                        pltpu.SemaphoreType.DMA((2,))],
    )
    return pl.pallas_call(
        functools.partial(_ffn_kernel, layer=layer),
        grid_spec=grid_spec,
        out_shape=jax.ShapeDtypeStruct((P, half), i32),
        compiler_params=_cparams(("arbitrary",)),
        name="expert_ffn",
    )(t['tile_expert'], t['next_expert'], t['n_used'], xs, w1, b1.reshape(L, E, 1, F2), w2,
      b2.reshape(L, E, 1, D))


def _combine_kernel(lstart_s, nchunk_s, gbase_s, wchunks_s,
                    ys_ref, dest_ref, gate_ref, x1_ref, g2_ref, fg_ref, o_ref, local_ref, sem, *, final):
    w = pl.program_id(0)
    nw = pl.num_programs(0)
    W = ROUTE_TILE
    R = local_ref.shape[1]
    slot = w % 2

    def chunk_copy(s, src_row, dst_row):
        return pltpu.make_async_copy(
            ys_ref.at[pl.ds(pl.multiple_of(src_row, ROW_CHUNK), ROW_CHUNK), :],
            local_ref.at[s, pl.ds(pl.multiple_of(dst_row, ROW_CHUNK), ROW_CHUNK), :], sem.at[s])

    def fetch(win, s):
        def per_expert(e, c):
            n = nchunk_s[win * N_EXPERTS + e]
            dst = lstart_s[win * N_EXPERTS + e]
            src = gbase_s[win * N_EXPERTS + e]

            def issue(j, c):
                chunk_copy(s, src + j * ROW_CHUNK, dst + j * ROW_CHUNK).start()
                return c
            lax.fori_loop(0, n, issue, 0)
            return c
        lax.fori_loop(0, N_EXPERTS, per_expert, 0)

    @pl.when(w == 0)
    def _():
        local_ref[...] = jnp.zeros(local_ref.shape, i32)
        fetch(0, 0)

    @pl.when(w + 1 < nw)
    def _():
        fetch(jnp.minimum(w + 1, nw - 1), 1 - slot)

    @pl.when(wchunks_s[w] > 0)
    def _():
        rows = pl.multiple_of(wchunks_s[w] * ROW_CHUNK, ROW_CHUNK)
        pltpu.make_async_copy(ys_ref.at[pl.ds(0, rows), :], local_ref.at[slot, pl.ds(0, rows), :],
                              sem.at[slot]).wait()

    rio = lax.broadcasted_iota(i32, (R, W), 0)
    weights = jnp.zeros((R, W), f32)
    for k in range(TOP_K):
        weights = jnp.where(rio == dest_ref[k:k + 1, :], gate_ref[k:k + 1, :], weights)
    weights = weights.astype(bf16)
    y_lo, y_hi = _unpack_pairs(local_ref[slot])
    tn = (((0,), (0,)), ((), ()))
    moe = jnp.concatenate([lax.dot_general(weights, y_lo, tn, preferred_element_type=f32),
                           lax.dot_general(weights, y_hi, tn, preferred_element_type=f32)], axis=1)
    x2 = x1_ref[...] + g2_ref[0] * moe
    if final:
        ms = jnp.mean(x2 * x2, axis=-1, keepdims=True)
        x2 = x2 * lax.rsqrt(ms + EPS) * fg_ref[...]
    o_ref[...] = x2


def _combine(ys, dest, gates, x1, gate2, final_g, t, seq, final):
    N, D = x1.shape
    W = ROUTE_TILE
    nw = N // W
    per_b = seq // W
    R = _max_window_rows()
    grid_spec = pltpu.PrefetchScalarGridSpec(
        num_scalar_prefetch=4,
        grid=(nw,),
        in_specs=[pl.BlockSpec(memory_space=pl.ANY),
                  pl.BlockSpec((TOP_K, W), lambda w, *_: (0, w)),
                  pl.BlockSpec((TOP_K, W), lambda w, *_: (0, w)),
                  pl.BlockSpec((W, D), lambda w, *_: (w, 0)),
                  pl.BlockSpec((1, 1, D), lambda w, *_: (w // per_b, 0, 0)),
                  pl.BlockSpec((1, D), lambda w, *_: (0, 0))],
        out_specs=pl.BlockSpec((W, D), lambda w, *_: (w, 0)),
        scratch_shapes=[pltpu.VMEM((2, R, D // 2), i32), pltpu.SemaphoreType.DMA((2,))],
    )
    return pl.pallas_call(
        functools.partial(_combine_kernel, final=final),
        grid_spec=grid_spec,
        out_shape=jax.ShapeDtypeStruct((N, D), f32),
        compiler_params=_cparams(("arbitrary",)),
        name="expert_combine",
    )(t['lstart'], t['nchunk'], t['gbase'], t['wchunks'], ys, dest, gates, x1, gate2, final_g)


def _routing_tables(padded, n_tiles):
    nw, E = padded.shape
    lstart = jnp.cumsum(padded, axis=1) - padded
    tot = jnp.sum(padded, axis=0)
    slack = DISPATCH_COPY_ROWS - ROW_CHUNK
    region = jnp.where(tot > 0, (tot + slack + FFN_TILE - 1) // FFN_TILE * FFN_TILE, 0)
    region_end = jnp.cumsum(region)
    region_start = region_end - region
    gbase = region_start[None, :] + jnp.cumsum(padded, axis=0) - padded
    nchunk = padded // ROW_CHUNK
    tail_start = region_start + tot
    tail_chunks = (region - tot) // ROW_CHUNK
    n_used = (region_end[-1] // FFN_TILE).astype(i32).reshape(1)
    tile_row = jnp.arange(n_tiles, dtype=i32) * FFN_TILE
    tile_expert = jnp.minimum(
        jnp.sum((region_end[None, :] <= tile_row[:, None]).astype(i32), axis=1), E - 1).astype(i32)
    eidx = jnp.arange(E, dtype=i32)
    later_nonempty = (eidx[None, :] > eidx[:, None]) & (region[None, :] > 0)
    next_expert = jnp.min(jnp.where(later_nonempty, eidx[None, :], E), axis=1)
    flat = lambda a: a.reshape(-1).astype(i32)
    ncopy = (padded + DISPATCH_COPY_ROWS - 1) // DISPATCH_COPY_ROWS
    return dict(lstart=flat(lstart), nchunk=flat(nchunk), gbase=flat(gbase), next_expert=flat(next_expert),
                wchunks=flat(jnp.sum(nchunk, axis=1)), ncopy=flat(ncopy), wcopies=flat(jnp.sum(ncopy, axis=1)),
                tail_start=flat(tail_start), tail_chunks=flat(tail_chunks),
                n_used=n_used, tile_expert=tile_expert)


def _block_diag(w):
    g, a, b = w.shape
    out = jnp.zeros((g * a, g * b), w.dtype)
    for i in range(g):
        out = out.at[i * a:(i + 1) * a, i * b:(i + 1) * b].set(w[i])
    return out


def kernel(x, c, w_mod, b_mod, norm1_g, w_in, conv_w, conv_b, conv_norm_g, conv_norm_b, rel_bias, pool_w, pool_scale, sgu_norm_g, sgu_norm_b, sgu_w, sgu_b, mix_out_g, w_out, norm2_g, router_w, router_b, exp_w1, exp_b1, exp_w2, exp_b2, final_norm_g):
    B, S, D = x.shape
    L = w_mod.shape[0]
    N = B * S
    nw = N // ROUTE_TILE
    assert S % SEQ_TILE == 0 and S % MOBA_BLOCK == 0 and N % ROUTE_TILE == 0 and S % ROUTE_TILE == 0
    p_bound = (N * TOP_K + nw * N_EXPERTS * (ROW_CHUNK - 1)
               + N_EXPERTS * (DISPATCH_COPY_ROWS - ROW_CHUNK + FFN_TILE - 1))
    n_tiles = -(-p_bound // FFN_TILE)
    p_rows = n_tiles * FFN_TILE

    mod = _modulation(c, w_mod, b_mod)
    bias_tiles = _bias_tiles(rel_bias)
    far_bucket = int(_t5_bucket_table(MOBA_BLOCK + 2)[MOBA_BLOCK + 1])
    assert far_bucket == int(_t5_bucket_table(S + 1)[S])
    far_bias = rel_bias[far_bucket].astype(f32)
    row = lambda a: a.reshape(1, -1)
    for l in range(L):
        m6 = mod[l].reshape(B, 6, 1, D)
        shift1, scale1, gate1, shift2, scale2, gate2 = (m6[:, j] for j in range(6))
        mg = row(mix_out_g[l])
        pc, qT, k, vT, kmean, pp, ps = _in_projection(x, shift1, scale1, row(norm1_g[l]), w_in[l].astype(bf16))
        yc, yp, ys = _local_mixers(
            pc, pp, ps, conv_w[l], row(conv_b[l]), row(conv_norm_g[l]), row(conv_norm_b[l]),
            _block_diag(pool_w[l]).astype(bf16), row(pool_scale[l]),
            row(sgu_norm_g[l]), row(sgu_norm_b[l]), sgu_w[l],
            jnp.repeat(sgu_b[l].T, HEAD_DIM, axis=1), mg)
        ya = _moba_attention(qT, k, vT, kmean.reshape(B, -1, GROUP_WIDTH), bias_tiles, far_bias, mg)
        flat = lambda a: a.reshape(N, -1)
        x1, h2, destT, gateT, padded = _outproj_router(
            flat(yc), flat(ya), flat(yp), flat(ys), x.reshape(N, D), gate1, w_out[l].astype(bf16),
            row(norm2_g[l]), shift2, scale2, router_w[l].T, router_b[l].reshape(-1, 1), S)
        t = _routing_tables(padded.reshape(nw, N_EXPERTS), n_tiles)
        xs = _dispatch(h2, destT, t, p_rows)
        ysort = _expert_ffn(xs, exp_w1, exp_b1, exp_w2, exp_b2, t, l)
        x = _combine(ysort, destT, gateT, x1, gate2, row(final_norm_g), t, S,
                     final=(l == L - 1)).reshape(B, S, D)
    return x
```

```python
import functools
import math

import numpy as np
import jax
import jax.numpy as jnp
from jax import lax
from jax.experimental import pallas as pl
from jax.experimental.pallas import tpu as pltpu

f32, bf16, i32 = jnp.float32, jnp.bfloat16, jnp.int32

GROUP_WIDTH = 256
HEADS = 4
HEAD_DIM = 64
V_ROWS = HEAD_DIM + 16
LOG2E = math.log2(math.e)
CONV_WIDTH = 31
MOBA_BLOCK = 256
MOBA_TOPK = 3
Q_CHUNK = 128
N_BUCKETS = 32
T5_MAX_DISTANCE = 128
POOL_WINDOWS = (2, 4, 8, 16)
SGU_CHUNK = 128
N_EXPERTS = 32
TOP_K = 4
SWIGLU_LIMIT = 7.0
SWIGLU_ALPHA = 1.702
EPS = 1e-6

SUBLANES = 8
LANES = 128
HALO = 32
SEQ_TILE = 512
ROUTE_TILE = 256
ROUTE_WINDOWS_PER_STEP = 4
ROW_CHUNK = 8
DISPATCH_COPY_ROWS = 32
FFN_TILE = 512
FFN_SUBTILES = 2
NEG = -1e30
VMEM_LIMIT = 56 * 1024 * 1024


def _cparams(sem):
    return pltpu.CompilerParams(dimension_semantics=sem, vmem_limit_bytes=VMEM_LIMIT)


def _split_bf16(a):
    hi = a.astype(bf16)
    lo = (a - hi.astype(f32)).astype(bf16)
    return hi, lo


def _mod_kernel(c_ref, w_ref, b_ref, o_ref):
    c = c_ref[...]
    cond = c * jax.nn.sigmoid(c)
    o_ref[0] = jnp.dot(cond, w_ref[0], preferred_element_type=f32,
                       precision=lax.Precision.HIGHEST) + b_ref[0]


def _modulation(c, w_mod, b_mod):
    L, D, M = w_mod.shape
    B = c.shape[0]
    tn = 1536
    return pl.pallas_call(
        _mod_kernel,
        grid=(L, M // tn),
        in_specs=[pl.BlockSpec((B, D), lambda l, j: (0, 0)),
                  pl.BlockSpec((1, D, tn), lambda l, j: (l, 0, j)),
                  pl.BlockSpec((1, 1, tn), lambda l, j: (l, 0, j))],
        out_specs=pl.BlockSpec((1, B, tn), lambda l, j: (l, 0, j)),
        out_shape=jax.ShapeDtypeStruct((L, B, M), f32),
        compiler_params=_cparams(("arbitrary", "arbitrary")),
        name="modulation",
    )(c, w_mod, b_mod.reshape(L, 1, M))


def _inproj_kernel(x_ref, sh_ref, sc_ref, g_ref, w_ref,
                   pc_ref, qT_ref, k_ref, vT_ref, km_ref, pp_ref, ps_ref):
    x = x_ref[0]
    ms = jnp.mean(x * x, axis=-1, keepdims=True)
    h = x * lax.rsqrt(ms + EPS) * g_ref[...]
    h = h * (1.0 + sc_ref[0]) + sh_ref[0]
    proj = jnp.dot(h.astype(bf16), w_ref[...], preferred_element_type=f32)
    gw = GROUP_WIDTH
    pc_ref[0] = proj[:, 0:2 * gw]
    q = proj[:, 2 * gw:3 * gw] * (HEAD_DIM ** -0.5 * LOG2E)
    qT_ref[0] = q.T.astype(bf16)
    kk = proj[:, 3 * gw:4 * gw]
    for h in range(HEADS):
        k_ref[0, h] = kk[:, h * HEAD_DIM:(h + 1) * HEAD_DIM].astype(bf16)
    for j in range(SEQ_TILE // MOBA_BLOCK):
        km_ref[0, 0, j:j + 1, :] = jnp.mean(kk[j * MOBA_BLOCK:(j + 1) * MOBA_BLOCK], axis=0, keepdims=True)
    for j in range(SEQ_TILE // MOBA_BLOCK):
        vT = proj[j * MOBA_BLOCK:(j + 1) * MOBA_BLOCK, 4 * gw:5 * gw].T.astype(bf16)
        for h in range(HEADS):
            vT_ref[0, j, h] = jnp.concatenate(
                [vT[h * HEAD_DIM:(h + 1) * HEAD_DIM, :], jnp.ones((V_ROWS - HEAD_DIM, MOBA_BLOCK), bf16)], axis=0)
    pp_ref[0] = proj[:, 5 * gw:6 * gw]
    ps_ref[0] = proj[:, 6 * gw:8 * gw]


def _in_projection(x, shift, scale, g, w_bf):
    B, S, D = x.shape
    gw = GROUP_WIDTH
    ts = SEQ_TILE
    nt = S // ts
    row = lambda b, i: (b, i, 0)
    col = lambda b, i: (b, 0, i)
    vec = lambda b, i: (b, 0, 0)
    return pl.pallas_call(
        _inproj_kernel,
        grid=(B, nt),
        in_specs=[pl.BlockSpec((1, ts, D), row),
                  pl.BlockSpec((1, 1, D), vec),
                  pl.BlockSpec((1, 1, D), vec),
                  pl.BlockSpec((1, D), lambda b, i: (0, 0)),
                  pl.BlockSpec(w_bf.shape, lambda b, i: (0, 0))],
        out_specs=[pl.BlockSpec((1, ts, 2 * gw), row),
                   pl.BlockSpec((1, gw, ts), col),
                   pl.BlockSpec((1, HEADS, ts, HEAD_DIM), lambda b, i: (b, 0, i, 0)),
                   pl.BlockSpec((1, ts // MOBA_BLOCK, HEADS, V_ROWS, MOBA_BLOCK), lambda b, i: (b, i, 0, 0, 0)),
                   pl.BlockSpec((1, 1, ts // MOBA_BLOCK, gw), lambda b, i: (b, i, 0, 0)),
                   pl.BlockSpec((1, ts, gw), row),
                   pl.BlockSpec((1, ts, 2 * gw), row)],
        out_shape=[jax.ShapeDtypeStruct((B, S, 2 * gw), f32),
                   jax.ShapeDtypeStruct((B, gw, S), bf16),
                   jax.ShapeDtypeStruct((B, HEADS, S, HEAD_DIM), bf16),
                   jax.ShapeDtypeStruct((B, S // MOBA_BLOCK, HEADS, V_ROWS, MOBA_BLOCK), bf16),
                   jax.ShapeDtypeStruct((B, nt, ts // MOBA_BLOCK, gw), f32),
                   jax.ShapeDtypeStruct((B, S, gw), f32),
                   jax.ShapeDtypeStruct((B, S, 2 * gw), f32)],
        compiler_params=_cparams(("arbitrary", "arbitrary")),
        name="in_projection",
    )(x, shift, scale, g, w_bf)


def _group_rms(y, g):
    return y * lax.rsqrt(jnp.mean(y * y, axis=-1, keepdims=True) + EPS) * g


def _local_kernel(pc_ref, pcp_ref, pp_ref, ppp_ref, ps_ref,
                  cw_ref, cb_ref, cng_ref, cnb_ref, pw_ref, psc_ref,
                  sg_ref, sb_ref, sw_ref, sbias_ref, mg_ref,
                  yc_ref, yp_ref, ys_ref,
                  gext, gshift, zext, s2, s4, s8):
    i = pl.program_id(1)
    ts = SEQ_TILE
    gw = GROUP_WIDTH
    first = i == 0
    lane = lax.broadcasted_iota(i32, (1, gw), 1)

    pc = pc_ref[0]
    g = pc[:, :gw] * jax.nn.sigmoid(pc[:, gw:])
    ph = pcp_ref[0]
    gh = ph[:, :gw] * jax.nn.sigmoid(ph[:, gw:])
    gext[0:HALO, :] = jnp.where(first, 0.0, gh)
    gext[HALO:HALO + ts, :] = g
    span = ts + HALO - SUBLANES
    for a in range(1, SUBLANES):
        gshift[a - 1, 0:span, :] = gext[pl.ds(a, span), :]
    acc = jnp.zeros((ts, gw), f32)
    for j in range(CONV_WIDTH):
        b, a = divmod(HALO - (CONV_WIDTH - 1) + j, SUBLANES)
        tap = gext[pl.ds(SUBLANES * b, ts), :] if a == 0 else gshift[a - 1, pl.ds(SUBLANES * b, ts), :]
        acc = acc + cw_ref[j:j + 1, :] * tap
    y = acc + cb_ref[...]
    r = lax.broadcasted_iota(i32, (gw, gw), 0) // HEAD_DIM
    c = lax.broadcasted_iota(i32, (gw, gw), 1) // HEAD_DIM
    avg = jnp.where(r == c, 1.0 / HEAD_DIM, 0.0).astype(bf16)

    def head_mean(t):
        hi, lo = _split_bf16(t)
        return (jnp.dot(hi, avg, preferred_element_type=f32)
                + jnp.dot(lo, avg, preferred_element_type=f32))

    mu = head_mean(y)
    yc = y - mu
    var = head_mean(yc * yc)
    yn = yc * lax.rsqrt(var + EPS) * cng_ref[...] + cnb_ref[...]
    yconv = yn * jax.nn.sigmoid(yn)
    yc_ref[0] = _group_rms(yconv, mg_ref[:, 0:gw]).astype(bf16)

    z = pp_ref[0]
    zext[0:HALO, :] = jnp.where(first, 0.0, ppp_ref[0])
    zext[HALO:HALO + ts, :] = z
    n2, n4, n8 = ts + 14, ts + 12, ts + 8
    s2[0:n2, :] = zext[pl.ds(HALO - 14, n2), :] + zext[pl.ds(HALO - 15, n2), :]
    s4[0:n4, :] = s2[pl.ds(2, n4), :] + s2[pl.ds(0, n4), :]
    s8[0:n8, :] = s4[pl.ds(4, n8), :] + s4[pl.ds(0, n8), :]
    w2 = s2[pl.ds(14, ts), :]
    w4 = s4[pl.ds(12, ts), :]
    w8 = s8[pl.ds(8, ts), :]
    w16 = w8 + s8[pl.ds(0, ts), :]
    tpos = (i * ts + lax.broadcasted_iota(i32, (ts, 1), 0) + 1).astype(f32)
    grp = lane // (gw // len(POOL_WINDOWS))
    pooled = jnp.zeros((ts, gw), f32)
    for gi, (w, sw) in enumerate(zip(POOL_WINDOWS, (w2, w4, w8, w16))):
        pooled = jnp.where(grp == gi, sw / jnp.minimum(tpos, float(w)), pooled)
    pooled = pooled - z
    yp = jnp.dot(pooled.astype(bf16), pw_ref[...], preferred_element_type=f32) * psc_ref[...]
    yp_ref[0] = _group_rms(yp, mg_ref[:, 2 * gw:3 * gw]).astype(bf16)

    zz = ps_ref[0]
    zz = 0.5 * zz * (1.0 + lax.erf(zz * (1.0 / math.sqrt(2.0))))
    u = zz[:, :gw]
    v = zz[:, gw:]
    vm = jnp.mean(v, axis=-1, keepdims=True)
    vc = v - vm
    vv = jnp.mean(vc * vc, axis=-1, keepdims=True)
    vn = (vc * lax.rsqrt(vv + EPS) * sg_ref[...] + sb_ref[...]).astype(bf16)
    li = lax.broadcasted_iota(i32, (SGU_CHUNK, SGU_CHUNK), 0)
    lj = lax.broadcasted_iota(i32, (SGU_CHUNK, SGU_CHUNK), 1)
    head_of_lane = lane // HEAD_DIM
    wts = [jnp.where(li >= lj, sw_ref[h], 0.0).astype(bf16) for h in range(HEADS)]
    outs = []
    for n in range(ts // SGU_CHUNK):
        vch = vn[n * SGU_CHUNK:(n + 1) * SGU_CHUNK]
        mixed = sbias_ref[...]
        for h in range(HEADS):
            mh = jnp.dot(wts[h], vch, preferred_element_type=f32)
            mixed = mixed + jnp.where(head_of_lane == h, mh, 0.0)
        outs.append(u[n * SGU_CHUNK:(n + 1) * SGU_CHUNK] * mixed)
    ysgu = jnp.concatenate(outs, axis=0)
    ys_ref[0] = _group_rms(ysgu, mg_ref[:, 3 * gw:4 * gw]).astype(bf16)


def _local_mixers(pc, pp, ps, cw, cb, cng, cnb, pw_bd, psc, sg, sb, sw, sbias, mg):
    B, S, _ = pc.shape
    gw = GROUP_WIDTH
    ts = SEQ_TILE
    hb = ts // HALO
    row = lambda b, i: (b, i, 0)
    prev = lambda b, i: (b, jnp.maximum(i * hb - 1, 0), 0)
    full2 = lambda b, i: (0, 0)
    full3 = lambda b, i: (0, 0, 0)
    out = jax.ShapeDtypeStruct((B, S, gw), bf16)
    return pl.pallas_call(
        _local_kernel,
        grid=(B, S // ts),
        in_specs=[pl.BlockSpec((1, ts, 2 * gw), row),
                  pl.BlockSpec((1, HALO, 2 * gw), prev),
                  pl.BlockSpec((1, ts, gw), row),
                  pl.BlockSpec((1, HALO, gw), prev),
                  pl.BlockSpec((1, ts, 2 * gw), row),
                  pl.BlockSpec(cw.shape, full2), pl.BlockSpec(cb.shape, full2),
                  pl.BlockSpec(cng.shape, full2), pl.BlockSpec(cnb.shape, full2),
                  pl.BlockSpec(pw_bd.shape, full2), pl.BlockSpec(psc.shape, full2),
                  pl.BlockSpec(sg.shape, full2), pl.BlockSpec(sb.shape, full2),
                  pl.BlockSpec(sw.shape, full3), pl.BlockSpec(sbias.shape, full2),
                  pl.BlockSpec(mg.shape, full2)],
        out_specs=[pl.BlockSpec((1, ts, gw), row)] * 3,
        out_shape=[out, out, out],
        scratch_shapes=[pltpu.VMEM((ts + HALO, gw), f32), pltpu.VMEM((SUBLANES - 1, ts + HALO, gw), f32),
                        pltpu.VMEM((ts + HALO, gw), f32),
                        pltpu.VMEM((ts + 16, gw), f32), pltpu.VMEM((ts + 16, gw), f32),
                        pltpu.VMEM((ts + 16, gw), f32)],
        compiler_params=_cparams(("arbitrary", "arbitrary")),
        name="local_mixers",
    )(pc, pc, pp, pp, ps, cw, cb, cng, cnb, pw_bd, psc, sg, sb, sw, sbias, mg)


def _t5_bucket_table(max_dist):
    d = np.arange(max_dist, dtype=np.int64)
    max_exact = N_BUCKETS // 2
    nf = np.maximum(d, 1).astype(np.float32)
    large = max_exact + (np.log(nf / np.float32(max_exact)) / np.float32(math.log(T5_MAX_DISTANCE / max_exact))
                         * np.float32(N_BUCKETS - max_exact)).astype(np.int32)
    large = np.minimum(large, N_BUCKETS - 1)
    return np.where(d < max_exact, d, large).astype(np.int32)


_TILE_BASES = (0, MOBA_BLOCK)


def _bias_kernel(tab_ref, o_ref):
    blk, qc = MOBA_BLOCK, MOBA_BLOCK
    table = _t5_bucket_table(2 * blk + qc)
    first = [int(np.argmax(table >= b)) for b in range(N_BUCKETS)]
    j = lax.broadcasted_iota(i32, (blk, qc), 0)
    q = lax.broadcasted_iota(i32, (blk, qc), 1)
    for t, base in enumerate(_TILE_BASES):
        d = base + q - j
        lo, hi = max(base - (blk - 1), 0), base + qc - 1
        for h in range(HEADS):
            val = jnp.full((blk, qc), tab_ref[h], f32)
            for b in range(1, N_BUCKETS):
                if first[b] > hi:
                    continue
                if first[b] <= lo:
                    val = jnp.full((blk, qc), tab_ref[b * HEADS + h], f32)
                else:
                    val = jnp.where(d >= first[b], tab_ref[b * HEADS + h], val)
            o_ref[t, h] = jnp.where(d >= 0, val * LOG2E, NEG)


def _bias_tiles(rel_bias):
    return pl.pallas_call(
        _bias_kernel,
        in_specs=[pl.BlockSpec(memory_space=pltpu.SMEM)],
        out_shape=jax.ShapeDtypeStruct((len(_TILE_BASES), HEADS, MOBA_BLOCK, MOBA_BLOCK), f32),
        name="bias_tiles",
    )(rel_bias.astype(f32).reshape(-1))


def _attn_kernel(far_ref, qT_ref, k_ref, vT_ref, km_ref, bias_ref, mg_ref, o_ref, mask_ref, sa_ref, sb_ref):
    own = pl.program_id(1)
    blk, gw, hd = MOBA_BLOCK, GROUP_WIDTH, HEAD_DIM
    nb = km_ref.shape[1]

    nio = lax.broadcasted_iota(i32, (nb, blk), 0)
    past = nio < own
    km = km_ref[0]
    q_heads = []
    for h in range(HEADS):
        qh = qT_ref[0, h * hd:(h + 1) * hd, :]
        q_heads.append(qh)
        km_hi, km_lo = _split_bf16(km[:, h * hd:(h + 1) * hd])
        gate = (jnp.dot(km_hi, qh, preferred_element_type=f32)
                + jnp.dot(km_lo, qh, preferred_element_type=f32))
        gate = jnp.where(past, gate, -jnp.inf)
        rank = jnp.zeros((nb, blk), i32)
        for m in range(nb):
            gm = gate[m:m + 1, :]
            beats = (gm > gate) | ((gm == gate) & (m < nio))
            rank = rank + beats.astype(i32)
        sel = (rank < MOBA_TOPK) & past
        mask_ref[0, h] = jnp.where(sel, 0.0, NEG)
        mask_ref[1, h] = jnp.where(sel & (nio < own - 1), far_ref[h] * LOG2E, NEG)

    def qk(n, h):
        kb = k_ref[0, h, pl.ds(pl.multiple_of(n * blk, blk), blk), :]
        return jnp.dot(kb, q_heads[h], preferred_element_type=f32)

    def far_scores(n, h):
        return qk(n, h) + mask_ref[1, h, pl.ds(n, 1), :]

    lane_groups = blk // LANES

    def update(state, scores, n):
        out = []
        for h in range(HEADS):
            vb = vT_ref[0, n, h]
            for c in range(lane_groups):
                u = h * lane_groups + c
                m, acc = state[2 * u:2 * u + 2]
                s = scores[h][:, c * LANES:(c + 1) * LANES]
                m_new = jnp.maximum(m, jnp.max(s, axis=0, keepdims=True))
                alpha = jnp.exp2(m - m_new)
                p = jnp.exp2(s - m_new)
                acc = acc * alpha + jnp.dot(vb, p.astype(bf16), preferred_element_type=f32)
                out += [m_new, acc]
        return tuple(out)

    adj = jnp.maximum(own - 1, 0)
    n_far = jnp.maximum(own - 1, 0)
    s_own = [qk(own, h) + bias_ref[0, h] for h in range(HEADS)]
    s_adj = [qk(adj, h) + bias_ref[1, h] + mask_ref[0, h, pl.ds(adj, 1), :] for h in range(HEADS)]
    for h in range(HEADS):
        sa_ref[h] = far_scores(0, h)

    state = []
    for h in range(HEADS):
        vb = jnp.concatenate([vT_ref[0, own, h], vT_ref[0, adj, h]], axis=1)
        for c in range(lane_groups):
            so = s_own[h][:, c * LANES:(c + 1) * LANES]
            sj = s_adj[h][:, c * LANES:(c + 1) * LANES]
            m0 = jnp.maximum(jnp.max(so, axis=0, keepdims=True), jnp.max(sj, axis=0, keepdims=True))
            p = jnp.concatenate([jnp.exp2(so - m0).astype(bf16), jnp.exp2(sj - m0).astype(bf16)], axis=0)
            state += [m0, jnp.dot(vb, p, preferred_element_type=f32)]
    state = tuple(state)

    def body(i, state):
        first = jnp.minimum(2 * i, nb - 1)
        second = jnp.minimum(2 * i + 1, nb - 1)
        third = jnp.minimum(2 * i + 2, nb - 1)
        for h in range(HEADS):
            sb_ref[h] = far_scores(second, h)
        state = update(state, [sa_ref[h] for h in range(HEADS)], first)
        for h in range(HEADS):
            sa_ref[h] = far_scores(third, h)
        return update(state, [sb_ref[h] for h in range(HEADS)], second)

    fin = lax.fori_loop(0, (n_far + 1) // 2, body, state)

    def normalised(acc):
        return acc[0:hd, :] / acc[hd:hd + 1, :]

    outT = jnp.concatenate(
        [jnp.concatenate([normalised(fin[2 * (h * lane_groups + c) + 1]) for c in range(lane_groups)], axis=1)
         for h in range(HEADS)], axis=0)
    o_ref[0] = _group_rms(outT.T, mg_ref[:, gw:2 * gw]).astype(bf16)


def _moba_attention(qT, k, vT, kmean, bias_tiles, far_bias, mg):
    B, _, S, _ = k.shape
    gw, blk = GROUP_WIDTH, MOBA_BLOCK
    nb = kmean.shape[1]
    return pl.pallas_call(
        _attn_kernel,
        grid=(B, nb),
        in_specs=[pl.BlockSpec(memory_space=pltpu.SMEM),
                  pl.BlockSpec((1, gw, blk), lambda b, c: (b, 0, c)),
                  pl.BlockSpec((1, HEADS, S, HEAD_DIM), lambda b, c: (b, 0, 0, 0)),
                  pl.BlockSpec((1, nb, HEADS, V_ROWS, blk), lambda b, c: (b, 0, 0, 0, 0)),
                  pl.BlockSpec((1, nb, gw), lambda b, c: (b, 0, 0)),
                  pl.BlockSpec(bias_tiles.shape, lambda b, c: (0, 0, 0, 0)),
                  pl.BlockSpec(mg.shape, lambda b, c: (0, 0))],
        out_specs=pl.BlockSpec((1, blk, gw), lambda b, c: (b, c, 0)),
        out_shape=jax.ShapeDtypeStruct((B, S, gw), bf16),
        scratch_shapes=[pltpu.VMEM((2, HEADS, nb, blk), f32),
                        pltpu.VMEM((HEADS, blk, blk), f32), pltpu.VMEM((HEADS, blk, blk), f32)],
        compiler_params=_cparams(("arbitrary", "arbitrary")),
        name="moba_attention",
    )(far_bias, qT, k, vT, kmean, bias_tiles, mg)


def _outproj_router_kernel(yc_ref, ya_ref, yp_ref, ys_ref, x_ref, g1_ref, wo_ref, n2_ref, sh_ref, sc_ref,
                           rw_ref, rb_ref,
                           x1_ref, h2_ref, dest_ref, gate_ref, pad_ref):
    W = ROUTE_TILE
    windows = range(ROUTE_WINDOWS_PER_STEP)
    rows = [slice(u * W, (u + 1) * W) for u in windows]
    mixed = [jnp.concatenate([r[rows[u], :] for r in (yc_ref, ya_ref, yp_ref, ys_ref)], axis=1) for u in windows]
    projected = [jnp.dot(m, wo_ref[...], preferred_element_type=f32) for m in mixed]

    his, los = [], []
    for u in windows:
        x1 = x_ref[rows[u], :] + g1_ref[0] * projected[u]
        x1_ref[rows[u], :] = x1
        ms = jnp.mean(x1 * x1, axis=-1, keepdims=True)
        h = x1 * lax.rsqrt(ms + EPS) * n2_ref[...]
        h = h * (1.0 + sc_ref[0]) + sh_ref[0]
        h_hi, h_lo = _split_bf16(h)
        h2_ref[rows[u], :] = h_hi
        his.append(h_hi)
        los.append(h_lo)

    nt = (((1,), (1,)), ((), ()))
    rw_hi, rw_lo = _split_bf16(rw_ref[...])
    logits = [(lax.dot_general(rw_hi, his[u], nt, preferred_element_type=f32)
               + lax.dot_general(rw_hi, los[u], nt, preferred_element_type=f32)
               + lax.dot_general(rw_lo, his[u], nt, preferred_element_type=f32)) + rb_ref[...] for u in windows]

    eio = lax.broadcasted_iota(i32, (N_EXPERTS, W), 0)
    sels, multis = [], []
    for u in windows:
        work = logits[u]
        vals, sel_u = [], []
        for k in range(TOP_K):
            m = jnp.max(work, axis=0, keepdims=True)
            idx = jnp.min(jnp.where(work == m, eio, N_EXPERTS), axis=0, keepdims=True)
            sel = eio == idx
            vals.append(m)
            sel_u.append(sel)
            work = jnp.where(sel, -jnp.inf, work)
        exps = [jnp.exp(v - vals[0]) for v in vals]
        denom = exps[0] + exps[1] + exps[2] + exps[3]
        for k in range(TOP_K):
            gate_ref[k:k + 1, rows[u]] = exps[k] / denom
        multi = jnp.zeros((N_EXPERTS, W), f32)
        for sel in sel_u:
            multi = multi + sel.astype(f32)
        sels.append(sel_u)
        multis.append(multi)

    before = (lax.broadcasted_iota(i32, (W, W), 0) < lax.broadcasted_iota(i32, (W, W), 1)).astype(bf16)
    earlier = [jnp.dot(multis[u].astype(bf16), before, preferred_element_type=f32) for u in windows]

    lower = (lax.broadcasted_iota(i32, (N_EXPERTS, N_EXPERTS), 1)
             < lax.broadcasted_iota(i32, (N_EXPERTS, N_EXPERTS), 0)).astype(bf16)
    seg_start = []
    for u in windows:
        cnt = jnp.sum(multis[u], axis=1, keepdims=True).astype(i32)
        padded = (cnt + (ROW_CHUNK - 1)) // ROW_CHUNK * ROW_CHUNK
        pad_ref[u] = padded
        seg_start.append(jnp.dot(lower, jnp.broadcast_to(padded.astype(f32), (N_EXPERTS, W)).astype(bf16),
                                 preferred_element_type=f32))
    for u in windows:
        row = seg_start[u] + earlier[u]
        for k in range(TOP_K):
            dest_ref[k:k + 1, rows[u]] = jnp.sum(jnp.where(sels[u][k], row, 0.0), axis=0,
                                                 keepdims=True).astype(i32)


def _outproj_router(yc, ya, yp, ys, x, gate1, wo_bf, n2g, shift2, scale2, rwT, rb, seq):
    N, D = x.shape
    gw = GROUP_WIDTH
    nw = N // ROUTE_TILE
    wps = ROUTE_WINDOWS_PER_STEP
    W = ROUTE_TILE * wps
    per_b = seq // W
    row = lambda i: (i, 0)
    vec = lambda i: (i // per_b, 0, 0)
    full = lambda i: (0, 0)
    colblk = lambda i: (0, i)
    return pl.pallas_call(
        _outproj_router_kernel,
        grid=(nw // wps,),
        in_specs=[pl.BlockSpec((W, gw), row)] * 4 + [
            pl.BlockSpec((W, D), row),
            pl.BlockSpec((1, 1, D), vec),
            pl.BlockSpec(wo_bf.shape, full),
            pl.BlockSpec((1, D), full),
            pl.BlockSpec((1, 1, D), vec),
            pl.BlockSpec((1, 1, D), vec),
            pl.BlockSpec(rwT.shape, full),
            pl.BlockSpec(rb.shape, full)],
        out_specs=[pl.BlockSpec((W, D), row),
                   pl.BlockSpec((W, D), row),
                   pl.BlockSpec((TOP_K, W), colblk),
                   pl.BlockSpec((TOP_K, W), colblk),
                   pl.BlockSpec((wps, N_EXPERTS, 1), lambda i: (i, 0, 0))],
        out_shape=[jax.ShapeDtypeStruct((N, D), f32),
                   jax.ShapeDtypeStruct((N, D), bf16),
                   jax.ShapeDtypeStruct((TOP_K, N), i32),
                   jax.ShapeDtypeStruct((TOP_K, N), f32),
                   jax.ShapeDtypeStruct((nw, N_EXPERTS, 1), i32)],
        compiler_params=_cparams(("arbitrary",)),
        name="outproj_router",
    )(yc, ya, yp, ys, x, gate1, wo_bf, n2g, shift2, scale2, rwT, rb)


def _max_window_rows():
    return -(-(ROUTE_TILE * TOP_K + N_EXPERTS * (ROW_CHUNK - 1)) // 128) * 128


def _pack_pairs(lo, hi):
    lo_bits = lax.bitcast_convert_type(lo, jnp.uint32)
    hi_bits = lax.bitcast_convert_type(hi, jnp.uint32)
    return lax.bitcast_convert_type(hi_bits | (lo_bits >> 16), i32)


def _unpack_pairs(words):
    bits = lax.bitcast_convert_type(words, jnp.uint32)
    lo = lax.bitcast_convert_type(bits << 16, f32).astype(bf16)
    hi = lax.bitcast_convert_type(bits & jnp.uint32(0xFFFF0000), f32).astype(bf16)
    return lo, hi


def _dispatch_kernel(lstart_s, ncopy_s, gbase_s, wcopies_s, tail_start_s, tail_chunks_s, n_used_s,
                     h_ref, dest_ref, xs_ref, sorted_ref, zero_ref, sem):
    w = pl.program_id(0)
    nw = pl.num_programs(0)
    W = ROUTE_TILE
    R = sorted_ref.shape[1]
    half = sorted_ref.shape[2]
    slot = w % 2
    rio = lax.broadcasted_iota(i32, (R, W), 0)
    hit = rio == dest_ref[0:1, :]
    for k in range(1, TOP_K):
        hit = hit | (rio == dest_ref[k:k + 1, :])
    onehot = jnp.where(hit, 1.0, 0.0).astype(bf16)
    rows = jnp.dot(onehot, h_ref[...], preferred_element_type=f32)
    sorted_ref[slot] = _pack_pairs(rows[:, :half], rows[:, half:])

    def piece_copy(s, src_row, dst_row):
        return pltpu.make_async_copy(
            sorted_ref.at[s, pl.ds(pl.multiple_of(src_row, ROW_CHUNK), DISPATCH_COPY_ROWS), :],
            xs_ref.at[pl.ds(pl.multiple_of(dst_row, ROW_CHUNK), DISPATCH_COPY_ROWS), :], sem.at[s])

    def zero_copy(dst_row):
        return pltpu.make_async_copy(
            zero_ref.at[0:ROW_CHUNK, :],
            xs_ref.at[pl.ds(pl.multiple_of(dst_row, ROW_CHUNK), ROW_CHUNK), :], sem.at[0])

    def zero_tile_copy(tile):
        return pltpu.make_async_copy(
            zero_ref, xs_ref.at[pl.ds(pl.multiple_of(tile * FFN_TILE, FFN_TILE), FFN_TILE), :], sem.at[0])

    def drain(s, count):
        @pl.when(count > 0)
        def _():
            rows = pl.multiple_of(count * DISPATCH_COPY_ROWS, DISPATCH_COPY_ROWS)
            pltpu.make_async_copy(xs_ref.at[pl.ds(0, rows), :], xs_ref.at[pl.ds(0, rows), :], sem.at[s]).wait()

    @pl.when(w > 0)
    def _():
        drain(1 - slot, wcopies_s[jnp.maximum(w - 1, 0)])

    def per_expert(e, c):
        n = ncopy_s[w * N_EXPERTS + e]
        src = lstart_s[w * N_EXPERTS + e]
        dst = gbase_s[w * N_EXPERTS + e]

        def issue(j, c):
            piece_copy(slot, src + j * DISPATCH_COPY_ROWS, dst + j * DISPATCH_COPY_ROWS).start()
            return c
        lax.fori_loop(0, n, issue, 0)
        return c

    lax.fori_loop(0, N_EXPERTS, per_expert, 0)

    @pl.when(w == nw - 1)
    def _():
        drain(slot, wcopies_s[w])
        zero_ref[...] = jnp.zeros(zero_ref.shape, i32)

        def per_tail(e, total):
            n = tail_chunks_s[e]
            dst = tail_start_s[e]

            def issue(j, c):
                zero_copy(dst + j * ROW_CHUNK).start()
                return c
            lax.fori_loop(0, n, issue, 0)
            return total + n
        tails = lax.fori_loop(0, N_EXPERTS, per_tail, 0)

        def drain_tail(j, c):
            zero_copy(0).wait()
            return c
        lax.fori_loop(0, tails, drain_tail, 0)

        n_tiles = xs_ref.shape[0] // FFN_TILE

        def issue_tile(j, c):
            zero_tile_copy(j).start()
            return c
        lax.fori_loop(n_used_s[0], n_tiles, issue_tile, 0)

        def drain_tile(j, c):
            zero_tile_copy(0).wait()
            return c
        lax.fori_loop(n_used_s[0], n_tiles, drain_tile, 0)


def _dispatch(h2, destT, t, p_rows):
    N, D = h2.shape
    W = ROUTE_TILE
    nw = N // W
    R = _max_window_rows()
    assert R >= W * TOP_K + N_EXPERTS * (ROW_CHUNK - 1) + DISPATCH_COPY_ROWS - ROW_CHUNK
    grid_spec = pltpu.PrefetchScalarGridSpec(
        num_scalar_prefetch=7,
        grid=(nw,),
        in_specs=[pl.BlockSpec((W, D), lambda w, *_: (w, 0)),
                  pl.BlockSpec((TOP_K, W), lambda w, *_: (0, w))],
        out_specs=pl.BlockSpec(memory_space=pl.ANY),
        scratch_shapes=[pltpu.VMEM((2, R, D // 2), i32), pltpu.VMEM((FFN_TILE, D // 2), i32),
                        pltpu.SemaphoreType.DMA((2,))],
    )
    return pl.pallas_call(
        _dispatch_kernel,
        grid_spec=grid_spec,
        out_shape=jax.ShapeDtypeStruct((p_rows, D // 2), i32),
        compiler_params=_cparams(("arbitrary",)),
        name="expert_dispatch",
    )(t['lstart'], t['ncopy'], t['gbase'], t['wcopies'], t['tail_start'], t['tail_chunks'], t['n_used'],
      h2, destT)


def _ffn_kernel(tile_expert_s, next_expert_s, n_used_s, x_ref, w1_hbm, b1_ref, w2_hbm, b2_ref, y_ref,
                w1f_ref, w2f_ref, w1b_ref, w2b_ref, sem, *, layer):
    i = pl.program_id(0)
    last = n_used_s[0] - 1
    expert = tile_expert_s[jnp.minimum(i, last)]
    prev_expert = tile_expert_s[jnp.minimum(jnp.maximum(i - 1, 0), last)]

    def weight_copies(e):
        return (pltpu.make_async_copy(w1_hbm.at[layer, e], w1f_ref, sem.at[0]),
                pltpu.make_async_copy(w2_hbm.at[layer, e], w2f_ref, sem.at[1]))

    @pl.when(i == 0)
    def _():
        for c in weight_copies(expert):
            c.start()

    @pl.when((i == 0) | (expert != prev_expert))
    def _():
        for c in weight_copies(expert):
            c.wait()
        w1b_ref[...] = w1f_ref[...].astype(bf16)
        w2b_ref[...] = w2f_ref[...].astype(bf16)
        nxt = next_expert_s[expert]

        @pl.when(nxt < N_EXPERTS)
        def _():
            for c in weight_copies(nxt):
                c.start()

    @pl.when(i < n_used_s[0])
    def _():
        dff = w2b_ref.shape[0]
        half = x_ref.shape[1]
        sub = x_ref.shape[0] // FFN_SUBTILES
        hidden = []
        for r in range(FFN_SUBTILES):
            x_lo, x_hi = _unpack_pairs(x_ref[r * sub:(r + 1) * sub, :])
            hidden.append(jnp.dot(x_lo, w1b_ref[0:half, :], preferred_element_type=f32)
                          + jnp.dot(x_hi, w1b_ref[half:2 * half, :], preferred_element_type=f32))
        for r in range(FFN_SUBTILES):
            hh = hidden[r] + b1_ref[0, 0]
            x_glu = jnp.minimum(hh[:, :dff], SWIGLU_LIMIT)
            x_lin = jnp.clip(hh[:, dff:], -SWIGLU_LIMIT, SWIGLU_LIMIT)
            act = x_glu * jax.nn.sigmoid(SWIGLU_ALPHA * x_glu) * (x_lin + 1.0)
            y = jnp.dot(act.astype(bf16), w2b_ref[...], preferred_element_type=f32) + b2_ref[0, 0]
            y = y.astype(bf16).astype(f32)
            y_ref[r * sub:(r + 1) * sub, :] = _pack_pairs(y[:, :half], y[:, half:])

    @pl.when(i >= n_used_s[0])
    def _():
        y_ref[...] = jnp.zeros(y_ref.shape, i32)


def _expert_ffn(xs, w1, b1, w2, b2, t, layer):
    P, half = xs.shape
    L, E, D, F2 = w1.shape
    tm = FFN_TILE
    nt = P // tm

    def tile(i, te, ne, nu):
        return (jnp.minimum(i, nu[0] - 1), 0)

    def expert4(i, te, ne, nu):
        return (layer, te[jnp.minimum(i, nu[0] - 1)], 0, 0)

    grid_spec = pltpu.PrefetchScalarGridSpec(
        num_scalar_prefetch=3,
        grid=(nt,),
        in_specs=[pl.BlockSpec((tm, half), tile),
                  pl.BlockSpec(memory_space=pl.ANY),
                  pl.BlockSpec((1, 1, 1, F2), expert4),
                  pl.BlockSpec(memory_space=pl.ANY),
                  pl.BlockSpec((1, 1, 1, D), expert4)],
        out_specs=pl.BlockSpec((tm, half), lambda i, te, ne, nu: (i, 0)),
        scratch_shapes=[pltpu.VMEM((D, F2), f32), pltpu.VMEM((F2 // 2, D), f32),
                        pltpu.VMEM((D, F2), bf16), pltpu.VMEM((F2 // 2, D), bf16),
                        pltpu.SemaphoreType.DMA((2,))],
    )
    return pl.pallas_call(
        functools.partial(_ffn_kernel, layer=layer),
        grid_spec=grid_spec,
        out_shape=jax.ShapeDtypeStruct((P, half), i32),
        compiler_params=_cparams(("arbitrary",)),
        name="expert_ffn",
    )(t['tile_expert'], t['next_expert'], t['n_used'], xs, w1, b1.reshape(L, E, 1, F2), w2,
      b2.reshape(L, E, 1, D))


def _combine_kernel(chunk_src_s, wchunks_s,
                    ys_ref, dest_ref, gate_ref, x1_ref, g2_ref, fg_ref, o_ref, local_ref, sem, *, final):
    w = pl.program_id(0)
    nw = pl.num_programs(0)
    W = ROUTE_TILE
    R = local_ref.shape[1]
    slot = w % 2
    max_chunks = R // ROW_CHUNK

    def chunk_copy(s, src_row, dst_row):
        return pltpu.make_async_copy(
            ys_ref.at[pl.ds(pl.multiple_of(src_row, ROW_CHUNK), ROW_CHUNK), :],
            local_ref.at[s, pl.ds(pl.multiple_of(dst_row, ROW_CHUNK), ROW_CHUNK), :], sem.at[s])

    def fetch(win, s):
        def issue(c, carry):
            chunk_copy(s, chunk_src_s[win * max_chunks + c], c * ROW_CHUNK).start()
            return carry
        lax.fori_loop(0, wchunks_s[win], issue, 0)

    @pl.when(w == 0)
    def _():
        local_ref[...] = jnp.zeros(local_ref.shape, i32)
        fetch(0, 0)

    @pl.when(w + 1 < nw)
    def _():
        fetch(jnp.minimum(w + 1, nw - 1), 1 - slot)

    @pl.when(wchunks_s[w] > 0)
    def _():
        rows = pl.multiple_of(wchunks_s[w] * ROW_CHUNK, ROW_CHUNK)
        pltpu.make_async_copy(ys_ref.at[pl.ds(0, rows), :], local_ref.at[slot, pl.ds(0, rows), :],
                              sem.at[slot]).wait()

    rio = lax.broadcasted_iota(i32, (R, W), 0)
    weights = jnp.zeros((R, W), f32)
    for k in range(TOP_K):
        weights = jnp.where(rio == dest_ref[k:k + 1, :], gate_ref[k:k + 1, :], weights)
    weights = weights.astype(bf16)
    y_lo, y_hi = _unpack_pairs(local_ref[slot])
    tn = (((0,), (0,)), ((), ()))
    moe = jnp.concatenate([lax.dot_general(weights, y_lo, tn, preferred_element_type=f32),
                           lax.dot_general(weights, y_hi, tn, preferred_element_type=f32)], axis=1)
    x2 = x1_ref[...] + g2_ref[0] * moe
    if final:
        ms = jnp.mean(x2 * x2, axis=-1, keepdims=True)
        x2 = x2 * lax.rsqrt(ms + EPS) * fg_ref[...]
    o_ref[...] = x2


def _combine(ys, dest, gates, x1, gate2, final_g, t, seq, final):
    N, D = x1.shape
    W = ROUTE_TILE
    nw = N // W
    per_b = seq // W
    R = _max_window_rows()
    grid_spec = pltpu.PrefetchScalarGridSpec(
        num_scalar_prefetch=2,
        grid=(nw,),
        in_specs=[pl.BlockSpec(memory_space=pl.ANY),
                  pl.BlockSpec((TOP_K, W), lambda w, *_: (0, w)),
                  pl.BlockSpec((TOP_K, W), lambda w, *_: (0, w)),
                  pl.BlockSpec((W, D), lambda w, *_: (w, 0)),
                  pl.BlockSpec((1, 1, D), lambda w, *_: (w // per_b, 0, 0)),
                  pl.BlockSpec((1, D), lambda w, *_: (0, 0))],
        out_specs=pl.BlockSpec((W, D), lambda w, *_: (w, 0)),
        scratch_shapes=[pltpu.VMEM((2, R, D // 2), i32), pltpu.SemaphoreType.DMA((2,))],
    )
    return pl.pallas_call(
        functools.partial(_combine_kernel, final=final),
        grid_spec=grid_spec,
        out_shape=jax.ShapeDtypeStruct((N, D), f32),
        compiler_params=_cparams(("arbitrary",)),
        name="expert_combine",
    )(t['chunk_src'], t['wchunks'], ys, dest, gates, x1, gate2, final_g)


def _routing_tables(padded, n_tiles):
    nw, E = padded.shape
    lstart = jnp.cumsum(padded, axis=1) - padded
    tot = jnp.sum(padded, axis=0)
    slack = DISPATCH_COPY_ROWS - ROW_CHUNK
    region = jnp.where(tot > 0, (tot + slack + FFN_TILE - 1) // FFN_TILE * FFN_TILE, 0)
    region_end = jnp.cumsum(region)
    region_start = region_end - region
    gbase = region_start[None, :] + jnp.cumsum(padded, axis=0) - padded
    nchunk = padded // ROW_CHUNK
    tail_start = region_start + tot
    tail_chunks = (region - tot) // ROW_CHUNK
    n_used = (region_end[-1] // FFN_TILE).astype(i32).reshape(1)
    tile_row = jnp.arange(n_tiles, dtype=i32) * FFN_TILE
    tile_expert = jnp.minimum(
        jnp.sum((region_end[None, :] <= tile_row[:, None]).astype(i32), axis=1), E - 1).astype(i32)
    eidx = jnp.arange(E, dtype=i32)
    later_nonempty = (eidx[None, :] > eidx[:, None]) & (region[None, :] > 0)
    next_expert = jnp.min(jnp.where(later_nonempty, eidx[None, :], E), axis=1)
    flat = lambda a: a.reshape(-1).astype(i32)
    ncopy = (padded + DISPATCH_COPY_ROWS - 1) // DISPATCH_COPY_ROWS
    chunk_row = jnp.arange(_max_window_rows() // ROW_CHUNK, dtype=i32) * ROW_CHUNK
    owner = jnp.sum(((lstart + padded)[:, None, :] <= chunk_row[None, :, None]).astype(i32), axis=2)
    shift = jnp.sum(jnp.where(owner[:, :, None] == eidx[None, None, :], (gbase - lstart)[:, None, :], 0), axis=2)
    chunk_src = chunk_row[None, :] + shift
    return dict(lstart=flat(lstart), chunk_src=flat(chunk_src), gbase=flat(gbase), next_expert=flat(next_expert),
                wchunks=flat(jnp.sum(nchunk, axis=1)), ncopy=flat(ncopy), wcopies=flat(jnp.sum(ncopy, axis=1)),
                tail_start=flat(tail_start), tail_chunks=flat(tail_chunks),
                n_used=n_used, tile_expert=tile_expert)


def _block_diag(w):
    g, a, b = w.shape
    out = jnp.zeros((g * a, g * b), w.dtype)
    for i in range(g):
        out = out.at[i * a:(i + 1) * a, i * b:(i + 1) * b].set(w[i])
    return out


def kernel(x, c, w_mod, b_mod, norm1_g, w_in, conv_w, conv_b, conv_norm_g, conv_norm_b, rel_bias, pool_w, pool_scale, sgu_norm_g, sgu_norm_b, sgu_w, sgu_b, mix_out_g, w_out, norm2_g, router_w, router_b, exp_w1, exp_b1, exp_w2, exp_b2, final_norm_g):
    B, S, D = x.shape
    L = w_mod.shape[0]
    N = B * S
    nw = N // ROUTE_TILE
    assert S % SEQ_TILE == 0 and S % MOBA_BLOCK == 0 and N % ROUTE_TILE == 0 and S % ROUTE_TILE == 0
    p_bound = (N * TOP_K + nw * N_EXPERTS * (ROW_CHUNK - 1)
               + N_EXPERTS * (DISPATCH_COPY_ROWS - ROW_CHUNK + FFN_TILE - 1))
    n_tiles = -(-p_bound // FFN_TILE)
    p_rows = n_tiles * FFN_TILE

    mod = _modulation(c, w_mod, b_mod)
    bias_tiles = _bias_tiles(rel_bias)
    far_bucket = int(_t5_bucket_table(MOBA_BLOCK + 2)[MOBA_BLOCK + 1])
    assert far_bucket == int(_t5_bucket_table(S + 1)[S])
    far_bias = rel_bias[far_bucket].astype(f32)
    row = lambda a: a.reshape(1, -1)
    for l in range(L):
        m6 = mod[l].reshape(B, 6, 1, D)
        shift1, scale1, gate1, shift2, scale2, gate2 = (m6[:, j] for j in range(6))
        mg = row(mix_out_g[l])
        pc, qT, k, vT, kmean, pp, ps = _in_projection(x, shift1, scale1, row(norm1_g[l]), w_in[l].astype(bf16))
        yc, yp, ys = _local_mixers(
            pc, pp, ps, conv_w[l], row(conv_b[l]), row(conv_norm_g[l]), row(conv_norm_b[l]),
            _block_diag(pool_w[l]).astype(bf16), row(pool_scale[l]),
            row(sgu_norm_g[l]), row(sgu_norm_b[l]), sgu_w[l],
            jnp.repeat(sgu_b[l].T, HEAD_DIM, axis=1), mg)
        ya = _moba_attention(qT, k, vT, kmean.reshape(B, -1, GROUP_WIDTH), bias_tiles, far_bias, mg)
        flat = lambda a: a.reshape(N, -1)
        x1, h2, destT, gateT, padded = _outproj_router(
            flat(yc), flat(ya), flat(yp), flat(ys), x.reshape(N, D), gate1, w_out[l].astype(bf16),
            row(norm2_g[l]), shift2, scale2, router_w[l].T, router_b[l].reshape(-1, 1), S)
        t = _routing_tables(padded.reshape(nw, N_EXPERTS), n_tiles)
        xs = _dispatch(h2, destT, t, p_rows)
        ysort = _expert_ffn(xs, exp_w1, exp_b1, exp_w2, exp_b2, t, l)
        x = _combine(ysort, destT, gateT, x1, gate2, row(final_norm_g), t, S,
                     final=(l == L - 1)).reshape(B, S, D)
    return x
```

```python
import functools
import math

import numpy as np
import jax
import jax.numpy as jnp
from jax import lax
from jax.experimental import pallas as pl
from jax.experimental.pallas import tpu as pltpu

f32, bf16, i32 = jnp.float32, jnp.bfloat16, jnp.int32

GROUP_WIDTH = 256
HEADS = 4
HEAD_DIM = 64
V_ROWS = HEAD_DIM + 16
LOG2E = math.log2(math.e)
CONV_WIDTH = 31
MOBA_BLOCK = 256
MOBA_TOPK = 3
Q_CHUNK = 128
N_BUCKETS = 32
T5_MAX_DISTANCE = 128
POOL_WINDOWS = (2, 4, 8, 16)
SGU_CHUNK = 128
N_EXPERTS = 32
TOP_K = 4
SWIGLU_LIMIT = 7.0
SWIGLU_ALPHA = 1.702
EPS = 1e-6

SUBLANES = 8
LANES = 128
ATTN_QUERY_GROUP = 256
HALO = 32
SEQ_TILE = 512
ROUTE_TILE = 256
ROUTE_WINDOWS_PER_STEP = 4
ROW_CHUNK = 8
DISPATCH_COPY_ROWS = 32
COMBINE_COPY_ROWS = 32
MAX_DISPATCH_PIECES = ROUTE_TILE * TOP_K // DISPATCH_COPY_ROWS + N_EXPERTS
MAX_COMBINE_PIECES = (ROUTE_TILE * TOP_K + N_EXPERTS * (ROW_CHUNK - 1)) // COMBINE_COPY_ROWS
MAX_COMBINE_CHUNKS = N_EXPERTS * (COMBINE_COPY_ROWS // ROW_CHUNK - 1)
FFN_TILE = 512
FFN_SUBTILES = 2
NEG = -1e30
VMEM_LIMIT = 56 * 1024 * 1024


def _cparams(sem):
    return pltpu.CompilerParams(dimension_semantics=sem, vmem_limit_bytes=VMEM_LIMIT)


def _split_bf16(a):
    hi = a.astype(bf16)
    lo = (a - hi.astype(f32)).astype(bf16)
    return hi, lo


def _mod_kernel(c_ref, w_ref, b_ref, o_ref):
    c = c_ref[...]
    cond = c * jax.nn.sigmoid(c)
    o_ref[0] = jnp.dot(cond, w_ref[0], preferred_element_type=f32,
                       precision=lax.Precision.HIGHEST) + b_ref[0]


def _modulation(c, w_mod, b_mod):
    L, D, M = w_mod.shape
    B = c.shape[0]
    tn = 1536
    return pl.pallas_call(
        _mod_kernel,
        grid=(L, M // tn),
        in_specs=[pl.BlockSpec((B, D), lambda l, j: (0, 0)),
                  pl.BlockSpec((1, D, tn), lambda l, j: (l, 0, j)),
                  pl.BlockSpec((1, 1, tn), lambda l, j: (l, 0, j))],
        out_specs=pl.BlockSpec((1, B, tn), lambda l, j: (l, 0, j)),
        out_shape=jax.ShapeDtypeStruct((L, B, M), f32),
        compiler_params=_cparams(("arbitrary", "arbitrary")),
        name="modulation",
    )(c, w_mod, b_mod.reshape(L, 1, M))


def _inproj_kernel(x_ref, sh_ref, sc_ref, g_ref, w_ref,
                   pc_ref, qT_ref, k_ref, vT_ref, km_ref, pp_ref, ps_ref):
    x = x_ref[0]
    ms = jnp.mean(x * x, axis=-1, keepdims=True)
    h = x * lax.rsqrt(ms + EPS) * g_ref[...]
    h = h * (1.0 + sc_ref[0]) + sh_ref[0]
    proj = jnp.dot(h.astype(bf16), w_ref[...], preferred_element_type=f32)
    gw = GROUP_WIDTH
    pc_ref[0] = proj[:, 0:2 * gw]
    q = proj[:, 2 * gw:3 * gw] * (HEAD_DIM ** -0.5 * LOG2E)
    qT_ref[0] = q.T.astype(bf16)
    kk = proj[:, 3 * gw:4 * gw]
    for h in range(HEADS):
        k_ref[0, h] = kk[:, h * HEAD_DIM:(h + 1) * HEAD_DIM].astype(bf16)
    for j in range(SEQ_TILE // MOBA_BLOCK):
        km_ref[0, 0, j:j + 1, :] = jnp.mean(kk[j * MOBA_BLOCK:(j + 1) * MOBA_BLOCK], axis=0, keepdims=True)
    for j in range(SEQ_TILE // MOBA_BLOCK):
        vT = proj[j * MOBA_BLOCK:(j + 1) * MOBA_BLOCK, 4 * gw:5 * gw].T.astype(bf16)
        for h in range(HEADS):
            vT_ref[0, j, h] = jnp.concatenate(
                [vT[h * HEAD_DIM:(h + 1) * HEAD_DIM, :], jnp.ones((V_ROWS - HEAD_DIM, MOBA_BLOCK), bf16)], axis=0)
    pp_ref[0] = proj[:, 5 * gw:6 * gw]
    ps_ref[0] = proj[:, 6 * gw:8 * gw]


def _in_projection(x, shift, scale, g, w_bf):
    B, S, D = x.shape
    gw = GROUP_WIDTH
    ts = SEQ_TILE
    nt = S // ts
    row = lambda b, i: (b, i, 0)
    col = lambda b, i: (b, 0, i)
    vec = lambda b, i: (b, 0, 0)
    return pl.pallas_call(
        _inproj_kernel,
        grid=(B, nt),
        in_specs=[pl.BlockSpec((1, ts, D), row),
                  pl.BlockSpec((1, 1, D), vec),
                  pl.BlockSpec((1, 1, D), vec),
                  pl.BlockSpec((1, D), lambda b, i: (0, 0)),
                  pl.BlockSpec(w_bf.shape, lambda b, i: (0, 0))],
        out_specs=[pl.BlockSpec((1, ts, 2 * gw), row),
                   pl.BlockSpec((1, gw, ts), col),
                   pl.BlockSpec((1, HEADS, ts, HEAD_DIM), lambda b, i: (b, 0, i, 0)),
                   pl.BlockSpec((1, ts // MOBA_BLOCK, HEADS, V_ROWS, MOBA_BLOCK), lambda b, i: (b, i, 0, 0, 0)),
                   pl.BlockSpec((1, 1, ts // MOBA_BLOCK, gw), lambda b, i: (b, i, 0, 0)),
                   pl.BlockSpec((1, ts, gw), row),
                   pl.BlockSpec((1, ts, 2 * gw), row)],
        out_shape=[jax.ShapeDtypeStruct((B, S, 2 * gw), f32),
                   jax.ShapeDtypeStruct((B, gw, S), bf16),
                   jax.ShapeDtypeStruct((B, HEADS, S, HEAD_DIM), bf16),
                   jax.ShapeDtypeStruct((B, S // MOBA_BLOCK, HEADS, V_ROWS, MOBA_BLOCK), bf16),
                   jax.ShapeDtypeStruct((B, nt, ts // MOBA_BLOCK, gw), f32),
                   jax.ShapeDtypeStruct((B, S, gw), f32),
                   jax.ShapeDtypeStruct((B, S, 2 * gw), f32)],
        compiler_params=_cparams(("arbitrary", "arbitrary")),
        name="in_projection",
    )(x, shift, scale, g, w_bf)


def _group_rms(y, g):
    return y * lax.rsqrt(jnp.mean(y * y, axis=-1, keepdims=True) + EPS) * g


def _local_kernel(pc_ref, pcp_ref, pp_ref, ppp_ref, ps_ref,
                  cw_ref, cb_ref, cng_ref, cnb_ref, pw_ref, psc_ref,
                  sg_ref, sb_ref, sw_ref, sbias_ref, mg_ref,
                  yc_ref, yp_ref, ys_ref,
                  gext, gshift, zext, s2, s4, s8):
    i = pl.program_id(1)
    ts = SEQ_TILE
    gw = GROUP_WIDTH
    first = i == 0
    lane = lax.broadcasted_iota(i32, (1, gw), 1)

    pc = pc_ref[0]
    g = pc[:, :gw] * jax.nn.sigmoid(pc[:, gw:])
    ph = pcp_ref[0]
    gh = ph[:, :gw] * jax.nn.sigmoid(ph[:, gw:])
    gext[0:HALO, :] = jnp.where(first, 0.0, gh)
    gext[HALO:HALO + ts, :] = g
    span = ts + HALO - SUBLANES
    for a in range(1, SUBLANES):
        gshift[a - 1, 0:span, :] = gext[pl.ds(a, span), :]
    acc = jnp.zeros((ts, gw), f32)
    for j in range(CONV_WIDTH):
        b, a = divmod(HALO - (CONV_WIDTH - 1) + j, SUBLANES)
        tap = gext[pl.ds(SUBLANES * b, ts), :] if a == 0 else gshift[a - 1, pl.ds(SUBLANES * b, ts), :]
        acc = acc + cw_ref[j:j + 1, :] * tap
    y = acc + cb_ref[...]
    r = lax.broadcasted_iota(i32, (gw, gw), 0) // HEAD_DIM
    c = lax.broadcasted_iota(i32, (gw, gw), 1) // HEAD_DIM
    avg = jnp.where(r == c, 1.0 / HEAD_DIM, 0.0).astype(bf16)

    def head_mean(t):
        hi, lo = _split_bf16(t)
        return (jnp.dot(hi, avg, preferred_element_type=f32)
                + jnp.dot(lo, avg, preferred_element_type=f32))

    mu = head_mean(y)
    yc = y - mu
    var = head_mean(yc * yc)
    yn = yc * lax.rsqrt(var + EPS) * cng_ref[...] + cnb_ref[...]
    yconv = yn * jax.nn.sigmoid(yn)
    yc_ref[0] = _group_rms(yconv, mg_ref[:, 0:gw]).astype(bf16)

    z = pp_ref[0]
    zext[0:HALO, :] = jnp.where(first, 0.0, ppp_ref[0])
    zext[HALO:HALO + ts, :] = z
    n2, n4, n8 = ts + 14, ts + 12, ts + 8
    s2[0:n2, :] = zext[pl.ds(HALO - 14, n2), :] + zext[pl.ds(HALO - 15, n2), :]
    s4[0:n4, :] = s2[pl.ds(2, n4), :] + s2[pl.ds(0, n4), :]
    s8[0:n8, :] = s4[pl.ds(4, n8), :] + s4[pl.ds(0, n8), :]
    w2 = s2[pl.ds(14, ts), :]
    w4 = s4[pl.ds(12, ts), :]
    w8 = s8[pl.ds(8, ts), :]
    w16 = w8 + s8[pl.ds(0, ts), :]
    tpos = (i * ts + lax.broadcasted_iota(i32, (ts, 1), 0) + 1).astype(f32)
    grp = lane // (gw // len(POOL_WINDOWS))
    pooled = jnp.zeros((ts, gw), f32)
    for gi, (w, sw) in enumerate(zip(POOL_WINDOWS, (w2, w4, w8, w16))):
        pooled = jnp.where(grp == gi, sw / jnp.minimum(tpos, float(w)), pooled)
    pooled = pooled - z
    yp = jnp.dot(pooled.astype(bf16), pw_ref[...], preferred_element_type=f32) * psc_ref[...]
    yp_ref[0] = _group_rms(yp, mg_ref[:, 2 * gw:3 * gw]).astype(bf16)

    zz = ps_ref[0]
    zz = 0.5 * zz * (1.0 + lax.erf(zz * (1.0 / math.sqrt(2.0))))
    u = zz[:, :gw]
    v = zz[:, gw:]
    vm = jnp.mean(v, axis=-1, keepdims=True)
    vc = v - vm
    vv = jnp.mean(vc * vc, axis=-1, keepdims=True)
    vn = (vc * lax.rsqrt(vv + EPS) * sg_ref[...] + sb_ref[...]).astype(bf16)
    li = lax.broadcasted_iota(i32, (SGU_CHUNK, SGU_CHUNK), 0)
    lj = lax.broadcasted_iota(i32, (SGU_CHUNK, SGU_CHUNK), 1)
    head_of_lane = lane // HEAD_DIM
    wts = [jnp.where(li >= lj, sw_ref[h], 0.0).astype(bf16) for h in range(HEADS)]
    outs = []
    for n in range(ts // SGU_CHUNK):
        vch = vn[n * SGU_CHUNK:(n + 1) * SGU_CHUNK]
        mixed = sbias_ref[...]
        for h in range(HEADS):
            mh = jnp.dot(wts[h], vch, preferred_element_type=f32)
            mixed = mixed + jnp.where(head_of_lane == h, mh, 0.0)
        outs.append(u[n * SGU_CHUNK:(n + 1) * SGU_CHUNK] * mixed)
    ysgu = jnp.concatenate(outs, axis=0)
    ys_ref[0] = _group_rms(ysgu, mg_ref[:, 3 * gw:4 * gw]).astype(bf16)


def _local_mixers(pc, pp, ps, cw, cb, cng, cnb, pw_bd, psc, sg, sb, sw, sbias, mg):
    B, S, _ = pc.shape
    gw = GROUP_WIDTH
    ts = SEQ_TILE
    hb = ts // HALO
    row = lambda b, i: (b, i, 0)
    prev = lambda b, i: (b, jnp.maximum(i * hb - 1, 0), 0)
    full2 = lambda b, i: (0, 0)
    full3 = lambda b, i: (0, 0, 0)
    out = jax.ShapeDtypeStruct((B, S, gw), bf16)
    return pl.pallas_call(
        _local_kernel,
        grid=(B, S // ts),
        in_specs=[pl.BlockSpec((1, ts, 2 * gw), row),
                  pl.BlockSpec((1, HALO, 2 * gw), prev),
                  pl.BlockSpec((1, ts, gw), row),
                  pl.BlockSpec((1, HALO, gw), prev),
                  pl.BlockSpec((1, ts, 2 * gw), row),
                  pl.BlockSpec(cw.shape, full2), pl.BlockSpec(cb.shape, full2),
                  pl.BlockSpec(cng.shape, full2), pl.BlockSpec(cnb.shape, full2),
                  pl.BlockSpec(pw_bd.shape, full2), pl.BlockSpec(psc.shape, full2),
                  pl.BlockSpec(sg.shape, full2), pl.BlockSpec(sb.shape, full2),
                  pl.BlockSpec(sw.shape, full3), pl.BlockSpec(sbias.shape, full2),
                  pl.BlockSpec(mg.shape, full2)],
        out_specs=[pl.BlockSpec((1, ts, gw), row)] * 3,
        out_shape=[out, out, out],
        scratch_shapes=[pltpu.VMEM((ts + HALO, gw), f32), pltpu.VMEM((SUBLANES - 1, ts + HALO, gw), f32),
                        pltpu.VMEM((ts + HALO, gw), f32),
                        pltpu.VMEM((ts + 16, gw), f32), pltpu.VMEM((ts + 16, gw), f32),
                        pltpu.VMEM((ts + 16, gw), f32)],
        compiler_params=_cparams(("arbitrary", "arbitrary")),
        name="local_mixers",
    )(pc, pc, pp, pp, ps, cw, cb, cng, cnb, pw_bd, psc, sg, sb, sw, sbias, mg)


def _t5_bucket_table(max_dist):
    d = np.arange(max_dist, dtype=np.int64)
    max_exact = N_BUCKETS // 2
    nf = np.maximum(d, 1).astype(np.float32)
    large = max_exact + (np.log(nf / np.float32(max_exact)) / np.float32(math.log(T5_MAX_DISTANCE / max_exact))
                         * np.float32(N_BUCKETS - max_exact)).astype(np.int32)
    large = np.minimum(large, N_BUCKETS - 1)
    return np.where(d < max_exact, d, large).astype(np.int32)


_TILE_BASES = (0, MOBA_BLOCK)


def _bias_kernel(tab_ref, o_ref):
    blk, qc = MOBA_BLOCK, MOBA_BLOCK
    table = _t5_bucket_table(2 * blk + qc)
    first = [int(np.argmax(table >= b)) for b in range(N_BUCKETS)]
    j = lax.broadcasted_iota(i32, (blk, qc), 0)
    q = lax.broadcasted_iota(i32, (blk, qc), 1)
    for t, base in enumerate(_TILE_BASES):
        d = base + q - j
        lo, hi = max(base - (blk - 1), 0), base + qc - 1
        for h in range(HEADS):
            val = jnp.full((blk, qc), tab_ref[h], f32)
            for b in range(1, N_BUCKETS):
                if first[b] > hi:
                    continue
                if first[b] <= lo:
                    val = jnp.full((blk, qc), tab_ref[b * HEADS + h], f32)
                else:
                    val = jnp.where(d >= first[b], tab_ref[b * HEADS + h], val)
            o_ref[t, h] = jnp.where(d >= 0, val * LOG2E, NEG)


def _bias_tiles(rel_bias):
    return pl.pallas_call(
        _bias_kernel,
        in_specs=[pl.BlockSpec(memory_space=pltpu.SMEM)],
        out_shape=jax.ShapeDtypeStruct((len(_TILE_BASES), HEADS, MOBA_BLOCK, MOBA_BLOCK), f32),
        name="bias_tiles",
    )(rel_bias.astype(f32).reshape(-1))


def _attn_kernel(far_ref, qT_ref, k_ref, vT_ref, km_ref, bias_ref, mg_ref, o_ref, mask_ref, sa_ref, sb_ref):
    own = pl.program_id(1)
    blk, gw, hd = MOBA_BLOCK, GROUP_WIDTH, HEAD_DIM
    nb = km_ref.shape[1]

    nio = lax.broadcasted_iota(i32, (nb, blk), 0)
    past = nio < own
    km = km_ref[0]
    q_heads = []
    for h in range(HEADS):
        qh = qT_ref[0, h * hd:(h + 1) * hd, :]
        q_heads.append(qh)
        km_hi, km_lo = _split_bf16(km[:, h * hd:(h + 1) * hd])
        gate = (jnp.dot(km_hi, qh, preferred_element_type=f32)
                + jnp.dot(km_lo, qh, preferred_element_type=f32))
        gate = jnp.where(past, gate, -jnp.inf)
        picked = jnp.zeros((nb, blk), jnp.bool_)
        for _ in range(MOBA_TOPK):
            top = jnp.max(gate, axis=0, keepdims=True)
            first = jnp.min(jnp.where(gate == top, nio, nb), axis=0, keepdims=True)
            hit = nio == first
            picked = picked | hit
            gate = jnp.where(hit, -jnp.inf, gate)
        sel = picked & past
        mask_ref[0, h] = jnp.where(sel, 0.0, NEG)
        mask_ref[1, h] = jnp.where(sel & (nio < own - 1), far_ref[h] * LOG2E, NEG)

    def qk(n, h):
        kb = k_ref[0, h, pl.ds(pl.multiple_of(n * blk, blk), blk), :]
        return jnp.dot(kb, q_heads[h], preferred_element_type=f32)

    def far_scores(n, h):
        return qk(n, h) + mask_ref[1, h, pl.ds(n, 1), :]

    lane_groups = blk // ATTN_QUERY_GROUP

    def update(state, scores, n):
        out = []
        for h in range(HEADS):
            vb = vT_ref[0, n, h]
            for c in range(lane_groups):
                u = h * lane_groups + c
                m, acc = state[2 * u:2 * u + 2]
                s = scores[h][:, c * ATTN_QUERY_GROUP:(c + 1) * ATTN_QUERY_GROUP]
                m_new = jnp.maximum(m, jnp.max(s, axis=0, keepdims=True))
                alpha = jnp.exp2(m - m_new)
                p = jnp.exp2(s - m_new)
                acc = acc * alpha + jnp.dot(vb, p.astype(bf16), preferred_element_type=f32)
                out += [m_new, acc]
        return tuple(out)

    adj = jnp.maximum(own - 1, 0)
    n_far = jnp.maximum(own - 1, 0)
    s_own = [qk(own, h) + bias_ref[0, h] for h in range(HEADS)]
    s_adj = [qk(adj, h) + bias_ref[1, h] + mask_ref[0, h, pl.ds(adj, 1), :] for h in range(HEADS)]
    for h in range(HEADS):
        sa_ref[h] = far_scores(0, h)

    state = []
    for h in range(HEADS):
        vb = jnp.concatenate([vT_ref[0, own, h], vT_ref[0, adj, h]], axis=1)
        for c in range(lane_groups):
            so = s_own[h][:, c * ATTN_QUERY_GROUP:(c + 1) * ATTN_QUERY_GROUP]
            sj = s_adj[h][:, c * ATTN_QUERY_GROUP:(c + 1) * ATTN_QUERY_GROUP]
            m0 = jnp.maximum(jnp.max(so, axis=0, keepdims=True), jnp.max(sj, axis=0, keepdims=True))
            p = jnp.concatenate([jnp.exp2(so - m0).astype(bf16), jnp.exp2(sj - m0).astype(bf16)], axis=0)
            state += [m0, jnp.dot(vb, p, preferred_element_type=f32)]
    state = tuple(state)

    def body(i, state):
        first = jnp.minimum(2 * i, nb - 1)
        second = jnp.minimum(2 * i + 1, nb - 1)
        third = jnp.minimum(2 * i + 2, nb - 1)
        for h in range(HEADS):
            sb_ref[h] = far_scores(second, h)
        state = update(state, [sa_ref[h] for h in range(HEADS)], first)
        for h in range(HEADS):
            sa_ref[h] = far_scores(third, h)
        return update(state, [sb_ref[h] for h in range(HEADS)], second)

    fin = lax.fori_loop(0, (n_far + 1) // 2, body, state)

    def normalised(acc):
        return acc[0:hd, :] / acc[hd:hd + 1, :]

    outT = jnp.concatenate(
        [jnp.concatenate([normalised(fin[2 * (h * lane_groups + c) + 1]) for c in range(lane_groups)], axis=1)
         for h in range(HEADS)], axis=0)
    o_ref[0] = _group_rms(outT.T, mg_ref[:, gw:2 * gw]).astype(bf16)


def _moba_attention(qT, k, vT, kmean, bias_tiles, far_bias, mg):
    B, _, S, _ = k.shape
    gw, blk = GROUP_WIDTH, MOBA_BLOCK
    nb = kmean.shape[1]
    return pl.pallas_call(
        _attn_kernel,
        grid=(B, nb),
        in_specs=[pl.BlockSpec(memory_space=pltpu.SMEM),
                  pl.BlockSpec((1, gw, blk), lambda b, c: (b, 0, c)),
                  pl.BlockSpec((1, HEADS, S, HEAD_DIM), lambda b, c: (b, 0, 0, 0)),
                  pl.BlockSpec((1, nb, HEADS, V_ROWS, blk), lambda b, c: (b, 0, 0, 0, 0)),
                  pl.BlockSpec((1, nb, gw), lambda b, c: (b, 0, 0)),
                  pl.BlockSpec(bias_tiles.shape, lambda b, c: (0, 0, 0, 0)),
                  pl.BlockSpec(mg.shape, lambda b, c: (0, 0))],
        out_specs=pl.BlockSpec((1, blk, gw), lambda b, c: (b, c, 0)),
        out_shape=jax.ShapeDtypeStruct((B, S, gw), bf16),
        scratch_shapes=[pltpu.VMEM((2, HEADS, nb, blk), f32),
                        pltpu.VMEM((HEADS, blk, blk), f32), pltpu.VMEM((HEADS, blk, blk), f32)],
        compiler_params=_cparams(("arbitrary", "arbitrary")),
        name="moba_attention",
    )(far_bias, qT, k, vT, kmean, bias_tiles, mg)


def _outproj_router_kernel(yc_ref, ya_ref, yp_ref, ys_ref, x_ref, g1_ref, wo_ref, n2_ref, sh_ref, sc_ref,
                           rw_ref, rb_ref,
                           x1_ref, h2_ref, dest_ref, gate_ref, pad_ref):
    W = ROUTE_TILE
    windows = range(ROUTE_WINDOWS_PER_STEP)
    rows = [slice(u * W, (u + 1) * W) for u in windows]
    mixed = [jnp.concatenate([r[rows[u], :] for r in (yc_ref, ya_ref, yp_ref, ys_ref)], axis=1) for u in windows]
    projected = [jnp.dot(m, wo_ref[...], preferred_element_type=f32) for m in mixed]

    his, los = [], []
    for u in windows:
        x1 = x_ref[rows[u], :] + g1_ref[0] * projected[u]
        x1_ref[rows[u], :] = x1
        ms = jnp.mean(x1 * x1, axis=-1, keepdims=True)
        h = x1 * lax.rsqrt(ms + EPS) * n2_ref[...]
        h = h * (1.0 + sc_ref[0]) + sh_ref[0]
        h_hi, h_lo = _split_bf16(h)
        h2_ref[rows[u], :] = h_hi
        his.append(h_hi)
        los.append(h_lo)

    nt = (((1,), (1,)), ((), ()))
    rw_hi, rw_lo = _split_bf16(rw_ref[...])
    logits = [(lax.dot_general(rw_hi, his[u], nt, preferred_element_type=f32)
               + lax.dot_general(rw_hi, los[u], nt, preferred_element_type=f32)
               + lax.dot_general(rw_lo, his[u], nt, preferred_element_type=f32)) + rb_ref[...] for u in windows]

    eio = lax.broadcasted_iota(i32, (N_EXPERTS, W), 0)
    sels, multis = [], []
    for u in windows:
        work = logits[u]
        vals, sel_u = [], []
        for k in range(TOP_K):
            m = jnp.max(work, axis=0, keepdims=True)
            idx = jnp.min(jnp.where(work == m, eio, N_EXPERTS), axis=0, keepdims=True)
            sel = eio == idx
            vals.append(m)
            sel_u.append(sel)
            work = jnp.where(sel, -jnp.inf, work)
        exps = [jnp.exp(v - vals[0]) for v in vals]
        denom = exps[0] + exps[1] + exps[2] + exps[3]
        for k in range(TOP_K):
            gate_ref[k:k + 1, rows[u]] = exps[k] / denom
        multi = jnp.zeros((N_EXPERTS, W), f32)
        for sel in sel_u:
            multi = multi + sel.astype(f32)
        sels.append(sel_u)
        multis.append(multi)

    before = (lax.broadcasted_iota(i32, (W, W), 0) < lax.broadcasted_iota(i32, (W, W), 1)).astype(bf16)
    earlier = [jnp.dot(multis[u].astype(bf16), before, preferred_element_type=f32) for u in windows]

    lower = (lax.broadcasted_iota(i32, (N_EXPERTS, N_EXPERTS), 1)
             < lax.broadcasted_iota(i32, (N_EXPERTS, N_EXPERTS), 0)).astype(bf16)
    seg_start = []
    for u in windows:
        cnt = jnp.sum(multis[u], axis=1, keepdims=True).astype(i32)
        padded = (cnt + (ROW_CHUNK - 1)) // ROW_CHUNK * ROW_CHUNK
        pad_ref[u] = padded
        seg_start.append(jnp.dot(lower, jnp.broadcast_to(padded.astype(f32), (N_EXPERTS, W)).astype(bf16),
                                 preferred_element_type=f32))
    for u in windows:
        row = seg_start[u] + earlier[u]
        for k in range(TOP_K):
            dest_ref[k:k + 1, rows[u]] = jnp.sum(jnp.where(sels[u][k], row, 0.0), axis=0,
                                                 keepdims=True).astype(i32)


def _outproj_router(yc, ya, yp, ys, x, gate1, wo_bf, n2g, shift2, scale2, rwT, rb, seq):
    N, D = x.shape
    gw = GROUP_WIDTH
    nw = N // ROUTE_TILE
    wps = ROUTE_WINDOWS_PER_STEP
    W = ROUTE_TILE * wps
    per_b = seq // W
    row = lambda i: (i, 0)
    vec = lambda i: (i // per_b, 0, 0)
    full = lambda i: (0, 0)
    colblk = lambda i: (0, i)
    return pl.pallas_call(
        _outproj_router_kernel,
        grid=(nw // wps,),
        in_specs=[pl.BlockSpec((W, gw), row)] * 4 + [
            pl.BlockSpec((W, D), row),
            pl.BlockSpec((1, 1, D), vec),
            pl.BlockSpec(wo_bf.shape, full),
            pl.BlockSpec((1, D), full),
            pl.BlockSpec((1, 1, D), vec),
            pl.BlockSpec((1, 1, D), vec),
            pl.BlockSpec(rwT.shape, full),
            pl.BlockSpec(rb.shape, full)],
        out_specs=[pl.BlockSpec((W, D), row),
                   pl.BlockSpec((W, D), row),
                   pl.BlockSpec((TOP_K, W), colblk),
                   pl.BlockSpec((TOP_K, W), colblk),
                   pl.BlockSpec((wps, N_EXPERTS, 1), lambda i: (i, 0, 0))],
        out_shape=[jax.ShapeDtypeStruct((N, D), f32),
                   jax.ShapeDtypeStruct((N, D), bf16),
                   jax.ShapeDtypeStruct((TOP_K, N), i32),
                   jax.ShapeDtypeStruct((TOP_K, N), f32),
                   jax.ShapeDtypeStruct((nw, N_EXPERTS, 1), i32)],
        compiler_params=_cparams(("arbitrary",)),
        name="outproj_router",
    )(yc, ya, yp, ys, x, gate1, wo_bf, n2g, shift2, scale2, rwT, rb)


def _max_window_rows():
    return -(-(ROUTE_TILE * TOP_K + N_EXPERTS * (ROW_CHUNK - 1)) // 128) * 128


def _pack_pairs(lo, hi):
    lo_bits = lax.bitcast_convert_type(lo, jnp.uint32)
    hi_bits = lax.bitcast_convert_type(hi, jnp.uint32)
    return lax.bitcast_convert_type(hi_bits | (lo_bits >> 16), i32)


def _unpack_pairs(words):
    bits = lax.bitcast_convert_type(words, jnp.uint32)
    lo = lax.bitcast_convert_type(bits << 16, f32).astype(bf16)
    hi = lax.bitcast_convert_type(bits & jnp.uint32(0xFFFF0000), f32).astype(bf16)
    return lo, hi


def _dispatch_kernel(piece_src_s, piece_dst_s, wcopies_s, tail_start_s, tail_chunks_s, n_used_s,
                     h_ref, dest_ref, xs_ref, sorted_ref, zero_ref, sem):
    w = pl.program_id(0)
    nw = pl.num_programs(0)
    W = ROUTE_TILE
    R = sorted_ref.shape[1]
    half = sorted_ref.shape[2]
    slot = w % 2
    rio = lax.broadcasted_iota(i32, (R, W), 0)
    hit = rio == dest_ref[0:1, :]
    for k in range(1, TOP_K):
        hit = hit | (rio == dest_ref[k:k + 1, :])
    onehot = jnp.where(hit, 1.0, 0.0).astype(bf16)
    rows = jnp.dot(onehot, h_ref[...], preferred_element_type=f32)
    sorted_ref[slot] = _pack_pairs(rows[:, :half], rows[:, half:])

    def piece_copy(s, src_row, dst_row):
        return pltpu.make_async_copy(
            sorted_ref.at[s, pl.ds(pl.multiple_of(src_row, ROW_CHUNK), DISPATCH_COPY_ROWS), :],
            xs_ref.at[pl.ds(pl.multiple_of(dst_row, ROW_CHUNK), DISPATCH_COPY_ROWS), :], sem.at[s])

    def zero_copy(dst_row):
        return pltpu.make_async_copy(
            zero_ref.at[0:ROW_CHUNK, :],
            xs_ref.at[pl.ds(pl.multiple_of(dst_row, ROW_CHUNK), ROW_CHUNK), :], sem.at[0])

    def zero_tile_copy(tile):
        return pltpu.make_async_copy(
            zero_ref, xs_ref.at[pl.ds(pl.multiple_of(tile * FFN_TILE, FFN_TILE), FFN_TILE), :], sem.at[0])

    def drain(s, count):
        @pl.when(count > 0)
        def _():
            rows = pl.multiple_of(count * DISPATCH_COPY_ROWS, DISPATCH_COPY_ROWS)
            pltpu.make_async_copy(xs_ref.at[pl.ds(0, rows), :], xs_ref.at[pl.ds(0, rows), :], sem.at[s]).wait()

    @pl.when(w > 0)
    def _():
        drain(1 - slot, wcopies_s[jnp.maximum(w - 1, 0)])

    for parity in range(2):
        @pl.when(slot == parity)
        def _():
            def issue(q, carry):
                piece_copy(parity, piece_src_s[w * MAX_DISPATCH_PIECES + q],
                           piece_dst_s[w * MAX_DISPATCH_PIECES + q]).start()
                return carry
            lax.fori_loop(0, wcopies_s[w], issue, 0)

    @pl.when(w == nw - 1)
    def _():
        drain(slot, wcopies_s[w])
        zero_ref[...] = jnp.zeros(zero_ref.shape, i32)

        def per_tail(e, total):
            n = tail_chunks_s[e]
            dst = tail_start_s[e]

            def issue(j, c):
                zero_copy(dst + j * ROW_CHUNK).start()
                return c
            lax.fori_loop(0, n, issue, 0)
            return total + n
        tails = lax.fori_loop(0, N_EXPERTS, per_tail, 0)

        def drain_tail(j, c):
            zero_copy(0).wait()
            return c
        lax.fori_loop(0, tails, drain_tail, 0)

        n_tiles = xs_ref.shape[0] // FFN_TILE

        def issue_tile(j, c):
            zero_tile_copy(j).start()
            return c
        lax.fori_loop(n_used_s[0], n_tiles, issue_tile, 0)

        def drain_tile(j, c):
            zero_tile_copy(0).wait()
            return c
        lax.fori_loop(n_used_s[0], n_tiles, drain_tile, 0)


def _dispatch(h2, destT, t, p_rows):
    N, D = h2.shape
    W = ROUTE_TILE
    nw = N // W
    R = _max_window_rows()
    assert R >= W * TOP_K + N_EXPERTS * (ROW_CHUNK - 1) + DISPATCH_COPY_ROWS - ROW_CHUNK
    grid_spec = pltpu.PrefetchScalarGridSpec(
        num_scalar_prefetch=6,
        grid=(nw,),
        in_specs=[pl.BlockSpec((W, D), lambda w, *_: (w, 0)),
                  pl.BlockSpec((TOP_K, W), lambda w, *_: (0, w))],
        out_specs=pl.BlockSpec(memory_space=pl.ANY),
        scratch_shapes=[pltpu.VMEM((2, R, D // 2), i32), pltpu.VMEM((FFN_TILE, D // 2), i32),
                        pltpu.SemaphoreType.DMA((2,))],
    )
    return pl.pallas_call(
        _dispatch_kernel,
        grid_spec=grid_spec,
        out_shape=jax.ShapeDtypeStruct((p_rows, D // 2), i32),
        compiler_params=_cparams(("arbitrary",)),
        name="expert_dispatch",
    )(t['piece_src'], t['piece_dst'], t['wcopies'], t['tail_start'], t['tail_chunks'], t['n_used'], h2, destT)


def _ffn_kernel(tile_expert_s, next_expert_s, n_used_s, x_ref, w1_hbm, b1_ref, w2_hbm, b2_ref, y_ref,
                w1f_ref, w2f_ref, w1b_ref, w2b_ref, sem, *, layer):
    i = pl.program_id(0)
    last = n_used_s[0] - 1
    expert = tile_expert_s[jnp.minimum(i, last)]
    prev_expert = tile_expert_s[jnp.minimum(jnp.maximum(i - 1, 0), last)]

    def weight_copies(e):
        return (pltpu.make_async_copy(w1_hbm.at[layer, e], w1f_ref, sem.at[0]),
                pltpu.make_async_copy(w2_hbm.at[layer, e], w2f_ref, sem.at[1]))

    @pl.when(i == 0)
    def _():
        for c in weight_copies(expert):
            c.start()

    @pl.when((i == 0) | (expert != prev_expert))
    def _():
        for c in weight_copies(expert):
            c.wait()
        w1b_ref[...] = w1f_ref[...].astype(bf16)
        w2b_ref[...] = w2f_ref[...].astype(bf16)
        nxt = next_expert_s[expert]

        @pl.when(nxt < N_EXPERTS)
        def _():
            for c in weight_copies(nxt):
                c.start()

    @pl.when(i < n_used_s[0])
    def _():
        dff = w2b_ref.shape[0]
        half = x_ref.shape[1]
        sub = x_ref.shape[0] // FFN_SUBTILES
        hidden = []
        for r in range(FFN_SUBTILES):
            x_lo, x_hi = _unpack_pairs(x_ref[r * sub:(r + 1) * sub, :])
            hidden.append(jnp.dot(x_lo, w1b_ref[0:half, :], preferred_element_type=f32)
                          + jnp.dot(x_hi, w1b_ref[half:2 * half, :], preferred_element_type=f32))
        for r in range(FFN_SUBTILES):
            hh = hidden[r] + b1_ref[0, 0]
            x_glu = jnp.minimum(hh[:, :dff], SWIGLU_LIMIT)
            x_lin = jnp.clip(hh[:, dff:], -SWIGLU_LIMIT, SWIGLU_LIMIT)
            act = x_glu * jax.nn.sigmoid(SWIGLU_ALPHA * x_glu) * (x_lin + 1.0)
            y = jnp.dot(act.astype(bf16), w2b_ref[...], preferred_element_type=f32) + b2_ref[0, 0]
            y = y.astype(bf16).astype(f32)
            y_ref[r * sub:(r + 1) * sub, :] = _pack_pairs(y[:, :half], y[:, half:])

    @pl.when(i >= n_used_s[0])
    def _():
        y_ref[...] = jnp.zeros(y_ref.shape, i32)


def _expert_ffn(xs, w1, b1, w2, b2, t, layer):
    P, half = xs.shape
    L, E, D, F2 = w1.shape
    tm = FFN_TILE
    nt = P // tm

    def tile(i, te, ne, nu):
        return (jnp.minimum(i, nu[0] - 1), 0)

    def expert4(i, te, ne, nu):
        return (layer, te[jnp.minimum(i, nu[0] - 1)], 0, 0)

    grid_spec = pltpu.PrefetchScalarGridSpec(
        num_scalar_prefetch=3,
        grid=(nt,),
        in_specs=[pl.BlockSpec((tm, half), tile),
                  pl.BlockSpec(memory_space=pl.ANY),
                  pl.BlockSpec((1, 1, 1, F2), expert4),
                  pl.BlockSpec(memory_space=pl.ANY),
                  pl.BlockSpec((1, 1, 1, D), expert4)],
        out_specs=pl.BlockSpec((tm, half), lambda i, te, ne, nu: (i, 0)),
        scratch_shapes=[pltpu.VMEM((D, F2), f32), pltpu.VMEM((F2 // 2, D), f32),
                        pltpu.VMEM((D, F2), bf16), pltpu.VMEM((F2 // 2, D), bf16),
                        pltpu.SemaphoreType.DMA((2,))],
    )
    return pl.pallas_call(
        functools.partial(_ffn_kernel, layer=layer),
        grid_spec=grid_spec,
        out_shape=jax.ShapeDtypeStruct((P, half), i32),
        compiler_params=_cparams(("arbitrary",)),
        name="expert_ffn",
    )(t['tile_expert'], t['next_expert'], t['n_used'], xs, w1, b1.reshape(L, E, 1, F2), w2,
      b2.reshape(L, E, 1, D))


def _combine_kernel(big_src_s, big_dst_s, wbig_s, small_src_s, small_dst_s, wsmall_s, wchunks_s,
                    ys_ref, dest_ref, gate_ref, x1_ref, g2_ref, fg_ref, o_ref, local_ref, sem, *, final):
    w = pl.program_id(0)
    nw = pl.num_programs(0)
    W = ROUTE_TILE
    R = local_ref.shape[1]
    slot = w % 2

    def rows_copy(s, src_row, dst_row, rows):
        return pltpu.make_async_copy(
            ys_ref.at[pl.ds(pl.multiple_of(src_row, ROW_CHUNK), rows), :],
            local_ref.at[s, pl.ds(pl.multiple_of(dst_row, ROW_CHUNK), rows), :], sem.at[s])

    def fetch(win, s):
        def big(q, carry):
            rows_copy(s, big_src_s[win * MAX_COMBINE_PIECES + q], big_dst_s[win * MAX_COMBINE_PIECES + q],
                      COMBINE_COPY_ROWS).start()
            return carry
        lax.fori_loop(0, wbig_s[win], big, 0)

        def small(q, carry):
            rows_copy(s, small_src_s[win * MAX_COMBINE_CHUNKS + q], small_dst_s[win * MAX_COMBINE_CHUNKS + q],
                      ROW_CHUNK).start()
            return carry
        lax.fori_loop(0, wsmall_s[win], small, 0)

    @pl.when(w == 0)
    def _():
        local_ref[...] = jnp.zeros(local_ref.shape, i32)
        fetch(0, 0)

    for parity in range(2):
        @pl.when((w + 1 < nw) & (slot == parity))
        def _():
            fetch(jnp.minimum(w + 1, nw - 1), 1 - parity)

    @pl.when(wchunks_s[w] > 0)
    def _():
        rows = pl.multiple_of(wchunks_s[w] * ROW_CHUNK, ROW_CHUNK)
        pltpu.make_async_copy(ys_ref.at[pl.ds(0, rows), :], local_ref.at[slot, pl.ds(0, rows), :],
                              sem.at[slot]).wait()

    rio = lax.broadcasted_iota(i32, (R, W), 0)
    weights = jnp.zeros((R, W), f32)
    for k in range(TOP_K):
        weights = jnp.where(rio == dest_ref[k:k + 1, :], gate_ref[k:k + 1, :], weights)
    weights = weights.astype(bf16)
    y_lo, y_hi = _unpack_pairs(local_ref[slot])
    tn = (((0,), (0,)), ((), ()))
    moe = jnp.concatenate([lax.dot_general(weights, y_lo, tn, preferred_element_type=f32),
                           lax.dot_general(weights, y_hi, tn, preferred_element_type=f32)], axis=1)
    x2 = x1_ref[...] + g2_ref[0] * moe
    if final:
        ms = jnp.mean(x2 * x2, axis=-1, keepdims=True)
        x2 = x2 * lax.rsqrt(ms + EPS) * fg_ref[...]
    o_ref[...] = x2


def _combine(ys, dest, gates, x1, gate2, final_g, t, seq, final):
    N, D = x1.shape
    W = ROUTE_TILE
    nw = N // W
    per_b = seq // W
    R = _max_window_rows()
    grid_spec = pltpu.PrefetchScalarGridSpec(
        num_scalar_prefetch=7,
        grid=(nw,),
        in_specs=[pl.BlockSpec(memory_space=pl.ANY),
                  pl.BlockSpec((TOP_K, W), lambda w, *_: (0, w)),
                  pl.BlockSpec((TOP_K, W), lambda w, *_: (0, w)),
                  pl.BlockSpec((W, D), lambda w, *_: (w, 0)),
                  pl.BlockSpec((1, 1, D), lambda w, *_: (w // per_b, 0, 0)),
                  pl.BlockSpec((1, D), lambda w, *_: (0, 0))],
        out_specs=pl.BlockSpec((W, D), lambda w, *_: (w, 0)),
        scratch_shapes=[pltpu.VMEM((2, R, D // 2), i32), pltpu.SemaphoreType.DMA((2,))],
    )
    return pl.pallas_call(
        functools.partial(_combine_kernel, final=final),
        grid_spec=grid_spec,
        out_shape=jax.ShapeDtypeStruct((N, D), f32),
        compiler_params=_cparams(("arbitrary",)),
        name="expert_combine",
    )(t['big_src'], t['big_dst'], t['wbig'], t['small_src'], t['small_dst'], t['wsmall'], t['wchunks'],
      ys, dest, gates, x1, gate2, final_g)


def _routing_tables(padded, n_tiles):
    nw, E = padded.shape
    lstart = jnp.cumsum(padded, axis=1) - padded
    tot = jnp.sum(padded, axis=0)
    slack = DISPATCH_COPY_ROWS - ROW_CHUNK
    region = jnp.where(tot > 0, (tot + slack + FFN_TILE - 1) // FFN_TILE * FFN_TILE, 0)
    region_end = jnp.cumsum(region)
    region_start = region_end - region
    gbase = region_start[None, :] + jnp.cumsum(padded, axis=0) - padded
    nchunk = padded // ROW_CHUNK
    tail_start = region_start + tot
    tail_chunks = (region - tot) // ROW_CHUNK
    n_used = (region_end[-1] // FFN_TILE).astype(i32).reshape(1)
    tile_row = jnp.arange(n_tiles, dtype=i32) * FFN_TILE
    tile_expert = jnp.minimum(
        jnp.sum((region_end[None, :] <= tile_row[:, None]).astype(i32), axis=1), E - 1).astype(i32)
    eidx = jnp.arange(E, dtype=i32)
    later_nonempty = (eidx[None, :] > eidx[:, None]) & (region[None, :] > 0)
    next_expert = jnp.min(jnp.where(later_nonempty, eidx[None, :], E), axis=1)
    flat = lambda a: a.reshape(-1).astype(i32)

    def copy_list(counts, window_rows, expert_rows, step, max_items):
        ends = jnp.cumsum(counts, axis=1)
        slot = jnp.arange(max_items, dtype=i32)
        owner = jnp.sum((ends[:, None, :] <= slot[None, :, None]).astype(i32), axis=2)
        hit = owner[:, :, None] == eidx[None, None, :]
        pick = lambda a: jnp.sum(jnp.where(hit, a[:, None, :], 0), axis=2)
        offset = step * (slot[None, :] - pick(ends - counts))
        return flat(pick(window_rows) + offset), flat(pick(expert_rows) + offset), flat(ends[:, -1])

    ncopy = (padded + DISPATCH_COPY_ROWS - 1) // DISPATCH_COPY_ROWS
    piece_src, piece_dst, wcopies = copy_list(ncopy, lstart, gbase, DISPATCH_COPY_ROWS, MAX_DISPATCH_PIECES)
    nbig = padded // COMBINE_COPY_ROWS
    big_dst, big_src, wbig = copy_list(nbig, lstart, gbase, COMBINE_COPY_ROWS, MAX_COMBINE_PIECES)
    rest = nbig * COMBINE_COPY_ROWS
    small_dst, small_src, wsmall = copy_list(nchunk - nbig * (COMBINE_COPY_ROWS // ROW_CHUNK),
                                             lstart + rest, gbase + rest, ROW_CHUNK, MAX_COMBINE_CHUNKS)
    return dict(next_expert=flat(next_expert), wchunks=flat(jnp.sum(nchunk, axis=1)),
                piece_src=piece_src, piece_dst=piece_dst, wcopies=wcopies,
                big_src=big_src, big_dst=big_dst, wbig=wbig,
                small_src=small_src, small_dst=small_dst, wsmall=wsmall,
                tail_start=flat(tail_start), tail_chunks=flat(tail_chunks),
                n_used=n_used, tile_expert=tile_expert)


def _block_diag(w):
    g, a, b = w.shape
    out = jnp.zeros((g * a, g * b), w.dtype)
    for i in range(g):
        out = out.at[i * a:(i + 1) * a, i * b:(i + 1) * b].set(w[i])
    return out


def kernel(x, c, w_mod, b_mod, norm1_g, w_in, conv_w, conv_b, conv_norm_g, conv_norm_b, rel_bias, pool_w, pool_scale, sgu_norm_g, sgu_norm_b, sgu_w, sgu_b, mix_out_g, w_out, norm2_g, router_w, router_b, exp_w1, exp_b1, exp_w2, exp_b2, final_norm_g):
    B, S, D = x.shape
    L = w_mod.shape[0]
    N = B * S
    nw = N // ROUTE_TILE
    assert S % SEQ_TILE == 0 and S % MOBA_BLOCK == 0 and N % ROUTE_TILE == 0 and S % ROUTE_TILE == 0
    p_bound = (N * TOP_K + nw * N_EXPERTS * (ROW_CHUNK - 1)
               + N_EXPERTS * (DISPATCH_COPY_ROWS - ROW_CHUNK + FFN_TILE - 1))
    n_tiles = -(-p_bound // FFN_TILE)
    p_rows = n_tiles * FFN_TILE

    mod = _modulation(c, w_mod, b_mod)
    bias_tiles = _bias_tiles(rel_bias)
    far_bucket = int(_t5_bucket_table(MOBA_BLOCK + 2)[MOBA_BLOCK + 1])
    assert far_bucket == int(_t5_bucket_table(S + 1)[S])
    far_bias = rel_bias[far_bucket].astype(f32)
    row = lambda a: a.reshape(1, -1)
    for l in range(L):
        m6 = mod[l].reshape(B, 6, 1, D)
        shift1, scale1, gate1, shift2, scale2, gate2 = (m6[:, j] for j in range(6))
        mg = row(mix_out_g[l])
        pc, qT, k, vT, kmean, pp, ps = _in_projection(x, shift1, scale1, row(norm1_g[l]), w_in[l].astype(bf16))
        yc, yp, ys = _local_mixers(
            pc, pp, ps, conv_w[l], row(conv_b[l]), row(conv_norm_g[l]), row(conv_norm_b[l]),
            _block_diag(pool_w[l]).astype(bf16), row(pool_scale[l]),
            row(sgu_norm_g[l]), row(sgu_norm_b[l]), sgu_w[l],
            jnp.repeat(sgu_b[l].T, HEAD_DIM, axis=1), mg)
        ya = _moba_attention(qT, k, vT, kmean.reshape(B, -1, GROUP_WIDTH), bias_tiles, far_bias, mg)
        flat = lambda a: a.reshape(N, -1)
        x1, h2, destT, gateT, padded = _outproj_router(
            flat(yc), flat(ya), flat(yp), flat(ys), x.reshape(N, D), gate1, w_out[l].astype(bf16),
            row(norm2_g[l]), shift2, scale2, router_w[l].T, router_b[l].reshape(-1, 1), S)
        t = _routing_tables(padded.reshape(nw, N_EXPERTS), n_tiles)
        xs = _dispatch(h2, destT, t, p_rows)
        ysort = _expert_ffn(xs, exp_w1, exp_b1, exp_w2, exp_b2, t, l)
        x = _combine(ysort, destT, gateT, x1, gate2, row(final_norm_g), t, S,
                     final=(l == L - 1)).reshape(B, S, D)
    return x
```

```python
import functools
import math

import numpy as np
import jax
import jax.numpy as jnp
from jax import lax
from jax.experimental import pallas as pl
from jax.experimental.pallas import tpu as pltpu

f32, bf16, i32 = jnp.float32, jnp.bfloat16, jnp.int32

GROUP_WIDTH = 256
HEADS = 4
HEAD_DIM = 64
V_ROWS = HEAD_DIM + 16
LOG2E = math.log2(math.e)
CONV_WIDTH = 31
MOBA_BLOCK = 256
MOBA_TOPK = 3
Q_CHUNK = 128
N_BUCKETS = 32
T5_MAX_DISTANCE = 128
POOL_WINDOWS = (2, 4, 8, 16)
SGU_CHUNK = 128
N_EXPERTS = 32
TOP_K = 4
SWIGLU_LIMIT = 7.0
SWIGLU_ALPHA = 1.702
EPS = 1e-6

SUBLANES = 8
LANES = 128
ATTN_QUERY_GROUP = 256
HALO = 32
SEQ_TILE = 512
ROUTE_TILE = 256
ROUTE_WINDOWS_PER_STEP = 4
ROW_CHUNK = 8
DISPATCH_COPY_ROWS = 32
COMBINE_COPY_ROWS = 32
MAX_DISPATCH_PIECES = ROUTE_TILE * TOP_K // DISPATCH_COPY_ROWS + N_EXPERTS
MAX_COMBINE_PIECES = (ROUTE_TILE * TOP_K + N_EXPERTS * (ROW_CHUNK - 1)) // COMBINE_COPY_ROWS
MAX_COMBINE_CHUNKS = N_EXPERTS * (COMBINE_COPY_ROWS // ROW_CHUNK - 1)
FFN_TILE = 512
FFN_SUBTILES = 2
NEG = -1e30
VMEM_LIMIT = 56 * 1024 * 1024


def _cparams(sem):
    return pltpu.CompilerParams(dimension_semantics=sem, vmem_limit_bytes=VMEM_LIMIT)


def _split_bf16(a):
    hi = a.astype(bf16)
    lo = (a - hi.astype(f32)).astype(bf16)
    return hi, lo


def _mod_kernel(c_ref, w_ref, b_ref, o_ref):
    c = c_ref[...]
    cond = c * jax.nn.sigmoid(c)
    o_ref[0] = jnp.dot(cond, w_ref[0], preferred_element_type=f32,
                       precision=lax.Precision.HIGHEST) + b_ref[0]


def _modulation(c, w_mod, b_mod):
    L, D, M = w_mod.shape
    B = c.shape[0]
    tn = 1536
    return pl.pallas_call(
        _mod_kernel,
        grid=(L, M // tn),
        in_specs=[pl.BlockSpec((B, D), lambda l, j: (0, 0)),
                  pl.BlockSpec((1, D, tn), lambda l, j: (l, 0, j)),
                  pl.BlockSpec((1, 1, tn), lambda l, j: (l, 0, j))],
        out_specs=pl.BlockSpec((1, B, tn), lambda l, j: (l, 0, j)),
        out_shape=jax.ShapeDtypeStruct((L, B, M), f32),
        compiler_params=_cparams(("arbitrary", "arbitrary")),
        name="modulation",
    )(c, w_mod, b_mod.reshape(L, 1, M))


def _project_mix_kernel(x_ref, sh_ref, sc_ref, g_ref, w_ref,
                        cw_ref, cb_ref, cng_ref, cnb_ref, pw_ref, psc_ref, sg_ref, sb_ref, sw_ref, sbias_ref, mg_ref,
                        qT_ref, k_ref, vT_ref, km_ref, yc_ref, yp_ref, ys_ref,
                        gext, gshift, zext, s2, s4, s8):
    first = pl.program_id(1) == 0

    @pl.when(first)
    def _():
        gext[0:HALO, :] = jnp.zeros((HALO, GROUP_WIDTH), f32)
        zext[0:HALO, :] = jnp.zeros((HALO, GROUP_WIDTH), f32)

    @pl.when(jnp.logical_not(first))
    def _():
        gext[0:HALO, :] = gext[SEQ_TILE:SEQ_TILE + HALO, :]
        zext[0:HALO, :] = zext[SEQ_TILE:SEQ_TILE + HALO, :]

    x = x_ref[0]
    ms = jnp.mean(x * x, axis=-1, keepdims=True)
    h = x * lax.rsqrt(ms + EPS) * g_ref[...]
    h = h * (1.0 + sc_ref[0]) + sh_ref[0]
    gw = GROUP_WIDTH
    hb = h.astype(bf16)
    pc = jnp.dot(hb, w_ref[:, 0:2 * gw], preferred_element_type=f32)
    qkv = jnp.dot(hb, w_ref[:, 2 * gw:5 * gw], preferred_element_type=f32)
    rest = jnp.dot(hb, w_ref[:, 5 * gw:8 * gw], preferred_element_type=f32)
    _local_mixers(pc, rest[:, 0:gw], rest[:, gw:3 * gw],
                  cw_ref, cb_ref, cng_ref, cnb_ref, pw_ref, psc_ref, sg_ref, sb_ref, sw_ref, sbias_ref, mg_ref,
                  yc_ref, yp_ref, ys_ref, gext, gshift, zext, s2, s4, s8)
    proj = jnp.concatenate([pc, qkv], axis=1)
    q = proj[:, 2 * gw:3 * gw] * (HEAD_DIM ** -0.5 * LOG2E)
    qT_ref[0] = q.T.astype(bf16)
    kk = proj[:, 3 * gw:4 * gw]
    for h in range(HEADS):
        k_ref[0, h] = kk[:, h * HEAD_DIM:(h + 1) * HEAD_DIM].astype(bf16)
    for j in range(SEQ_TILE // MOBA_BLOCK):
        km_ref[0, 0, j:j + 1, :] = jnp.mean(kk[j * MOBA_BLOCK:(j + 1) * MOBA_BLOCK], axis=0, keepdims=True)
    for j in range(SEQ_TILE // MOBA_BLOCK):
        vT = proj[j * MOBA_BLOCK:(j + 1) * MOBA_BLOCK, 4 * gw:5 * gw].T.astype(bf16)
        for h in range(HEADS):
            vT_ref[0, j, h] = jnp.concatenate(
                [vT[h * HEAD_DIM:(h + 1) * HEAD_DIM, :], jnp.ones((V_ROWS - HEAD_DIM, MOBA_BLOCK), bf16)], axis=0)


def _group_rms(y, g):
    return y * lax.rsqrt(jnp.mean(y * y, axis=-1, keepdims=True) + EPS) * g


def _local_mixers(pc, z, zz,
                  cw_ref, cb_ref, cng_ref, cnb_ref, pw_ref, psc_ref,
                  sg_ref, sb_ref, sw_ref, sbias_ref, mg_ref,
                  yc_ref, yp_ref, ys_ref,
                  gext, gshift, zext, s2, s4, s8):
    i = pl.program_id(1)
    ts = SEQ_TILE
    gw = GROUP_WIDTH
    lane = lax.broadcasted_iota(i32, (1, gw), 1)

    g = pc[:, :gw] * jax.nn.sigmoid(pc[:, gw:])
    gext[HALO:HALO + ts, :] = g
    span = ts + HALO - SUBLANES
    for a in range(1, SUBLANES):
        gshift[a - 1, 0:span, :] = gext[pl.ds(a, span), :]
    acc = jnp.zeros((ts, gw), f32)
    for j in range(CONV_WIDTH):
        b, a = divmod(HALO - (CONV_WIDTH - 1) + j, SUBLANES)
        tap = gext[pl.ds(SUBLANES * b, ts), :] if a == 0 else gshift[a - 1, pl.ds(SUBLANES * b, ts), :]
        acc = acc + cw_ref[j:j + 1, :] * tap
    y = acc + cb_ref[...]
    r = lax.broadcasted_iota(i32, (gw, gw), 0) // HEAD_DIM
    c = lax.broadcasted_iota(i32, (gw, gw), 1) // HEAD_DIM
    avg = jnp.where(r == c, 1.0 / HEAD_DIM, 0.0).astype(bf16)

    def head_mean(t):
        hi, lo = _split_bf16(t)
        return (jnp.dot(hi, avg, preferred_element_type=f32)
                + jnp.dot(lo, avg, preferred_element_type=f32))

    mu = head_mean(y)
    yc = y - mu
    var = head_mean(yc * yc)
    yn = yc * lax.rsqrt(var + EPS) * cng_ref[...] + cnb_ref[...]
    yconv = yn * jax.nn.sigmoid(yn)
    yc_ref[0] = _group_rms(yconv, mg_ref[:, 0:gw]).astype(bf16)

    zext[HALO:HALO + ts, :] = z
    n2, n4, n8 = ts + 14, ts + 12, ts + 8
    s2[0:n2, :] = zext[pl.ds(HALO - 14, n2), :] + zext[pl.ds(HALO - 15, n2), :]
    s4[0:n4, :] = s2[pl.ds(2, n4), :] + s2[pl.ds(0, n4), :]
    s8[0:n8, :] = s4[pl.ds(4, n8), :] + s4[pl.ds(0, n8), :]
    w2 = s2[pl.ds(14, ts), :]
    w4 = s4[pl.ds(12, ts), :]
    w8 = s8[pl.ds(8, ts), :]
    w16 = w8 + s8[pl.ds(0, ts), :]
    tpos = (i * ts + lax.broadcasted_iota(i32, (ts, 1), 0) + 1).astype(f32)
    grp = lane // (gw // len(POOL_WINDOWS))
    pooled = jnp.zeros((ts, gw), f32)
    for gi, (w, sw) in enumerate(zip(POOL_WINDOWS, (w2, w4, w8, w16))):
        pooled = jnp.where(grp == gi, sw / jnp.minimum(tpos, float(w)), pooled)
    pooled = pooled - z
    yp = jnp.dot(pooled.astype(bf16), pw_ref[...], preferred_element_type=f32) * psc_ref[...]
    yp_ref[0] = _group_rms(yp, mg_ref[:, 2 * gw:3 * gw]).astype(bf16)

    zz = 0.5 * zz * (1.0 + lax.erf(zz * (1.0 / math.sqrt(2.0))))
    u = zz[:, :gw]
    v = zz[:, gw:]
    vm = jnp.mean(v, axis=-1, keepdims=True)
    vc = v - vm
    vv = jnp.mean(vc * vc, axis=-1, keepdims=True)
    vn = (vc * lax.rsqrt(vv + EPS) * sg_ref[...] + sb_ref[...]).astype(bf16)
    li = lax.broadcasted_iota(i32, (SGU_CHUNK, SGU_CHUNK), 0)
    lj = lax.broadcasted_iota(i32, (SGU_CHUNK, SGU_CHUNK), 1)
    head_of_lane = lane // HEAD_DIM
    wts = [jnp.where(li >= lj, sw_ref[h], 0.0).astype(bf16) for h in range(HEADS)]
    outs = []
    for n in range(ts // SGU_CHUNK):
        vch = vn[n * SGU_CHUNK:(n + 1) * SGU_CHUNK]
        mixed = sbias_ref[...]
        for h in range(HEADS):
            mh = jnp.dot(wts[h], vch, preferred_element_type=f32)
            mixed = mixed + jnp.where(head_of_lane == h, mh, 0.0)
        outs.append(u[n * SGU_CHUNK:(n + 1) * SGU_CHUNK] * mixed)
    ysgu = jnp.concatenate(outs, axis=0)
    ys_ref[0] = _group_rms(ysgu, mg_ref[:, 3 * gw:4 * gw]).astype(bf16)


def _project_and_mix(x, shift, scale, g, w_bf, cw, cb, cng, cnb, pw_bd, psc, sg, sb, sw, sbias, mg):
    B, S, D = x.shape
    gw = GROUP_WIDTH
    ts = SEQ_TILE
    nt = S // ts
    row = lambda b, i: (b, i, 0)
    col = lambda b, i: (b, 0, i)
    vec = lambda b, i: (b, 0, 0)
    full2 = lambda b, i: (0, 0)
    full3 = lambda b, i: (0, 0, 0)
    mixed = jax.ShapeDtypeStruct((B, S, gw), bf16)
    return pl.pallas_call(
        _project_mix_kernel,
        grid=(B, nt),
        in_specs=[pl.BlockSpec((1, ts, D), row),
                  pl.BlockSpec((1, 1, D), vec),
                  pl.BlockSpec((1, 1, D), vec),
                  pl.BlockSpec((1, D), full2),
                  pl.BlockSpec(w_bf.shape, full2),
                  pl.BlockSpec(cw.shape, full2), pl.BlockSpec(cb.shape, full2),
                  pl.BlockSpec(cng.shape, full2), pl.BlockSpec(cnb.shape, full2),
                  pl.BlockSpec(pw_bd.shape, full2), pl.BlockSpec(psc.shape, full2),
                  pl.BlockSpec(sg.shape, full2), pl.BlockSpec(sb.shape, full2),
                  pl.BlockSpec(sw.shape, full3), pl.BlockSpec(sbias.shape, full2),
                  pl.BlockSpec(mg.shape, full2)],
        out_specs=[pl.BlockSpec((1, gw, ts), col),
                   pl.BlockSpec((1, HEADS, ts, HEAD_DIM), lambda b, i: (b, 0, i, 0)),
                   pl.BlockSpec((1, ts // MOBA_BLOCK, HEADS, V_ROWS, MOBA_BLOCK), lambda b, i: (b, i, 0, 0, 0)),
                   pl.BlockSpec((1, 1, ts // MOBA_BLOCK, gw), lambda b, i: (b, i, 0, 0)),
                   pl.BlockSpec((1, ts, gw), row), pl.BlockSpec((1, ts, gw), row), pl.BlockSpec((1, ts, gw), row)],
        out_shape=[jax.ShapeDtypeStruct((B, gw, S), bf16),
                   jax.ShapeDtypeStruct((B, HEADS, S, HEAD_DIM), bf16),
                   jax.ShapeDtypeStruct((B, S // MOBA_BLOCK, HEADS, V_ROWS, MOBA_BLOCK), bf16),
                   jax.ShapeDtypeStruct((B, nt, ts // MOBA_BLOCK, gw), f32),
                   mixed, mixed, mixed],
        scratch_shapes=[pltpu.VMEM((ts + HALO, gw), f32), pltpu.VMEM((SUBLANES - 1, ts + HALO, gw), f32),
                        pltpu.VMEM((ts + HALO, gw), f32),
                        pltpu.VMEM((ts + 16, gw), f32), pltpu.VMEM((ts + 16, gw), f32),
                        pltpu.VMEM((ts + 16, gw), f32)],
        compiler_params=_cparams(("arbitrary", "arbitrary")),
        name="project_and_mix",
    )(x, shift, scale, g, w_bf, cw, cb, cng, cnb, pw_bd, psc, sg, sb, sw, sbias, mg)


def _t5_bucket_table(max_dist):
    d = np.arange(max_dist, dtype=np.int64)
    max_exact = N_BUCKETS // 2
    nf = np.maximum(d, 1).astype(np.float32)
    large = max_exact + (np.log(nf / np.float32(max_exact)) / np.float32(math.log(T5_MAX_DISTANCE / max_exact))
                         * np.float32(N_BUCKETS - max_exact)).astype(np.int32)
    large = np.minimum(large, N_BUCKETS - 1)
    return np.where(d < max_exact, d, large).astype(np.int32)


_TILE_BASES = (0, MOBA_BLOCK)


def _bias_kernel(tab_ref, o_ref):
    blk, qc = MOBA_BLOCK, MOBA_BLOCK
    table = _t5_bucket_table(2 * blk + qc)
    first = [int(np.argmax(table >= b)) for b in range(N_BUCKETS)]
    j = lax.broadcasted_iota(i32, (blk, qc), 0)
    q = lax.broadcasted_iota(i32, (blk, qc), 1)
    for t, base in enumerate(_TILE_BASES):
        d = base + q - j
        lo, hi = max(base - (blk - 1), 0), base + qc - 1
        for h in range(HEADS):
            val = jnp.full((blk, qc), tab_ref[h], f32)
            for b in range(1, N_BUCKETS):
                if first[b] > hi:
                    continue
                if first[b] <= lo:
                    val = jnp.full((blk, qc), tab_ref[b * HEADS + h], f32)
                else:
                    val = jnp.where(d >= first[b], tab_ref[b * HEADS + h], val)
            o_ref[t, h] = jnp.where(d >= 0, val * LOG2E, NEG)


def _bias_tiles(rel_bias):
    return pl.pallas_call(
        _bias_kernel,
        in_specs=[pl.BlockSpec(memory_space=pltpu.SMEM)],
        out_shape=jax.ShapeDtypeStruct((len(_TILE_BASES), HEADS, MOBA_BLOCK, MOBA_BLOCK), f32),
        name="bias_tiles",
    )(rel_bias.astype(f32).reshape(-1))


def _attn_kernel(far_ref, qT_ref, k_ref, vT_ref, km_ref, bias_ref, mg_ref, o_ref, mask_ref, sa_ref, sb_ref):
    own = pl.program_id(1)
    blk, gw, hd = MOBA_BLOCK, GROUP_WIDTH, HEAD_DIM
    nb = km_ref.shape[1]

    nio = lax.broadcasted_iota(i32, (nb, blk), 0)
    past = nio < own
    km = km_ref[0]
    q_heads = []
    for h in range(HEADS):
        qh = qT_ref[0, h * hd:(h + 1) * hd, :]
        q_heads.append(qh)
        km_hi, km_lo = _split_bf16(km[:, h * hd:(h + 1) * hd])
        gate = (jnp.dot(km_hi, qh, preferred_element_type=f32)
                + jnp.dot(km_lo, qh, preferred_element_type=f32))
        gate = jnp.where(past, gate, -jnp.inf)
        picked = jnp.zeros((nb, blk), jnp.bool_)
        for _ in range(MOBA_TOPK):
            top = jnp.max(gate, axis=0, keepdims=True)
            first = jnp.min(jnp.where(gate == top, nio, nb), axis=0, keepdims=True)
            hit = nio == first
            picked = picked | hit
            gate = jnp.where(hit, -jnp.inf, gate)
        sel = picked & past
        mask_ref[0, h] = jnp.where(sel, 0.0, NEG)
        mask_ref[1, h] = jnp.where(sel & (nio < own - 1), far_ref[h] * LOG2E, NEG)

    def qk(n, h):
        kb = k_ref[0, h, pl.ds(pl.multiple_of(n * blk, blk), blk), :]
        return jnp.dot(kb, q_heads[h], preferred_element_type=f32)

    def far_scores(n, h):
        return qk(n, h) + mask_ref[1, h, pl.ds(n, 1), :]

    lane_groups = blk // ATTN_QUERY_GROUP

    def update(state, scores, n):
        out = []
        for h in range(HEADS):
            vb = vT_ref[0, n, h]
            for c in range(lane_groups):
                u = h * lane_groups + c
                m, acc = state[2 * u:2 * u + 2]
                s = scores[h][:, c * ATTN_QUERY_GROUP:(c + 1) * ATTN_QUERY_GROUP]
                m_new = jnp.maximum(m, jnp.max(s, axis=0, keepdims=True))
                alpha = jnp.exp2(m - m_new)
                p = jnp.exp2(s - m_new)
                acc = acc * alpha + jnp.dot(vb, p.astype(bf16), preferred_element_type=f32)
                out += [m_new, acc]
        return tuple(out)

    adj = jnp.maximum(own - 1, 0)
    n_far = jnp.maximum(own - 1, 0)
    s_own = [qk(own, h) + bias_ref[0, h] for h in range(HEADS)]
    s_adj = [qk(adj, h) + bias_ref[1, h] + mask_ref[0, h, pl.ds(adj, 1), :] for h in range(HEADS)]
    for h in range(HEADS):
        sa_ref[h] = far_scores(0, h)

    state = []
    for h in range(HEADS):
        vb = jnp.concatenate([vT_ref[0, own, h], vT_ref[0, adj, h]], axis=1)
        for c in range(lane_groups):
            so = s_own[h][:, c * ATTN_QUERY_GROUP:(c + 1) * ATTN_QUERY_GROUP]
            sj = s_adj[h][:, c * ATTN_QUERY_GROUP:(c + 1) * ATTN_QUERY_GROUP]
            m0 = jnp.maximum(jnp.max(so, axis=0, keepdims=True), jnp.max(sj, axis=0, keepdims=True))
            p = jnp.concatenate([jnp.exp2(so - m0).astype(bf16), jnp.exp2(sj - m0).astype(bf16)], axis=0)
            state += [m0, jnp.dot(vb, p, preferred_element_type=f32)]
    state = tuple(state)

    def body(i, state):
        first = jnp.minimum(2 * i, nb - 1)
        second = jnp.minimum(2 * i + 1, nb - 1)
        third = jnp.minimum(2 * i + 2, nb - 1)
        for h in range(HEADS):
            sb_ref[h] = far_scores(second, h)
        state = update(state, [sa_ref[h] for h in range(HEADS)], first)
        for h in range(HEADS):
            sa_ref[h] = far_scores(third, h)
        return update(state, [sb_ref[h] for h in range(HEADS)], second)

    fin = lax.fori_loop(0, (n_far + 1) // 2, body, state)

    def normalised(acc):
        return acc[0:hd, :] / acc[hd:hd + 1, :]

    outT = jnp.concatenate(
        [jnp.concatenate([normalised(fin[2 * (h * lane_groups + c) + 1]) for c in range(lane_groups)], axis=1)
         for h in range(HEADS)], axis=0)
    o_ref[0] = _group_rms(outT.T, mg_ref[:, gw:2 * gw]).astype(bf16)


def _moba_attention(qT, k, vT, kmean, bias_tiles, far_bias, mg):
    B, _, S, _ = k.shape
    gw, blk = GROUP_WIDTH, MOBA_BLOCK
    nb = kmean.shape[1]
    return pl.pallas_call(
        _attn_kernel,
        grid=(B, nb),
        in_specs=[pl.BlockSpec(memory_space=pltpu.SMEM),
                  pl.BlockSpec((1, gw, blk), lambda b, c: (b, 0, c)),
                  pl.BlockSpec((1, HEADS, S, HEAD_DIM), lambda b, c: (b, 0, 0, 0)),
                  pl.BlockSpec((1, nb, HEADS, V_ROWS, blk), lambda b, c: (b, 0, 0, 0, 0)),
                  pl.BlockSpec((1, nb, gw), lambda b, c: (b, 0, 0)),
                  pl.BlockSpec(bias_tiles.shape, lambda b, c: (0, 0, 0, 0)),
                  pl.BlockSpec(mg.shape, lambda b, c: (0, 0))],
        out_specs=pl.BlockSpec((1, blk, gw), lambda b, c: (b, c, 0)),
        out_shape=jax.ShapeDtypeStruct((B, S, gw), bf16),
        scratch_shapes=[pltpu.VMEM((2, HEADS, nb, blk), f32),
                        pltpu.VMEM((HEADS, blk, blk), f32), pltpu.VMEM((HEADS, blk, blk), f32)],
        compiler_params=_cparams(("arbitrary", "arbitrary")),
        name="moba_attention",
    )(far_bias, qT, k, vT, kmean, bias_tiles, mg)


def _outproj_router_kernel(yc_ref, ya_ref, yp_ref, ys_ref, x_ref, g1_ref, wo_ref, n2_ref, sh_ref, sc_ref,
                           rw_ref, rb_ref,
                           x1_ref, h2_ref, dest_ref, gate_ref, pad_ref):
    W = ROUTE_TILE
    windows = range(ROUTE_WINDOWS_PER_STEP)
    rows = [slice(u * W, (u + 1) * W) for u in windows]
    mixed = [jnp.concatenate([r[rows[u], :] for r in (yc_ref, ya_ref, yp_ref, ys_ref)], axis=1) for u in windows]
    projected = [jnp.dot(m, wo_ref[...], preferred_element_type=f32) for m in mixed]

    his, los = [], []
    for u in windows:
        x1 = x_ref[rows[u], :] + g1_ref[0] * projected[u]
        x1_ref[rows[u], :] = x1
        ms = jnp.mean(x1 * x1, axis=-1, keepdims=True)
        h = x1 * lax.rsqrt(ms + EPS) * n2_ref[...]
        h = h * (1.0 + sc_ref[0]) + sh_ref[0]
        h_hi, h_lo = _split_bf16(h)
        h2_ref[rows[u], :] = h_hi
        his.append(h_hi)
        los.append(h_lo)

    nt = (((1,), (1,)), ((), ()))
    rw_hi, rw_lo = _split_bf16(rw_ref[...])
    logits = [(lax.dot_general(rw_hi, his[u], nt, preferred_element_type=f32)
               + lax.dot_general(rw_hi, los[u], nt, preferred_element_type=f32)
               + lax.dot_general(rw_lo, his[u], nt, preferred_element_type=f32)) + rb_ref[...] for u in windows]

    eio = lax.broadcasted_iota(i32, (N_EXPERTS, W), 0)
    sels, multis = [], []
    for u in windows:
        work = logits[u]
        vals, sel_u = [], []
        for k in range(TOP_K):
            m = jnp.max(work, axis=0, keepdims=True)
            idx = jnp.min(jnp.where(work == m, eio, N_EXPERTS), axis=0, keepdims=True)
            sel = eio == idx
            vals.append(m)
            sel_u.append(sel)
            work = jnp.where(sel, -jnp.inf, work)
        exps = [jnp.exp(v - vals[0]) for v in vals]
        denom = exps[0] + exps[1] + exps[2] + exps[3]
        for k in range(TOP_K):
            gate_ref[k:k + 1, rows[u]] = exps[k] / denom
        multi = jnp.zeros((N_EXPERTS, W), f32)
        for sel in sel_u:
            multi = multi + sel.astype(f32)
        sels.append(sel_u)
        multis.append(multi)

    before = (lax.broadcasted_iota(i32, (W, W), 0) < lax.broadcasted_iota(i32, (W, W), 1)).astype(bf16)
    earlier = [jnp.dot(multis[u].astype(bf16), before, preferred_element_type=f32) for u in windows]

    lower = (lax.broadcasted_iota(i32, (N_EXPERTS, N_EXPERTS), 1)
             < lax.broadcasted_iota(i32, (N_EXPERTS, N_EXPERTS), 0)).astype(bf16)
    seg_start = []
    for u in windows:
        cnt = jnp.sum(multis[u], axis=1, keepdims=True).astype(i32)
        padded = (cnt + (ROW_CHUNK - 1)) // ROW_CHUNK * ROW_CHUNK
        pad_ref[u] = padded
        seg_start.append(jnp.dot(lower, jnp.broadcast_to(padded.astype(f32), (N_EXPERTS, W)).astype(bf16),
                                 preferred_element_type=f32))
    for u in windows:
        row = seg_start[u] + earlier[u]
        for k in range(TOP_K):
            dest_ref[k:k + 1, rows[u]] = jnp.sum(jnp.where(sels[u][k], row, 0.0), axis=0,
                                                 keepdims=True).astype(i32)


def _outproj_router(yc, ya, yp, ys, x, gate1, wo_bf, n2g, shift2, scale2, rwT, rb, seq):
    N, D = x.shape
    gw = GROUP_WIDTH
    nw = N // ROUTE_TILE
    wps = ROUTE_WINDOWS_PER_STEP
    W = ROUTE_TILE * wps
    per_b = seq // W
    row = lambda i: (i, 0)
    vec = lambda i: (i // per_b, 0, 0)
    full = lambda i: (0, 0)
    colblk = lambda i: (0, i)
    return pl.pallas_call(
        _outproj_router_kernel,
        grid=(nw // wps,),
        in_specs=[pl.BlockSpec((W, gw), row)] * 4 + [
            pl.BlockSpec((W, D), row),
            pl.BlockSpec((1, 1, D), vec),
            pl.BlockSpec(wo_bf.shape, full),
            pl.BlockSpec((1, D), full),
            pl.BlockSpec((1, 1, D), vec),
            pl.BlockSpec((1, 1, D), vec),
            pl.BlockSpec(rwT.shape, full),
            pl.BlockSpec(rb.shape, full)],
        out_specs=[pl.BlockSpec((W, D), row),
                   pl.BlockSpec((W, D), row),
                   pl.BlockSpec((TOP_K, W), colblk),
                   pl.BlockSpec((TOP_K, W), colblk),
                   pl.BlockSpec((wps, N_EXPERTS, 1), lambda i: (i, 0, 0))],
        out_shape=[jax.ShapeDtypeStruct((N, D), f32),
                   jax.ShapeDtypeStruct((N, D), bf16),
                   jax.ShapeDtypeStruct((TOP_K, N), i32),
                   jax.ShapeDtypeStruct((TOP_K, N), f32),
                   jax.ShapeDtypeStruct((nw, N_EXPERTS, 1), i32)],
        compiler_params=_cparams(("arbitrary",)),
        name="outproj_router",
    )(yc, ya, yp, ys, x, gate1, wo_bf, n2g, shift2, scale2, rwT, rb)


def _max_window_rows():
    return -(-(ROUTE_TILE * TOP_K + N_EXPERTS * (ROW_CHUNK - 1)) // 128) * 128


def _pack_pairs(lo, hi):
    lo_bits = lax.bitcast_convert_type(lo, jnp.uint32)
    hi_bits = lax.bitcast_convert_type(hi, jnp.uint32)
    return lax.bitcast_convert_type(hi_bits | (lo_bits >> 16), i32)


def _unpack_pairs(words):
    bits = lax.bitcast_convert_type(words, jnp.uint32)
    lo = lax.bitcast_convert_type(bits << 16, f32).astype(bf16)
    hi = lax.bitcast_convert_type(bits & jnp.uint32(0xFFFF0000), f32).astype(bf16)
    return lo, hi


def _dispatch_kernel(piece_src_s, piece_dst_s, wcopies_s, tail_start_s, tail_chunks_s, n_used_s,
                     h_ref, dest_ref, xs_ref, sorted_ref, zero_ref, sem):
    w = pl.program_id(0)
    nw = pl.num_programs(0)
    W = ROUTE_TILE
    R = sorted_ref.shape[1]
    half = sorted_ref.shape[2]
    slot = w % 2
    rio = lax.broadcasted_iota(i32, (R, W), 0)
    hit = rio == dest_ref[0:1, :]
    for k in range(1, TOP_K):
        hit = hit | (rio == dest_ref[k:k + 1, :])
    onehot = jnp.where(hit, 1.0, 0.0).astype(bf16)
    rows = jnp.dot(onehot, h_ref[...], preferred_element_type=f32)
    sorted_ref[slot] = _pack_pairs(rows[:, :half], rows[:, half:])

    def piece_copy(s, src_row, dst_row):
        return pltpu.make_async_copy(
            sorted_ref.at[s, pl.ds(pl.multiple_of(src_row, ROW_CHUNK), DISPATCH_COPY_ROWS), :],
            xs_ref.at[pl.ds(pl.multiple_of(dst_row, ROW_CHUNK), DISPATCH_COPY_ROWS), :], sem.at[s])

    def zero_copy(dst_row):
        return pltpu.make_async_copy(
            zero_ref.at[0:ROW_CHUNK, :],
            xs_ref.at[pl.ds(pl.multiple_of(dst_row, ROW_CHUNK), ROW_CHUNK), :], sem.at[0])

    def zero_tile_copy(tile):
        return pltpu.make_async_copy(
            zero_ref, xs_ref.at[pl.ds(pl.multiple_of(tile * FFN_TILE, FFN_TILE), FFN_TILE), :], sem.at[0])

    def drain(s, count):
        @pl.when(count > 0)
        def _():
            rows = pl.multiple_of(count * DISPATCH_COPY_ROWS, DISPATCH_COPY_ROWS)
            pltpu.make_async_copy(xs_ref.at[pl.ds(0, rows), :], xs_ref.at[pl.ds(0, rows), :], sem.at[s]).wait()

    @pl.when(w > 0)
    def _():
        drain(1 - slot, wcopies_s[jnp.maximum(w - 1, 0)])

    for parity in range(2):
        @pl.when(slot == parity)
        def _():
            def issue(q, carry):
                piece_copy(parity, piece_src_s[w * MAX_DISPATCH_PIECES + q],
                           piece_dst_s[w * MAX_DISPATCH_PIECES + q]).start()
                return carry
            lax.fori_loop(0, wcopies_s[w], issue, 0)

    @pl.when(w == nw - 1)
    def _():
        drain(slot, wcopies_s[w])
        zero_ref[...] = jnp.zeros(zero_ref.shape, i32)

        def per_tail(e, total):
            n = tail_chunks_s[e]
            dst = tail_start_s[e]

            def issue(j, c):
                zero_copy(dst + j * ROW_CHUNK).start()
                return c
            lax.fori_loop(0, n, issue, 0)
            return total + n
        tails = lax.fori_loop(0, N_EXPERTS, per_tail, 0)

        def drain_tail(j, c):
            zero_copy(0).wait()
            return c
        lax.fori_loop(0, tails, drain_tail, 0)

        n_tiles = xs_ref.shape[0] // FFN_TILE

        def issue_tile(j, c):
            zero_tile_copy(j).start()
            return c
        lax.fori_loop(n_used_s[0], n_tiles, issue_tile, 0)

        def drain_tile(j, c):
            zero_tile_copy(0).wait()
            return c
        lax.fori_loop(n_used_s[0], n_tiles, drain_tile, 0)


def _dispatch(h2, destT, t, p_rows):
    N, D = h2.shape
    W = ROUTE_TILE
    nw = N // W
    R = _max_window_rows()
    assert R >= W * TOP_K + N_EXPERTS * (ROW_CHUNK - 1) + DISPATCH_COPY_ROWS - ROW_CHUNK
    grid_spec = pltpu.PrefetchScalarGridSpec(
        num_scalar_prefetch=6,
        grid=(nw,),
        in_specs=[pl.BlockSpec((W, D), lambda w, *_: (w, 0)),
                  pl.BlockSpec((TOP_K, W), lambda w, *_: (0, w))],
        out_specs=pl.BlockSpec(memory_space=pl.ANY),
        scratch_shapes=[pltpu.VMEM((2, R, D // 2), i32), pltpu.VMEM((FFN_TILE, D // 2), i32),
                        pltpu.SemaphoreType.DMA((2,))],
    )
    return pl.pallas_call(
        _dispatch_kernel,
        grid_spec=grid_spec,
        out_shape=jax.ShapeDtypeStruct((p_rows, D // 2), i32),
        compiler_params=_cparams(("arbitrary",)),
        name="expert_dispatch",
    )(t['piece_src'], t['piece_dst'], t['wcopies'], t['tail_start'], t['tail_chunks'], t['n_used'], h2, destT)


def _ffn_kernel(tile_expert_s, next_expert_s, n_used_s, x_ref, w1_hbm, b1_ref, w2_hbm, b2_ref, y_ref,
                w1f_ref, w2f_ref, w1b_ref, w2b_ref, sem, *, layer):
    i = pl.program_id(0)
    last = n_used_s[0] - 1
    expert = tile_expert_s[jnp.minimum(i, last)]
    prev_expert = tile_expert_s[jnp.minimum(jnp.maximum(i - 1, 0), last)]

    def weight_copies(e):
        return (pltpu.make_async_copy(w1_hbm.at[layer, e], w1f_ref, sem.at[0]),
                pltpu.make_async_copy(w2_hbm.at[layer, e], w2f_ref, sem.at[1]))

    @pl.when(i == 0)
    def _():
        for c in weight_copies(expert):
            c.start()

    @pl.when((i == 0) | (expert != prev_expert))
    def _():
        for c in weight_copies(expert):
            c.wait()
        w1b_ref[...] = w1f_ref[...].astype(bf16)
        w2b_ref[...] = w2f_ref[...].astype(bf16)
        nxt = next_expert_s[expert]

        @pl.when(nxt < N_EXPERTS)
        def _():
            for c in weight_copies(nxt):
                c.start()

    @pl.when(i < n_used_s[0])
    def _():
        dff = w2b_ref.shape[0]
        half = x_ref.shape[1]
        sub = x_ref.shape[0] // FFN_SUBTILES
        hidden = []
        for r in range(FFN_SUBTILES):
            x_lo, x_hi = _unpack_pairs(x_ref[r * sub:(r + 1) * sub, :])
            hidden.append(jnp.dot(x_lo, w1b_ref[0:half, :], preferred_element_type=f32)
                          + jnp.dot(x_hi, w1b_ref[half:2 * half, :], preferred_element_type=f32))
        for r in range(FFN_SUBTILES):
            hh = hidden[r] + b1_ref[0, 0]
            x_glu = jnp.minimum(hh[:, :dff], SWIGLU_LIMIT)
            x_lin = jnp.clip(hh[:, dff:], -SWIGLU_LIMIT, SWIGLU_LIMIT)
            act = x_glu * jax.nn.sigmoid(SWIGLU_ALPHA * x_glu) * (x_lin + 1.0)
            y = jnp.dot(act.astype(bf16), w2b_ref[...], preferred_element_type=f32) + b2_ref[0, 0]
            y = y.astype(bf16).astype(f32)
            y_ref[r * sub:(r + 1) * sub, :] = _pack_pairs(y[:, :half], y[:, half:])

    @pl.when(i >= n_used_s[0])
    def _():
        y_ref[...] = jnp.zeros(y_ref.shape, i32)


def _expert_ffn(xs, w1, b1, w2, b2, t, layer):
    P, half = xs.shape
    L, E, D, F2 = w1.shape
    tm = FFN_TILE
    nt = P // tm

    def tile(i, te, ne, nu):
        return (jnp.minimum(i, nu[0] - 1), 0)

    def expert4(i, te, ne, nu):
        return (layer, te[jnp.minimum(i, nu[0] - 1)], 0, 0)

    grid_spec = pltpu.PrefetchScalarGridSpec(
        num_scalar_prefetch=3,
        grid=(nt,),
        in_specs=[pl.BlockSpec((tm, half), tile),
                  pl.BlockSpec(memory_space=pl.ANY),
                  pl.BlockSpec((1, 1, 1, F2), expert4),
                  pl.BlockSpec(memory_space=pl.ANY),
                  pl.BlockSpec((1, 1, 1, D), expert4)],
        out_specs=pl.BlockSpec((tm, half), lambda i, te, ne, nu: (i, 0)),
        scratch_shapes=[pltpu.VMEM((D, F2), f32), pltpu.VMEM((F2 // 2, D), f32),
                        pltpu.VMEM((D, F2), bf16), pltpu.VMEM((F2 // 2, D), bf16),
                        pltpu.SemaphoreType.DMA((2,))],
    )
    return pl.pallas_call(
        functools.partial(_ffn_kernel, layer=layer),
        grid_spec=grid_spec,
        out_shape=jax.ShapeDtypeStruct((P, half), i32),
        compiler_params=_cparams(("arbitrary",)),
        name="expert_ffn",
    )(t['tile_expert'], t['next_expert'], t['n_used'], xs, w1, b1.reshape(L, E, 1, F2), w2,
      b2.reshape(L, E, 1, D))


def _combine_kernel(big_src_s, big_dst_s, wbig_s, small_src_s, small_dst_s, wsmall_s, wchunks_s,
                    ys_ref, dest_ref, gate_ref, x1_ref, g2_ref, fg_ref, o_ref, local_ref, sem, *, final):
    w = pl.program_id(0)
    nw = pl.num_programs(0)
    W = ROUTE_TILE
    R = local_ref.shape[1]
    slot = w % 2

    def rows_copy(s, src_row, dst_row, rows):
        return pltpu.make_async_copy(
            ys_ref.at[pl.ds(pl.multiple_of(src_row, ROW_CHUNK), rows), :],
            local_ref.at[s, pl.ds(pl.multiple_of(dst_row, ROW_CHUNK), rows), :], sem.at[s])

    def fetch(win, s):
        def big(q, carry):
            rows_copy(s, big_src_s[win * MAX_COMBINE_PIECES + q], big_dst_s[win * MAX_COMBINE_PIECES + q],
                      COMBINE_COPY_ROWS).start()
            return carry
        lax.fori_loop(0, wbig_s[win], big, 0)

        def small(q, carry):
            rows_copy(s, small_src_s[win * MAX_COMBINE_CHUNKS + q], small_dst_s[win * MAX_COMBINE_CHUNKS + q],
                      ROW_CHUNK).start()
            return carry
        lax.fori_loop(0, wsmall_s[win], small, 0)

    @pl.when(w == 0)
    def _():
        local_ref[...] = jnp.zeros(local_ref.shape, i32)
        fetch(0, 0)

    for parity in range(2):
        @pl.when((w + 1 < nw) & (slot == parity))
        def _():
            fetch(jnp.minimum(w + 1, nw - 1), 1 - parity)

    @pl.when(wchunks_s[w] > 0)
    def _():
        rows = pl.multiple_of(wchunks_s[w] * ROW_CHUNK, ROW_CHUNK)
        pltpu.make_async_copy(ys_ref.at[pl.ds(0, rows), :], local_ref.at[slot, pl.ds(0, rows), :],
                              sem.at[slot]).wait()

    rio = lax.broadcasted_iota(i32, (R, W), 0)
    weights = jnp.zeros((R, W), f32)
    for k in range(TOP_K):
        weights = jnp.where(rio == dest_ref[k:k + 1, :], gate_ref[k:k + 1, :], weights)
    weights = weights.astype(bf16)
    y_lo, y_hi = _unpack_pairs(local_ref[slot])
    tn = (((0,), (0,)), ((), ()))
    moe = jnp.concatenate([lax.dot_general(weights, y_lo, tn, preferred_element_type=f32),
                           lax.dot_general(weights, y_hi, tn, preferred_element_type=f32)], axis=1)
    x2 = x1_ref[...] + g2_ref[0] * moe
    if final:
        ms = jnp.mean(x2 * x2, axis=-1, keepdims=True)
        x2 = x2 * lax.rsqrt(ms + EPS) * fg_ref[...]
    o_ref[...] = x2


def _combine(ys, dest, gates, x1, gate2, final_g, t, seq, final):
    N, D = x1.shape
    W = ROUTE_TILE
    nw = N // W
    per_b = seq // W
    R = _max_window_rows()
    grid_spec = pltpu.PrefetchScalarGridSpec(
        num_scalar_prefetch=7,
        grid=(nw,),
        in_specs=[pl.BlockSpec(memory_space=pl.ANY),
                  pl.BlockSpec((TOP_K, W), lambda w, *_: (0, w)),
                  pl.BlockSpec((TOP_K, W), lambda w, *_: (0, w)),
                  pl.BlockSpec((W, D), lambda w, *_: (w, 0)),
                  pl.BlockSpec((1, 1, D), lambda w, *_: (w // per_b, 0, 0)),
                  pl.BlockSpec((1, D), lambda w, *_: (0, 0))],
        out_specs=pl.BlockSpec((W, D), lambda w, *_: (w, 0)),
        scratch_shapes=[pltpu.VMEM((2, R, D // 2), i32), pltpu.SemaphoreType.DMA((2,))],
    )
    return pl.pallas_call(
        functools.partial(_combine_kernel, final=final),
        grid_spec=grid_spec,
        out_shape=jax.ShapeDtypeStruct((N, D), f32),
        compiler_params=_cparams(("arbitrary",)),
        name="expert_combine",
    )(t['big_src'], t['big_dst'], t['wbig'], t['small_src'], t['small_dst'], t['wsmall'], t['wchunks'],
      ys, dest, gates, x1, gate2, final_g)


def _routing_tables(padded, n_tiles):
    nw, E = padded.shape
    lstart = jnp.cumsum(padded, axis=1) - padded
    tot = jnp.sum(padded, axis=0)
    slack = DISPATCH_COPY_ROWS - ROW_CHUNK
    region = jnp.where(tot > 0, (tot + slack + FFN_TILE - 1) // FFN_TILE * FFN_TILE, 0)
    region_end = jnp.cumsum(region)
    region_start = region_end - region
    gbase = region_start[None, :] + jnp.cumsum(padded, axis=0) - padded
    nchunk = padded // ROW_CHUNK
    tail_start = region_start + tot
    tail_chunks = (region - tot) // ROW_CHUNK
    n_used = (region_end[-1] // FFN_TILE).astype(i32).reshape(1)
    tile_row = jnp.arange(n_tiles, dtype=i32) * FFN_TILE
    tile_expert = jnp.minimum(
        jnp.sum((region_end[None, :] <= tile_row[:, None]).astype(i32), axis=1), E - 1).astype(i32)
    eidx = jnp.arange(E, dtype=i32)
    later_nonempty = (eidx[None, :] > eidx[:, None]) & (region[None, :] > 0)
    next_expert = jnp.min(jnp.where(later_nonempty, eidx[None, :], E), axis=1)
    flat = lambda a: a.reshape(-1).astype(i32)

    def copy_list(counts, window_rows, expert_rows, step, max_items):
        ends = jnp.cumsum(counts, axis=1)
        slot = jnp.arange(max_items, dtype=i32)
        owner = jnp.sum((ends[:, None, :] <= slot[None, :, None]).astype(i32), axis=2)
        hit = owner[:, :, None] == eidx[None, None, :]
        pick = lambda a: jnp.sum(jnp.where(hit, a[:, None, :], 0), axis=2)
        offset = step * (slot[None, :] - pick(ends - counts))
        return flat(pick(window_rows) + offset), flat(pick(expert_rows) + offset), flat(ends[:, -1])

    ncopy = (padded + DISPATCH_COPY_ROWS - 1) // DISPATCH_COPY_ROWS
    piece_src, piece_dst, wcopies = copy_list(ncopy, lstart, gbase, DISPATCH_COPY_ROWS, MAX_DISPATCH_PIECES)
    nbig = padded // COMBINE_COPY_ROWS
    big_dst, big_src, wbig = copy_list(nbig, lstart, gbase, COMBINE_COPY_ROWS, MAX_COMBINE_PIECES)
    rest = nbig * COMBINE_COPY_ROWS
    small_dst, small_src, wsmall = copy_list(nchunk - nbig * (COMBINE_COPY_ROWS // ROW_CHUNK),
                                             lstart + rest, gbase + rest, ROW_CHUNK, MAX_COMBINE_CHUNKS)
    return dict(next_expert=flat(next_expert), wchunks=flat(jnp.sum(nchunk, axis=1)),
                piece_src=piece_src, piece_dst=piece_dst, wcopies=wcopies,
                big_src=big_src, big_dst=big_dst, wbig=wbig,
                small_src=small_src, small_dst=small_dst, wsmall=wsmall,
                tail_start=flat(tail_start), tail_chunks=flat(tail_chunks),
                n_used=n_used, tile_expert=tile_expert)


def _block_diag(w):
    g, a, b = w.shape
    out = jnp.zeros((g * a, g * b), w.dtype)
    for i in range(g):
        out = out.at[i * a:(i + 1) * a, i * b:(i + 1) * b].set(w[i])
    return out


def kernel(x, c, w_mod, b_mod, norm1_g, w_in, conv_w, conv_b, conv_norm_g, conv_norm_b, rel_bias, pool_w, pool_scale, sgu_norm_g, sgu_norm_b, sgu_w, sgu_b, mix_out_g, w_out, norm2_g, router_w, router_b, exp_w1, exp_b1, exp_w2, exp_b2, final_norm_g):
    B, S, D = x.shape
    L = w_mod.shape[0]
    N = B * S
    nw = N // ROUTE_TILE
    assert S % SEQ_TILE == 0 and S % MOBA_BLOCK == 0 and N % ROUTE_TILE == 0 and S % ROUTE_TILE == 0
    p_bound = (N * TOP_K + nw * N_EXPERTS * (ROW_CHUNK - 1)
               + N_EXPERTS * (DISPATCH_COPY_ROWS - ROW_CHUNK + FFN_TILE - 1))
    n_tiles = -(-p_bound // FFN_TILE)
    p_rows = n_tiles * FFN_TILE

    mod = _modulation(c, w_mod, b_mod)
    bias_tiles = _bias_tiles(rel_bias)
    far_bucket = int(_t5_bucket_table(MOBA_BLOCK + 2)[MOBA_BLOCK + 1])
    assert far_bucket == int(_t5_bucket_table(S + 1)[S])
    far_bias = rel_bias[far_bucket].astype(f32)
    row = lambda a: a.reshape(1, -1)
    for l in range(L):
        m6 = mod[l].reshape(B, 6, 1, D)
        shift1, scale1, gate1, shift2, scale2, gate2 = (m6[:, j] for j in range(6))
        mg = row(mix_out_g[l])
        qT, k, vT, kmean, yc, yp, ys = _project_and_mix(
            x, shift1, scale1, row(norm1_g[l]), w_in[l].astype(bf16),
            conv_w[l], row(conv_b[l]), row(conv_norm_g[l]), row(conv_norm_b[l]),
            _block_diag(pool_w[l]).astype(bf16), row(pool_scale[l]),
            row(sgu_norm_g[l]), row(sgu_norm_b[l]), sgu_w[l],
            jnp.repeat(sgu_b[l].T, HEAD_DIM, axis=1), mg)
        ya = _moba_attention(qT, k, vT, kmean.reshape(B, -1, GROUP_WIDTH), bias_tiles, far_bias, mg)
        flat = lambda a: a.reshape(N, -1)
        x1, h2, destT, gateT, padded = _outproj_router(
            flat(yc), flat(ya), flat(yp), flat(ys), x.reshape(N, D), gate1, w_out[l].astype(bf16),
            row(norm2_g[l]), shift2, scale2, router_w[l].T, router_b[l].reshape(-1, 1), S)
        t = _routing_tables(padded.reshape(nw, N_EXPERTS), n_tiles)
        xs = _dispatch(h2, destT, t, p_rows)
        ysort = _expert_ffn(xs, exp_w1, exp_b1, exp_w2, exp_b2, t, l)
        x = _combine(ysort, destT, gateT, x1, gate2, row(final_norm_g), t, S,
                     final=(l == L - 1)).reshape(B, S, D)
    return x
```

```python
import functools
import math

import numpy as np
import jax
import jax.numpy as jnp
from jax import lax
from jax.experimental import pallas as pl
from jax.experimental.pallas import tpu as pltpu

f32, bf16, i32 = jnp.float32, jnp.bfloat16, jnp.int32

GROUP_WIDTH = 256
HEADS = 4
HEAD_DIM = 64
V_ROWS = HEAD_DIM + 16
LOG2E = math.log2(math.e)
CONV_WIDTH = 31
MOBA_BLOCK = 256
MOBA_TOPK = 3
Q_CHUNK = 128
N_BUCKETS = 32
T5_MAX_DISTANCE = 128
POOL_WINDOWS = (2, 4, 8, 16)
SGU_CHUNK = 128
N_EXPERTS = 32
TOP_K = 4
SWIGLU_LIMIT = 7.0
SWIGLU_ALPHA = 1.702
EPS = 1e-6

SUBLANES = 8
LANES = 128
ATTN_QUERY_GROUP = 256
HALO = 32
SEQ_TILE = 512
ROUTE_TILE = 256
ROUTE_WINDOWS_PER_STEP = 4
ROW_CHUNK = 8
DISPATCH_COPY_ROWS = 32
DISPATCH_ROW_BLOCK = 256
COMBINE_COPY_ROWS = 32
COMBINE_WINDOWS_PER_STEP = 2
MAX_DISPATCH_PIECES = ROUTE_TILE * TOP_K // DISPATCH_COPY_ROWS + N_EXPERTS
MAX_COMBINE_PIECES = (ROUTE_TILE * TOP_K + N_EXPERTS * (ROW_CHUNK - 1)) // COMBINE_COPY_ROWS
MAX_COMBINE_CHUNKS = N_EXPERTS * (COMBINE_COPY_ROWS // ROW_CHUNK - 1)
FFN_TILE = 512
FFN_SUBTILES = 2
NEG = -1e30
VMEM_LIMIT = 56 * 1024 * 1024


def _cparams(sem):
    return pltpu.CompilerParams(dimension_semantics=sem, vmem_limit_bytes=VMEM_LIMIT)


def _split_bf16(a):
    hi = a.astype(bf16)
    lo = (a - hi.astype(f32)).astype(bf16)
    return hi, lo


def _mod_kernel(c_ref, w_ref, b_ref, o_ref):
    c = c_ref[...]
    cond = c * jax.nn.sigmoid(c)
    o_ref[0] = jnp.dot(cond, w_ref[0], preferred_element_type=f32,
                       precision=lax.Precision.HIGHEST) + b_ref[0]


def _modulation(c, w_mod, b_mod):
    L, D, M = w_mod.shape
    B = c.shape[0]
    tn = 1536
    return pl.pallas_call(
        _mod_kernel,
        grid=(L, M // tn),
        in_specs=[pl.BlockSpec((B, D), lambda l, j: (0, 0)),
                  pl.BlockSpec((1, D, tn), lambda l, j: (l, 0, j)),
                  pl.BlockSpec((1, 1, tn), lambda l, j: (l, 0, j))],
        out_specs=pl.BlockSpec((1, B, tn), lambda l, j: (l, 0, j)),
        out_shape=jax.ShapeDtypeStruct((L, B, M), f32),
        compiler_params=_cparams(("arbitrary", "arbitrary")),
        name="modulation",
    )(c, w_mod, b_mod.reshape(L, 1, M))


def _project_mix_kernel(x_ref, sh_ref, sc_ref, g_ref, w_ref,
                        cw_ref, cb_ref, cng_ref, cnb_ref, pw_ref, psc_ref, sg_ref, sb_ref, sw_ref, sbias_ref, mg_ref,
                        qT_ref, k_ref, vT_ref, km_ref, yc_ref, yp_ref, ys_ref,
                        gext, gshift, zext, s2, s4, s8):
    first = pl.program_id(1) == 0

    @pl.when(first)
    def _():
        gext[0:HALO, :] = jnp.zeros((HALO, GROUP_WIDTH), f32)
        zext[0:HALO, :] = jnp.zeros((HALO, GROUP_WIDTH), f32)

    @pl.when(jnp.logical_not(first))
    def _():
        gext[0:HALO, :] = gext[SEQ_TILE:SEQ_TILE + HALO, :]
        zext[0:HALO, :] = zext[SEQ_TILE:SEQ_TILE + HALO, :]

    x = x_ref[0]
    ms = jnp.mean(x * x, axis=-1, keepdims=True)
    h = x * lax.rsqrt(ms + EPS) * g_ref[...]
    h = h * (1.0 + sc_ref[0]) + sh_ref[0]
    gw = GROUP_WIDTH
    hb = h.astype(bf16)
    pc = jnp.dot(hb, w_ref[:, 0:2 * gw], preferred_element_type=f32)
    qkv = jnp.dot(hb, w_ref[:, 2 * gw:5 * gw], preferred_element_type=f32)
    rest = jnp.dot(hb, w_ref[:, 5 * gw:8 * gw], preferred_element_type=f32)
    _local_mixers(pc, rest[:, 0:gw], rest[:, gw:3 * gw],
                  cw_ref, cb_ref, cng_ref, cnb_ref, pw_ref, psc_ref, sg_ref, sb_ref, sw_ref, sbias_ref, mg_ref,
                  yc_ref, yp_ref, ys_ref, gext, gshift, zext, s2, s4, s8)
    proj = jnp.concatenate([pc, qkv], axis=1)
    q = proj[:, 2 * gw:3 * gw] * (HEAD_DIM ** -0.5 * LOG2E)
    qT_ref[0] = q.T.astype(bf16)
    kk = proj[:, 3 * gw:4 * gw]
    for h in range(HEADS):
        k_ref[0, h] = kk[:, h * HEAD_DIM:(h + 1) * HEAD_DIM].astype(bf16)
    for j in range(SEQ_TILE // MOBA_BLOCK):
        km_ref[0, 0, j:j + 1, :] = jnp.mean(kk[j * MOBA_BLOCK:(j + 1) * MOBA_BLOCK], axis=0, keepdims=True)
    for j in range(SEQ_TILE // MOBA_BLOCK):
        vT = proj[j * MOBA_BLOCK:(j + 1) * MOBA_BLOCK, 4 * gw:5 * gw].T.astype(bf16)
        for h in range(HEADS):
            vT_ref[0, j, h] = jnp.concatenate(
                [vT[h * HEAD_DIM:(h + 1) * HEAD_DIM, :], jnp.ones((V_ROWS - HEAD_DIM, MOBA_BLOCK), bf16)], axis=0)


def _group_rms(y, g):
    return y * lax.rsqrt(jnp.mean(y * y, axis=-1, keepdims=True) + EPS) * g


def _local_mixers(pc, z, zz,
                  cw_ref, cb_ref, cng_ref, cnb_ref, pw_ref, psc_ref,
                  sg_ref, sb_ref, sw_ref, sbias_ref, mg_ref,
                  yc_ref, yp_ref, ys_ref,
                  gext, gshift, zext, s2, s4, s8):
    i = pl.program_id(1)
    ts = SEQ_TILE
    gw = GROUP_WIDTH
    lane = lax.broadcasted_iota(i32, (1, gw), 1)

    g = pc[:, :gw] * jax.nn.sigmoid(pc[:, gw:])
    gext[HALO:HALO + ts, :] = g
    span = ts + HALO - SUBLANES
    for a in range(1, SUBLANES):
        gshift[a - 1, 0:span, :] = gext[pl.ds(a, span), :]
    acc = jnp.zeros((ts, gw), f32)
    for j in range(CONV_WIDTH):
        b, a = divmod(HALO - (CONV_WIDTH - 1) + j, SUBLANES)
        tap = gext[pl.ds(SUBLANES * b, ts), :] if a == 0 else gshift[a - 1, pl.ds(SUBLANES * b, ts), :]
        acc = acc + cw_ref[j:j + 1, :] * tap
    y = acc + cb_ref[...]
    r = lax.broadcasted_iota(i32, (gw, gw), 0) // HEAD_DIM
    c = lax.broadcasted_iota(i32, (gw, gw), 1) // HEAD_DIM
    avg = jnp.where(r == c, 1.0 / HEAD_DIM, 0.0).astype(bf16)

    def head_mean(t):
        hi, lo = _split_bf16(t)
        return (jnp.dot(hi, avg, preferred_element_type=f32)
                + jnp.dot(lo, avg, preferred_element_type=f32))

    mu = head_mean(y)
    yc = y - mu
    var = head_mean(yc * yc)
    yn = yc * lax.rsqrt(var + EPS) * cng_ref[...] + cnb_ref[...]
    yconv = yn * jax.nn.sigmoid(yn)
    yc_ref[0] = _group_rms(yconv, mg_ref[:, 0:gw]).astype(bf16)

    zext[HALO:HALO + ts, :] = z
    n2, n4, n8 = ts + 14, ts + 12, ts + 8
    s2[0:n2, :] = zext[pl.ds(HALO - 14, n2), :] + zext[pl.ds(HALO - 15, n2), :]
    s4[0:n4, :] = s2[pl.ds(2, n4), :] + s2[pl.ds(0, n4), :]
    s8[0:n8, :] = s4[pl.ds(4, n8), :] + s4[pl.ds(0, n8), :]
    w2 = s2[pl.ds(14, ts), :]
    w4 = s4[pl.ds(12, ts), :]
    w8 = s8[pl.ds(8, ts), :]
    w16 = w8 + s8[pl.ds(0, ts), :]
    tpos = (i * ts + lax.broadcasted_iota(i32, (ts, 1), 0) + 1).astype(f32)
    grp = lane // (gw // len(POOL_WINDOWS))
    pooled = jnp.zeros((ts, gw), f32)
    for gi, (w, sw) in enumerate(zip(POOL_WINDOWS, (w2, w4, w8, w16))):
        pooled = jnp.where(grp == gi, sw / jnp.minimum(tpos, float(w)), pooled)
    pooled = pooled - z
    yp = jnp.dot(pooled.astype(bf16), pw_ref[...], preferred_element_type=f32) * psc_ref[...]
    yp_ref[0] = _group_rms(yp, mg_ref[:, 2 * gw:3 * gw]).astype(bf16)

    zz = 0.5 * zz * (1.0 + lax.erf(zz * (1.0 / math.sqrt(2.0))))
    u = zz[:, :gw]
    v = zz[:, gw:]
    vm = jnp.mean(v, axis=-1, keepdims=True)
    vc = v - vm
    vv = jnp.mean(vc * vc, axis=-1, keepdims=True)
    vn = (vc * lax.rsqrt(vv + EPS) * sg_ref[...] + sb_ref[...]).astype(bf16)
    li = lax.broadcasted_iota(i32, (SGU_CHUNK, SGU_CHUNK), 0)
    lj = lax.broadcasted_iota(i32, (SGU_CHUNK, SGU_CHUNK), 1)
    head_of_lane = lane // HEAD_DIM
    wts = [jnp.where(li >= lj, sw_ref[h], 0.0).astype(bf16) for h in range(HEADS)]
    outs = []
    for n in range(ts // SGU_CHUNK):
        vch = vn[n * SGU_CHUNK:(n + 1) * SGU_CHUNK]
        mixed = sbias_ref[...]
        for h in range(HEADS):
            mh = jnp.dot(wts[h], vch, preferred_element_type=f32)
            mixed = mixed + jnp.where(head_of_lane == h, mh, 0.0)
        outs.append(u[n * SGU_CHUNK:(n + 1) * SGU_CHUNK] * mixed)
    ysgu = jnp.concatenate(outs, axis=0)
    ys_ref[0] = _group_rms(ysgu, mg_ref[:, 3 * gw:4 * gw]).astype(bf16)


def _project_and_mix(x, shift, scale, g, w_bf, cw, cb, cng, cnb, pw_bd, psc, sg, sb, sw, sbias, mg):
    B, S, D = x.shape
    gw = GROUP_WIDTH
    ts = SEQ_TILE
    nt = S // ts
    row = lambda b, i: (b, i, 0)
    col = lambda b, i: (b, 0, i)
    vec = lambda b, i: (b, 0, 0)
    full2 = lambda b, i: (0, 0)
    full3 = lambda b, i: (0, 0, 0)
    mixed = jax.ShapeDtypeStruct((B, S, gw), bf16)
    return pl.pallas_call(
        _project_mix_kernel,
        grid=(B, nt),
        in_specs=[pl.BlockSpec((1, ts, D), row),
                  pl.BlockSpec((1, 1, D), vec),
                  pl.BlockSpec((1, 1, D), vec),
                  pl.BlockSpec((1, D), full2),
                  pl.BlockSpec(w_bf.shape, full2),
                  pl.BlockSpec(cw.shape, full2), pl.BlockSpec(cb.shape, full2),
                  pl.BlockSpec(cng.shape, full2), pl.BlockSpec(cnb.shape, full2),
                  pl.BlockSpec(pw_bd.shape, full2), pl.BlockSpec(psc.shape, full2),
                  pl.BlockSpec(sg.shape, full2), pl.BlockSpec(sb.shape, full2),
                  pl.BlockSpec(sw.shape, full3), pl.BlockSpec(sbias.shape, full2),
                  pl.BlockSpec(mg.shape, full2)],
        out_specs=[pl.BlockSpec((1, gw, ts), col),
                   pl.BlockSpec((1, HEADS, ts, HEAD_DIM), lambda b, i: (b, 0, i, 0)),
                   pl.BlockSpec((1, ts // MOBA_BLOCK, HEADS, V_ROWS, MOBA_BLOCK), lambda b, i: (b, i, 0, 0, 0)),
                   pl.BlockSpec((1, 1, ts // MOBA_BLOCK, gw), lambda b, i: (b, i, 0, 0)),
                   pl.BlockSpec((1, ts, gw), row), pl.BlockSpec((1, ts, gw), row), pl.BlockSpec((1, ts, gw), row)],
        out_shape=[jax.ShapeDtypeStruct((B, gw, S), bf16),
                   jax.ShapeDtypeStruct((B, HEADS, S, HEAD_DIM), bf16),
                   jax.ShapeDtypeStruct((B, S // MOBA_BLOCK, HEADS, V_ROWS, MOBA_BLOCK), bf16),
                   jax.ShapeDtypeStruct((B, nt, ts // MOBA_BLOCK, gw), f32),
                   mixed, mixed, mixed],
        scratch_shapes=[pltpu.VMEM((ts + HALO, gw), f32), pltpu.VMEM((SUBLANES - 1, ts + HALO, gw), f32),
                        pltpu.VMEM((ts + HALO, gw), f32),
                        pltpu.VMEM((ts + 16, gw), f32), pltpu.VMEM((ts + 16, gw), f32),
                        pltpu.VMEM((ts + 16, gw), f32)],
        compiler_params=_cparams(("arbitrary", "arbitrary")),
        name="project_and_mix",
    )(x, shift, scale, g, w_bf, cw, cb, cng, cnb, pw_bd, psc, sg, sb, sw, sbias, mg)


def _t5_bucket_table(max_dist):
    d = np.arange(max_dist, dtype=np.int64)
    max_exact = N_BUCKETS // 2
    nf = np.maximum(d, 1).astype(np.float32)
    large = max_exact + (np.log(nf / np.float32(max_exact)) / np.float32(math.log(T5_MAX_DISTANCE / max_exact))
                         * np.float32(N_BUCKETS - max_exact)).astype(np.int32)
    large = np.minimum(large, N_BUCKETS - 1)
    return np.where(d < max_exact, d, large).astype(np.int32)


_TILE_BASES = (0, MOBA_BLOCK)


def _bias_kernel(tab_ref, o_ref):
    blk, qc = MOBA_BLOCK, MOBA_BLOCK
    table = _t5_bucket_table(2 * blk + qc)
    first = [int(np.argmax(table >= b)) for b in range(N_BUCKETS)]
    j = lax.broadcasted_iota(i32, (blk, qc), 0)
    q = lax.broadcasted_iota(i32, (blk, qc), 1)
    for t, base in enumerate(_TILE_BASES):
        d = base + q - j
        lo, hi = max(base - (blk - 1), 0), base + qc - 1
        for h in range(HEADS):
            val = jnp.full((blk, qc), tab_ref[h], f32)
            for b in range(1, N_BUCKETS):
                if first[b] > hi:
                    continue
                if first[b] <= lo:
                    val = jnp.full((blk, qc), tab_ref[b * HEADS + h], f32)
                else:
                    val = jnp.where(d >= first[b], tab_ref[b * HEADS + h], val)
            o_ref[t, h] = jnp.where(d >= 0, val * LOG2E, NEG)


def _bias_tiles(rel_bias):
    return pl.pallas_call(
        _bias_kernel,
        in_specs=[pl.BlockSpec(memory_space=pltpu.SMEM)],
        out_shape=jax.ShapeDtypeStruct((len(_TILE_BASES), HEADS, MOBA_BLOCK, MOBA_BLOCK), f32),
        name="bias_tiles",
    )(rel_bias.astype(f32).reshape(-1))


def _attn_kernel(far_ref, qT_ref, k_ref, vT_ref, km_ref, bias_ref, mg_ref, o_ref, mask_ref, sa_ref, sb_ref):
    own = pl.program_id(1)
    blk, gw, hd = MOBA_BLOCK, GROUP_WIDTH, HEAD_DIM
    nb = km_ref.shape[1]

    nio = lax.broadcasted_iota(i32, (nb, blk), 0)
    past = nio < own
    km = km_ref[0]
    q_heads = []
    for h in range(HEADS):
        qh = qT_ref[0, h * hd:(h + 1) * hd, :]
        q_heads.append(qh)
        km_hi, km_lo = _split_bf16(km[:, h * hd:(h + 1) * hd])
        gate = (jnp.dot(km_hi, qh, preferred_element_type=f32)
                + jnp.dot(km_lo, qh, preferred_element_type=f32))
        gate = jnp.where(past, gate, -jnp.inf)
        picked = jnp.zeros((nb, blk), jnp.bool_)
        for _ in range(MOBA_TOPK):
            top = jnp.max(gate, axis=0, keepdims=True)
            first = jnp.min(jnp.where(gate == top, nio, nb), axis=0, keepdims=True)
            hit = nio == first
            picked = picked | hit
            gate = jnp.where(hit, -jnp.inf, gate)
        sel = picked & past
        mask_ref[0, h] = jnp.where(sel, 0.0, NEG)
        mask_ref[1, h] = jnp.where(sel & (nio < own - 1), far_ref[h] * LOG2E, NEG)

    def qk(n, h):
        kb = k_ref[0, h, pl.ds(pl.multiple_of(n * blk, blk), blk), :]
        return jnp.dot(kb, q_heads[h], preferred_element_type=f32)

    def far_scores(n, h):
        return qk(n, h) + mask_ref[1, h, pl.ds(n, 1), :]

    lane_groups = blk // ATTN_QUERY_GROUP

    def update(state, scores, n):
        out = []
        for h in range(HEADS):
            vb = vT_ref[0, n, h]
            for c in range(lane_groups):
                u = h * lane_groups + c
                m, acc = state[2 * u:2 * u + 2]
                s = scores[h][:, c * ATTN_QUERY_GROUP:(c + 1) * ATTN_QUERY_GROUP]
                m_new = jnp.maximum(m, jnp.max(s, axis=0, keepdims=True))
                alpha = jnp.exp2(m - m_new)
                p = jnp.exp2(s - m_new)
                acc = acc * alpha + jnp.dot(vb, p.astype(bf16), preferred_element_type=f32)
                out += [m_new, acc]
        return tuple(out)

    adj = jnp.maximum(own - 1, 0)
    n_far = jnp.maximum(own - 1, 0)
    s_own = [qk(own, h) + bias_ref[0, h] for h in range(HEADS)]
    s_adj = [qk(adj, h) + bias_ref[1, h] + mask_ref[0, h, pl.ds(adj, 1), :] for h in range(HEADS)]
    for h in range(HEADS):
        sa_ref[h] = far_scores(0, h)

    state = []
    for h in range(HEADS):
        vb = jnp.concatenate([vT_ref[0, own, h], vT_ref[0, adj, h]], axis=1)
        for c in range(lane_groups):
            so = s_own[h][:, c * ATTN_QUERY_GROUP:(c + 1) * ATTN_QUERY_GROUP]
            sj = s_adj[h][:, c * ATTN_QUERY_GROUP:(c + 1) * ATTN_QUERY_GROUP]
            m0 = jnp.maximum(jnp.max(so, axis=0, keepdims=True), jnp.max(sj, axis=0, keepdims=True))
            p = jnp.concatenate([jnp.exp2(so - m0).astype(bf16), jnp.exp2(sj - m0).astype(bf16)], axis=0)
            state += [m0, jnp.dot(vb, p, preferred_element_type=f32)]
    state = tuple(state)

    def body(i, state):
        first = jnp.minimum(2 * i, nb - 1)
        second = jnp.minimum(2 * i + 1, nb - 1)
        third = jnp.minimum(2 * i + 2, nb - 1)
        for h in range(HEADS):
            sb_ref[h] = far_scores(second, h)
        state = update(state, [sa_ref[h] for h in range(HEADS)], first)
        for h in range(HEADS):
            sa_ref[h] = far_scores(third, h)
        return update(state, [sb_ref[h] for h in range(HEADS)], second)

    fin = lax.fori_loop(0, (n_far + 1) // 2, body, state)

    def normalised(acc):
        return acc[0:hd, :] / acc[hd:hd + 1, :]

    outT = jnp.concatenate(
        [jnp.concatenate([normalised(fin[2 * (h * lane_groups + c) + 1]) for c in range(lane_groups)], axis=1)
         for h in range(HEADS)], axis=0)
    o_ref[0] = _group_rms(outT.T, mg_ref[:, gw:2 * gw]).astype(bf16)


def _moba_attention(qT, k, vT, kmean, bias_tiles, far_bias, mg):
    B, _, S, _ = k.shape
    gw, blk = GROUP_WIDTH, MOBA_BLOCK
    nb = kmean.shape[1]
    return pl.pallas_call(
        _attn_kernel,
        grid=(B, nb),
        in_specs=[pl.BlockSpec(memory_space=pltpu.SMEM),
                  pl.BlockSpec((1, gw, blk), lambda b, c: (b, 0, c)),
                  pl.BlockSpec((1, HEADS, S, HEAD_DIM), lambda b, c: (b, 0, 0, 0)),
                  pl.BlockSpec((1, nb, HEADS, V_ROWS, blk), lambda b, c: (b, 0, 0, 0, 0)),
                  pl.BlockSpec((1, nb, gw), lambda b, c: (b, 0, 0)),
                  pl.BlockSpec(bias_tiles.shape, lambda b, c: (0, 0, 0, 0)),
                  pl.BlockSpec(mg.shape, lambda b, c: (0, 0))],
        out_specs=pl.BlockSpec((1, blk, gw), lambda b, c: (b, c, 0)),
        out_shape=jax.ShapeDtypeStruct((B, S, gw), bf16),
        scratch_shapes=[pltpu.VMEM((2, HEADS, nb, blk), f32),
                        pltpu.VMEM((HEADS, blk, blk), f32), pltpu.VMEM((HEADS, blk, blk), f32)],
        compiler_params=_cparams(("arbitrary", "arbitrary")),
        name="moba_attention",
    )(far_bias, qT, k, vT, kmean, bias_tiles, mg)


def _outproj_router_kernel(yc_ref, ya_ref, yp_ref, ys_ref, x_ref, g1_ref, wo_ref, n2_ref, sh_ref, sc_ref,
                           rw_ref, rb_ref,
                           x1_ref, h2_ref, dest_ref, gate_ref, pad_ref):
    W = ROUTE_TILE
    windows = range(ROUTE_WINDOWS_PER_STEP)
    rows = [slice(u * W, (u + 1) * W) for u in windows]
    mixed = [jnp.concatenate([r[rows[u], :] for r in (yc_ref, ya_ref, yp_ref, ys_ref)], axis=1) for u in windows]
    projected = [jnp.dot(m, wo_ref[...], preferred_element_type=f32) for m in mixed]

    his, los = [], []
    for u in windows:
        x1 = x_ref[rows[u], :] + g1_ref[0] * projected[u]
        x1_ref[rows[u], :] = x1
        ms = jnp.mean(x1 * x1, axis=-1, keepdims=True)
        h = x1 * lax.rsqrt(ms + EPS) * n2_ref[...]
        h = h * (1.0 + sc_ref[0]) + sh_ref[0]
        h_hi, h_lo = _split_bf16(h)
        h2_ref[rows[u], :] = h_hi
        his.append(h_hi)
        los.append(h_lo)

    nt = (((1,), (1,)), ((), ()))
    rw_hi, rw_lo = _split_bf16(rw_ref[...])
    logits = [(lax.dot_general(rw_hi, his[u], nt, preferred_element_type=f32)
               + lax.dot_general(rw_hi, los[u], nt, preferred_element_type=f32)
               + lax.dot_general(rw_lo, his[u], nt, preferred_element_type=f32)) + rb_ref[...] for u in windows]

    eio = lax.broadcasted_iota(i32, (N_EXPERTS, W), 0)
    sels, multis = [], []
    for u in windows:
        work = logits[u]
        vals, sel_u = [], []
        for k in range(TOP_K):
            m = jnp.max(work, axis=0, keepdims=True)
            idx = jnp.min(jnp.where(work == m, eio, N_EXPERTS), axis=0, keepdims=True)
            sel = eio == idx
            vals.append(m)
            sel_u.append(sel)
            work = jnp.where(sel, -jnp.inf, work)
        exps = [jnp.exp(v - vals[0]) for v in vals]
        denom = exps[0] + exps[1] + exps[2] + exps[3]
        for k in range(TOP_K):
            gate_ref[k:k + 1, rows[u]] = exps[k] / denom
        multi = jnp.zeros((N_EXPERTS, W), f32)
        for sel in sel_u:
            multi = multi + sel.astype(f32)
        sels.append(sel_u)
        multis.append(multi)

    before = (lax.broadcasted_iota(i32, (W, W), 0) < lax.broadcasted_iota(i32, (W, W), 1)).astype(bf16)
    earlier = [jnp.dot(multis[u].astype(bf16), before, preferred_element_type=f32) for u in windows]

    lower = (lax.broadcasted_iota(i32, (N_EXPERTS, N_EXPERTS), 1)
             < lax.broadcasted_iota(i32, (N_EXPERTS, N_EXPERTS), 0)).astype(bf16)
    seg_start = []
    for u in windows:
        cnt = jnp.sum(multis[u], axis=1, keepdims=True).astype(i32)
        padded = (cnt + (ROW_CHUNK - 1)) // ROW_CHUNK * ROW_CHUNK
        pad_ref[u] = padded
        seg_start.append(jnp.dot(lower, jnp.broadcast_to(padded.astype(f32), (N_EXPERTS, W)).astype(bf16),
                                 preferred_element_type=f32))
    for u in windows:
        row = seg_start[u] + earlier[u]
        for k in range(TOP_K):
            dest_ref[k:k + 1, rows[u]] = jnp.sum(jnp.where(sels[u][k], row, 0.0), axis=0,
                                                 keepdims=True).astype(i32)


def _outproj_router(yc, ya, yp, ys, x, gate1, wo_bf, n2g, shift2, scale2, rwT, rb, seq):
    N, D = x.shape
    gw = GROUP_WIDTH
    nw = N // ROUTE_TILE
    wps = ROUTE_WINDOWS_PER_STEP
    W = ROUTE_TILE * wps
    per_b = seq // W
    row = lambda i: (i, 0)
    vec = lambda i: (i // per_b, 0, 0)
    full = lambda i: (0, 0)
    colblk = lambda i: (0, i)
    return pl.pallas_call(
        _outproj_router_kernel,
        grid=(nw // wps,),
        in_specs=[pl.BlockSpec((W, gw), row)] * 4 + [
            pl.BlockSpec((W, D), row),
            pl.BlockSpec((1, 1, D), vec),
            pl.BlockSpec(wo_bf.shape, full),
            pl.BlockSpec((1, D), full),
            pl.BlockSpec((1, 1, D), vec),
            pl.BlockSpec((1, 1, D), vec),
            pl.BlockSpec(rwT.shape, full),
            pl.BlockSpec(rb.shape, full)],
        out_specs=[pl.BlockSpec((W, D), row),
                   pl.BlockSpec((W, D), row),
                   pl.BlockSpec((TOP_K, W), colblk),
                   pl.BlockSpec((TOP_K, W), colblk),
                   pl.BlockSpec((wps, N_EXPERTS, 1), lambda i: (i, 0, 0))],
        out_shape=[jax.ShapeDtypeStruct((N, D), f32),
                   jax.ShapeDtypeStruct((N, D), bf16),
                   jax.ShapeDtypeStruct((TOP_K, N), i32),
                   jax.ShapeDtypeStruct((TOP_K, N), f32),
                   jax.ShapeDtypeStruct((nw, N_EXPERTS, 1), i32)],
        compiler_params=_cparams(("arbitrary",)),
        name="outproj_router",
    )(yc, ya, yp, ys, x, gate1, wo_bf, n2g, shift2, scale2, rwT, rb)


def _max_window_rows():
    return -(-(ROUTE_TILE * TOP_K + N_EXPERTS * (ROW_CHUNK - 1)) // 128) * 128


def _pack_pairs(lo, hi):
    lo_bits = lax.bitcast_convert_type(lo, jnp.uint32)
    hi_bits = lax.bitcast_convert_type(hi, jnp.uint32)
    return lax.bitcast_convert_type(hi_bits | (lo_bits >> 16), i32)


def _unpack_pairs(words):
    bits = lax.bitcast_convert_type(words, jnp.uint32)
    lo = lax.bitcast_convert_type(bits << 16, f32).astype(bf16)
    hi = lax.bitcast_convert_type(bits & jnp.uint32(0xFFFF0000), f32).astype(bf16)
    return lo, hi


def _dispatch_kernel(piece_src_s, piece_dst_s, wcopies_s, tail_start_s, tail_chunks_s, n_used_s,
                     h_ref, dest_ref, xs_ref, sorted_ref, zero_ref, sem):
    w = pl.program_id(0)
    nw = pl.num_programs(0)
    W = ROUTE_TILE
    R = sorted_ref.shape[1]
    half = sorted_ref.shape[2]
    slot = w % 2
    for r0 in range(0, R, DISPATCH_ROW_BLOCK):
        rio = r0 + lax.broadcasted_iota(i32, (DISPATCH_ROW_BLOCK, W), 0)
        hit = rio == dest_ref[0:1, :]
        for k in range(1, TOP_K):
            hit = hit | (rio == dest_ref[k:k + 1, :])
        onehot = jnp.where(hit, 1.0, 0.0).astype(bf16)
        rows = jnp.dot(onehot, h_ref[...], preferred_element_type=f32)
        sorted_ref[slot, r0:r0 + DISPATCH_ROW_BLOCK, :] = _pack_pairs(rows[:, :half], rows[:, half:])

    def piece_copy(s, src_row, dst_row):
        return pltpu.make_async_copy(
            sorted_ref.at[s, pl.ds(pl.multiple_of(src_row, ROW_CHUNK), DISPATCH_COPY_ROWS), :],
            xs_ref.at[pl.ds(pl.multiple_of(dst_row, ROW_CHUNK), DISPATCH_COPY_ROWS), :], sem.at[s])

    def zero_copy(dst_row):
        return pltpu.make_async_copy(
            zero_ref.at[0:ROW_CHUNK, :],
            xs_ref.at[pl.ds(pl.multiple_of(dst_row, ROW_CHUNK), ROW_CHUNK), :], sem.at[0])

    def zero_tile_copy(tile):
        return pltpu.make_async_copy(
            zero_ref, xs_ref.at[pl.ds(pl.multiple_of(tile * FFN_TILE, FFN_TILE), FFN_TILE), :], sem.at[0])

    def drain(s, count):
        @pl.when(count > 0)
        def _():
            rows = pl.multiple_of(count * DISPATCH_COPY_ROWS, DISPATCH_COPY_ROWS)
            pltpu.make_async_copy(xs_ref.at[pl.ds(0, rows), :], xs_ref.at[pl.ds(0, rows), :], sem.at[s]).wait()

    @pl.when(w > 0)
    def _():
        drain(1 - slot, wcopies_s[jnp.maximum(w - 1, 0)])

    for parity in range(2):
        @pl.when(slot == parity)
        def _():
            def issue(q, carry):
                piece_copy(parity, piece_src_s[w * MAX_DISPATCH_PIECES + q],
                           piece_dst_s[w * MAX_DISPATCH_PIECES + q]).start()
                return carry
            lax.fori_loop(0, wcopies_s[w], issue, 0)

    @pl.when(w == nw - 1)
    def _():
        drain(slot, wcopies_s[w])
        zero_ref[...] = jnp.zeros(zero_ref.shape, i32)

        def per_tail(e, total):
            n = tail_chunks_s[e]
            dst = tail_start_s[e]

            def issue(j, c):
                zero_copy(dst + j * ROW_CHUNK).start()
                return c
            lax.fori_loop(0, n, issue, 0)
            return total + n
        tails = lax.fori_loop(0, N_EXPERTS, per_tail, 0)

        def drain_tail(j, c):
            zero_copy(0).wait()
            return c
        lax.fori_loop(0, tails, drain_tail, 0)

        n_tiles = xs_ref.shape[0] // FFN_TILE

        def issue_tile(j, c):
            zero_tile_copy(j).start()
            return c
        lax.fori_loop(n_used_s[0], n_tiles, issue_tile, 0)

        def drain_tile(j, c):
            zero_tile_copy(0).wait()
            return c
        lax.fori_loop(n_used_s[0], n_tiles, drain_tile, 0)


def _dispatch(h2, destT, t, p_rows):
    N, D = h2.shape
    W = ROUTE_TILE
    nw = N // W
    R = _max_window_rows()
    assert R >= W * TOP_K + N_EXPERTS * (ROW_CHUNK - 1) + DISPATCH_COPY_ROWS - ROW_CHUNK
    grid_spec = pltpu.PrefetchScalarGridSpec(
        num_scalar_prefetch=6,
        grid=(nw,),
        in_specs=[pl.BlockSpec((W, D), lambda w, *_: (w, 0)),
                  pl.BlockSpec((TOP_K, W), lambda w, *_: (0, w))],
        out_specs=pl.BlockSpec(memory_space=pl.ANY),
        scratch_shapes=[pltpu.VMEM((2, R, D // 2), i32), pltpu.VMEM((FFN_TILE, D // 2), i32),
                        pltpu.SemaphoreType.DMA((2,))],
    )
    return pl.pallas_call(
        _dispatch_kernel,
        grid_spec=grid_spec,
        out_shape=jax.ShapeDtypeStruct((p_rows, D // 2), i32),
        compiler_params=_cparams(("arbitrary",)),
        name="expert_dispatch",
    )(t['piece_src'], t['piece_dst'], t['wcopies'], t['tail_start'], t['tail_chunks'], t['n_used'], h2, destT)


def _ffn_kernel(tile_expert_s, next_expert_s, n_used_s, x_ref, w1_hbm, b1_ref, w2_hbm, b2_ref, y_ref,
                w1f_ref, w2f_ref, w1b_ref, w2b_ref, sem, *, layer):
    i = pl.program_id(0)
    last = n_used_s[0] - 1
    expert = tile_expert_s[jnp.minimum(i, last)]
    prev_expert = tile_expert_s[jnp.minimum(jnp.maximum(i - 1, 0), last)]

    def weight_copies(e):
        return (pltpu.make_async_copy(w1_hbm.at[layer, e], w1f_ref, sem.at[0]),
                pltpu.make_async_copy(w2_hbm.at[layer, e], w2f_ref, sem.at[1]))

    @pl.when(i == 0)
    def _():
        for c in weight_copies(expert):
            c.start()

    @pl.when((i == 0) | (expert != prev_expert))
    def _():
        for c in weight_copies(expert):
            c.wait()
        w1b_ref[...] = w1f_ref[...].astype(bf16)
        w2b_ref[...] = w2f_ref[...].astype(bf16)
        nxt = next_expert_s[expert]

        @pl.when(nxt < N_EXPERTS)
        def _():
            for c in weight_copies(nxt):
                c.start()

    @pl.when(i < n_used_s[0])
    def _():
        dff = w2b_ref.shape[0]
        half = x_ref.shape[1]
        sub = x_ref.shape[0] // FFN_SUBTILES
        hidden = []
        for r in range(FFN_SUBTILES):
            x_lo, x_hi = _unpack_pairs(x_ref[r * sub:(r + 1) * sub, :])
            hidden.append(jnp.dot(x_lo, w1b_ref[0:half, :], preferred_element_type=f32)
                          + jnp.dot(x_hi, w1b_ref[half:2 * half, :], preferred_element_type=f32))
        for r in range(FFN_SUBTILES):
            hh = hidden[r] + b1_ref[0, 0]
            x_glu = jnp.minimum(hh[:, :dff], SWIGLU_LIMIT)
            x_lin = jnp.clip(hh[:, dff:], -SWIGLU_LIMIT, SWIGLU_LIMIT)
            act = x_glu * jax.nn.sigmoid(SWIGLU_ALPHA * x_glu) * (x_lin + 1.0)
            y = jnp.dot(act.astype(bf16), w2b_ref[...], preferred_element_type=f32) + b2_ref[0, 0]
            y = y.astype(bf16).astype(f32)
            y_ref[r * sub:(r + 1) * sub, :] = _pack_pairs(y[:, :half], y[:, half:])

    @pl.when(i >= n_used_s[0])
    def _():
        y_ref[...] = jnp.zeros(y_ref.shape, i32)


def _expert_ffn(xs, w1, b1, w2, b2, t, layer):
    P, half = xs.shape
    L, E, D, F2 = w1.shape
    tm = FFN_TILE
    nt = P // tm

    def tile(i, te, ne, nu):
        return (jnp.minimum(i, nu[0] - 1), 0)

    def expert4(i, te, ne, nu):
        return (layer, te[jnp.minimum(i, nu[0] - 1)], 0, 0)

    grid_spec = pltpu.PrefetchScalarGridSpec(
        num_scalar_prefetch=3,
        grid=(nt,),
        in_specs=[pl.BlockSpec((tm, half), tile),
                  pl.BlockSpec(memory_space=pl.ANY),
                  pl.BlockSpec((1, 1, 1, F2), expert4),
                  pl.BlockSpec(memory_space=pl.ANY),
                  pl.BlockSpec((1, 1, 1, D), expert4)],
        out_specs=pl.BlockSpec((tm, half), lambda i, te, ne, nu: (i, 0)),
        scratch_shapes=[pltpu.VMEM((D, F2), f32), pltpu.VMEM((F2 // 2, D), f32),
                        pltpu.VMEM((D, F2), bf16), pltpu.VMEM((F2 // 2, D), bf16),
                        pltpu.SemaphoreType.DMA((2,))],
    )
    return pl.pallas_call(
        functools.partial(_ffn_kernel, layer=layer),
        grid_spec=grid_spec,
        out_shape=jax.ShapeDtypeStruct((P, half), i32),
        compiler_params=_cparams(("arbitrary",)),
        name="expert_ffn",
    )(t['tile_expert'], t['next_expert'], t['n_used'], xs, w1, b1.reshape(L, E, 1, F2), w2,
      b2.reshape(L, E, 1, D))


def _combine_kernel(big_src_s, big_dst_s, wbig_s, small_src_s, small_dst_s, wsmall_s, wchunks_s,
                    ys_ref, dest_ref, gate_ref, x1_ref, g2_ref, fg_ref, o_ref, local_ref, sem, *, final):
    step = pl.program_id(0)
    n_steps = pl.num_programs(0)
    W = ROUTE_TILE
    wps = COMBINE_WINDOWS_PER_STEP
    R = local_ref.shape[2]
    slot = step % 2

    def rows_copy(s, u, src_row, dst_row, rows):
        return pltpu.make_async_copy(
            ys_ref.at[pl.ds(pl.multiple_of(src_row, ROW_CHUNK), rows), :],
            local_ref.at[s, u, pl.ds(pl.multiple_of(dst_row, ROW_CHUNK), rows), :], sem.at[s, u])

    def fetch(at_step, s):
        for u in range(wps):
            win = at_step * wps + u

            def big(q, carry, win=win, u=u):
                rows_copy(s, u, big_src_s[win * MAX_COMBINE_PIECES + q], big_dst_s[win * MAX_COMBINE_PIECES + q],
                          COMBINE_COPY_ROWS).start()
                return carry
            lax.fori_loop(0, wbig_s[win], big, 0)

            def small(q, carry, win=win, u=u):
                rows_copy(s, u, small_src_s[win * MAX_COMBINE_CHUNKS + q],
                          small_dst_s[win * MAX_COMBINE_CHUNKS + q], ROW_CHUNK).start()
                return carry
            lax.fori_loop(0, wsmall_s[win], small, 0)

    @pl.when(step == 0)
    def _():
        local_ref[...] = jnp.zeros(local_ref.shape, i32)
        fetch(0, 0)

    for parity in range(2):
        @pl.when((step + 1 < n_steps) & (slot == parity))
        def _():
            fetch(jnp.minimum(step + 1, n_steps - 1), 1 - parity)

    for u in range(wps):
        chunks = wchunks_s[step * wps + u]

        @pl.when(chunks > 0)
        def _(chunks=chunks, u=u):
            rows = pl.multiple_of(chunks * ROW_CHUNK, ROW_CHUNK)
            pltpu.make_async_copy(ys_ref.at[pl.ds(0, rows), :], local_ref.at[slot, u, pl.ds(0, rows), :],
                                  sem.at[slot, u]).wait()

    rio = lax.broadcasted_iota(i32, (R, W), 0)
    tn = (((0,), (0,)), ((), ()))
    weights, halves = [], []
    for u in range(wps):
        cols = slice(u * W, (u + 1) * W)
        wt = jnp.zeros((R, W), f32)
        for k in range(TOP_K):
            wt = jnp.where(rio == dest_ref[k:k + 1, cols], gate_ref[k:k + 1, cols], wt)
        weights.append(wt.astype(bf16))
        halves.append(_unpack_pairs(local_ref[slot, u]))
    for u in range(wps):
        y_lo, y_hi = halves[u]
        moe = jnp.concatenate([lax.dot_general(weights[u], y_lo, tn, preferred_element_type=f32),
                               lax.dot_general(weights[u], y_hi, tn, preferred_element_type=f32)], axis=1)
        rows = slice(u * W, (u + 1) * W)
        x2 = x1_ref[rows, :] + g2_ref[0] * moe
        if final:
            ms = jnp.mean(x2 * x2, axis=-1, keepdims=True)
            x2 = x2 * lax.rsqrt(ms + EPS) * fg_ref[...]
        o_ref[rows, :] = x2


def _combine(ys, dest, gates, x1, gate2, final_g, t, seq, final):
    N, D = x1.shape
    wps = COMBINE_WINDOWS_PER_STEP
    W = ROUTE_TILE * wps
    per_b = seq // W
    R = _max_window_rows()
    grid_spec = pltpu.PrefetchScalarGridSpec(
        num_scalar_prefetch=7,
        grid=(N // W,),
        in_specs=[pl.BlockSpec(memory_space=pl.ANY),
                  pl.BlockSpec((TOP_K, W), lambda w, *_: (0, w)),
                  pl.BlockSpec((TOP_K, W), lambda w, *_: (0, w)),
                  pl.BlockSpec((W, D), lambda w, *_: (w, 0)),
                  pl.BlockSpec((1, 1, D), lambda w, *_: (w // per_b, 0, 0)),
                  pl.BlockSpec((1, D), lambda w, *_: (0, 0))],
        out_specs=pl.BlockSpec((W, D), lambda w, *_: (w, 0)),
        scratch_shapes=[pltpu.VMEM((2, wps, R, D // 2), i32), pltpu.SemaphoreType.DMA((2, wps))],
    )
    return pl.pallas_call(
        functools.partial(_combine_kernel, final=final),
        grid_spec=grid_spec,
        out_shape=jax.ShapeDtypeStruct((N, D), f32),
        compiler_params=_cparams(("arbitrary",)),
        name="expert_combine",
    )(t['big_src'], t['big_dst'], t['wbig'], t['small_src'], t['small_dst'], t['wsmall'], t['wchunks'],
      ys, dest, gates, x1, gate2, final_g)


def _routing_tables(padded, n_tiles):
    nw, E = padded.shape
    lstart = jnp.cumsum(padded, axis=1) - padded
    tot = jnp.sum(padded, axis=0)
    slack = DISPATCH_COPY_ROWS - ROW_CHUNK
    region = jnp.where(tot > 0, (tot + slack + FFN_TILE - 1) // FFN_TILE * FFN_TILE, 0)
    region_end = jnp.cumsum(region)
    region_start = region_end - region
    gbase = region_start[None, :] + jnp.cumsum(padded, axis=0) - padded
    nchunk = padded // ROW_CHUNK
    tail_start = region_start + tot
    tail_chunks = (region - tot) // ROW_CHUNK
    n_used = (region_end[-1] // FFN_TILE).astype(i32).reshape(1)
    tile_row = jnp.arange(n_tiles, dtype=i32) * FFN_TILE
    tile_expert = jnp.minimum(
        jnp.sum((region_end[None, :] <= tile_row[:, None]).astype(i32), axis=1), E - 1).astype(i32)
    eidx = jnp.arange(E, dtype=i32)
    later_nonempty = (eidx[None, :] > eidx[:, None]) & (region[None, :] > 0)
    next_expert = jnp.min(jnp.where(later_nonempty, eidx[None, :], E), axis=1)
    flat = lambda a: a.reshape(-1).astype(i32)

    def copy_list(counts, window_rows, expert_rows, step, max_items):
        ends = jnp.cumsum(counts, axis=1)
        slot = jnp.arange(max_items, dtype=i32)
        owner = jnp.sum((ends[:, None, :] <= slot[None, :, None]).astype(i32), axis=2)
        hit = owner[:, :, None] == eidx[None, None, :]
        pick = lambda a: jnp.sum(jnp.where(hit, a[:, None, :], 0), axis=2)
        offset = step * (slot[None, :] - pick(ends - counts))
        return flat(pick(window_rows) + offset), flat(pick(expert_rows) + offset), flat(ends[:, -1])

    ncopy = (padded + DISPATCH_COPY_ROWS - 1) // DISPATCH_COPY_ROWS
    piece_src, piece_dst, wcopies = copy_list(ncopy, lstart, gbase, DISPATCH_COPY_ROWS, MAX_DISPATCH_PIECES)
    nbig = padded // COMBINE_COPY_ROWS
    big_dst, big_src, wbig = copy_list(nbig, lstart, gbase, COMBINE_COPY_ROWS, MAX_COMBINE_PIECES)
    rest = nbig * COMBINE_COPY_ROWS
    small_dst, small_src, wsmall = copy_list(nchunk - nbig * (COMBINE_COPY_ROWS // ROW_CHUNK),
                                             lstart + rest, gbase + rest, ROW_CHUNK, MAX_COMBINE_CHUNKS)
    return dict(next_expert=flat(next_expert), wchunks=flat(jnp.sum(nchunk, axis=1)),
                piece_src=piece_src, piece_dst=piece_dst, wcopies=wcopies,
                big_src=big_src, big_dst=big_dst, wbig=wbig,
                small_src=small_src, small_dst=small_dst, wsmall=wsmall,
                tail_start=flat(tail_start), tail_chunks=flat(tail_chunks),
                n_used=n_used, tile_expert=tile_expert)


def _block_diag(w):
    g, a, b = w.shape
    out = jnp.zeros((g * a, g * b), w.dtype)
    for i in range(g):
        out = out.at[i * a:(i + 1) * a, i * b:(i + 1) * b].set(w[i])
    return out


def kernel(x, c, w_mod, b_mod, norm1_g, w_in, conv_w, conv_b, conv_norm_g, conv_norm_b, rel_bias, pool_w, pool_scale, sgu_norm_g, sgu_norm_b, sgu_w, sgu_b, mix_out_g, w_out, norm2_g, router_w, router_b, exp_w1, exp_b1, exp_w2, exp_b2, final_norm_g):
    B, S, D = x.shape
    L = w_mod.shape[0]
    N = B * S
    nw = N // ROUTE_TILE
    assert S % SEQ_TILE == 0 and S % MOBA_BLOCK == 0 and N % ROUTE_TILE == 0 and S % ROUTE_TILE == 0
    p_bound = (N * TOP_K + nw * N_EXPERTS * (ROW_CHUNK - 1)
               + N_EXPERTS * (DISPATCH_COPY_ROWS - ROW_CHUNK + FFN_TILE - 1))
    n_tiles = -(-p_bound // FFN_TILE)
    p_rows = n_tiles * FFN_TILE

    mod = _modulation(c, w_mod, b_mod)
    bias_tiles = _bias_tiles(rel_bias)
    far_bucket = int(_t5_bucket_table(MOBA_BLOCK + 2)[MOBA_BLOCK + 1])
    assert far_bucket == int(_t5_bucket_table(S + 1)[S])
    far_bias = rel_bias[far_bucket].astype(f32)
    row = lambda a: a.reshape(1, -1)
    for l in range(L):
        m6 = mod[l].reshape(B, 6, 1, D)
        shift1, scale1, gate1, shift2, scale2, gate2 = (m6[:, j] for j in range(6))
        mg = row(mix_out_g[l])
        qT, k, vT, kmean, yc, yp, ys = _project_and_mix(
            x, shift1, scale1, row(norm1_g[l]), w_in[l].astype(bf16),
            conv_w[l], row(conv_b[l]), row(conv_norm_g[l]), row(conv_norm_b[l]),
            _block_diag(pool_w[l]).astype(bf16), row(pool_scale[l]),
            row(sgu_norm_g[l]), row(sgu_norm_b[l]), sgu_w[l],
            jnp.repeat(sgu_b[l].T, HEAD_DIM, axis=1), mg)
        ya = _moba_attention(qT, k, vT, kmean.reshape(B, -1, GROUP_WIDTH), bias_tiles, far_bias, mg)
        flat = lambda a: a.reshape(N, -1)
        x1, h2, destT, gateT, padded = _outproj_router(
            flat(yc), flat(ya), flat(yp), flat(ys), x.reshape(N, D), gate1, w_out[l].astype(bf16),
            row(norm2_g[l]), shift2, scale2, router_w[l].T, router_b[l].reshape(-1, 1), S)
        t = _routing_tables(padded.reshape(nw, N_EXPERTS), n_tiles)
        xs = _dispatch(h2, destT, t, p_rows)
        ysort = _expert_ffn(xs, exp_w1, exp_b1, exp_w2, exp_b2, t, l)
        x = _combine(ysort, destT, gateT, x1, gate2, row(final_norm_g), t, S,
                     final=(l == L - 1)).reshape(B, S, D)
    return x
```

```python
import functools
import math

import numpy as np
import jax
import jax.numpy as jnp
from jax import lax
from jax.experimental import pallas as pl
from jax.experimental.pallas import tpu as pltpu

f32, bf16, i32 = jnp.float32, jnp.bfloat16, jnp.int32

GROUP_WIDTH = 256
HEADS = 4
HEAD_DIM = 64
V_ROWS = HEAD_DIM + 16
LOG2E = math.log2(math.e)
CONV_WIDTH = 31
MOBA_BLOCK = 256
MOBA_TOPK = 3
Q_CHUNK = 128
N_BUCKETS = 32
T5_MAX_DISTANCE = 128
POOL_WINDOWS = (2, 4, 8, 16)
SGU_CHUNK = 128
N_EXPERTS = 32
TOP_K = 4
SWIGLU_LIMIT = 7.0
SWIGLU_ALPHA = 1.702
EPS = 1e-6

SUBLANES = 8
LANES = 128
ATTN_QUERY_GROUP = 256
HALO = 32
SEQ_TILE = 512
ROUTE_TILE = 256
ROUTE_WINDOWS_PER_STEP = 4
ROW_CHUNK = 8
DISPATCH_ROW_BLOCK = 256
DISPATCH_BUFFERS = 3
COMBINE_COPY_ROWS = 32
COMBINE_WINDOWS_PER_STEP = 2
MAX_COMBINE_PIECES = (ROUTE_TILE * TOP_K + N_EXPERTS * (ROW_CHUNK - 1)) // COMBINE_COPY_ROWS
MAX_COMBINE_CHUNKS = N_EXPERTS * (COMBINE_COPY_ROWS // ROW_CHUNK - 1)
FFN_TILE = 512
FFN_SUBTILES = 2
NEG = -1e30
VMEM_LIMIT = 56 * 1024 * 1024


def _cparams(sem):
    return pltpu.CompilerParams(dimension_semantics=sem, vmem_limit_bytes=VMEM_LIMIT)


def _split_bf16(a):
    hi = a.astype(bf16)
    lo = (a - hi.astype(f32)).astype(bf16)
    return hi, lo


def _mod_kernel(c_ref, w_ref, b_ref, o_ref):
    c = c_ref[...]
    cond = c * jax.nn.sigmoid(c)
    o_ref[0] = jnp.dot(cond, w_ref[0], preferred_element_type=f32,
                       precision=lax.Precision.HIGHEST) + b_ref[0]


def _modulation(c, w_mod, b_mod):
    L, D, M = w_mod.shape
    B = c.shape[0]
    tn = 1536
    return pl.pallas_call(
        _mod_kernel,
        grid=(L, M // tn),
        in_specs=[pl.BlockSpec((B, D), lambda l, j: (0, 0)),
                  pl.BlockSpec((1, D, tn), lambda l, j: (l, 0, j)),
                  pl.BlockSpec((1, 1, tn), lambda l, j: (l, 0, j))],
        out_specs=pl.BlockSpec((1, B, tn), lambda l, j: (l, 0, j)),
        out_shape=jax.ShapeDtypeStruct((L, B, M), f32),
        compiler_params=_cparams(("arbitrary", "arbitrary")),
        name="modulation",
    )(c, w_mod, b_mod.reshape(L, 1, M))


def _project_mix_kernel(x_ref, sh_ref, sc_ref, g_ref, w_ref,
                        cw_ref, cb_ref, cng_ref, cnb_ref, pw_ref, psc_ref, sg_ref, sb_ref, sw_ref, sbias_ref, mg_ref,
                        qT_ref, k_ref, vT_ref, km_ref, yc_ref, yp_ref, ys_ref,
                        gext, gshift, zext, s2, s4, s8):
    first = pl.program_id(1) == 0

    @pl.when(first)
    def _():
        gext[0:HALO, :] = jnp.zeros((HALO, GROUP_WIDTH), f32)
        zext[0:HALO, :] = jnp.zeros((HALO, GROUP_WIDTH), f32)

    @pl.when(jnp.logical_not(first))
    def _():
        gext[0:HALO, :] = gext[SEQ_TILE:SEQ_TILE + HALO, :]
        zext[0:HALO, :] = zext[SEQ_TILE:SEQ_TILE + HALO, :]

    x = x_ref[0]
    ms = jnp.mean(x * x, axis=-1, keepdims=True)
    h = x * lax.rsqrt(ms + EPS) * g_ref[...]
    h = h * (1.0 + sc_ref[0]) + sh_ref[0]
    gw = GROUP_WIDTH
    hb = h.astype(bf16)
    pc = jnp.dot(hb, w_ref[:, 0:2 * gw], preferred_element_type=f32)
    qkv = jnp.dot(hb, w_ref[:, 2 * gw:5 * gw], preferred_element_type=f32)
    rest = jnp.dot(hb, w_ref[:, 5 * gw:8 * gw], preferred_element_type=f32)
    _local_mixers(pc, rest[:, 0:gw], rest[:, gw:3 * gw],
                  cw_ref, cb_ref, cng_ref, cnb_ref, pw_ref, psc_ref, sg_ref, sb_ref, sw_ref, sbias_ref, mg_ref,
                  yc_ref, yp_ref, ys_ref, gext, gshift, zext, s2, s4, s8)
    proj = jnp.concatenate([pc, qkv], axis=1)
    q = proj[:, 2 * gw:3 * gw] * (HEAD_DIM ** -0.5 * LOG2E)
    qT_ref[0] = q.T.astype(bf16)
    kk = proj[:, 3 * gw:4 * gw]
    for h in range(HEADS):
        k_ref[0, h] = kk[:, h * HEAD_DIM:(h + 1) * HEAD_DIM].astype(bf16)
    for j in range(SEQ_TILE // MOBA_BLOCK):
        km_ref[0, 0, j:j + 1, :] = jnp.mean(kk[j * MOBA_BLOCK:(j + 1) * MOBA_BLOCK], axis=0, keepdims=True)
    for j in range(SEQ_TILE // MOBA_BLOCK):
        vT = proj[j * MOBA_BLOCK:(j + 1) * MOBA_BLOCK, 4 * gw:5 * gw].T.astype(bf16)
        for h in range(HEADS):
            vT_ref[0, j, h] = jnp.concatenate(
                [vT[h * HEAD_DIM:(h + 1) * HEAD_DIM, :], jnp.ones((V_ROWS - HEAD_DIM, MOBA_BLOCK), bf16)], axis=0)


def _group_rms(y, g):
    return y * lax.rsqrt(jnp.mean(y * y, axis=-1, keepdims=True) + EPS) * g


def _local_mixers(pc, z, zz,
                  cw_ref, cb_ref, cng_ref, cnb_ref, pw_ref, psc_ref,
                  sg_ref, sb_ref, sw_ref, sbias_ref, mg_ref,
                  yc_ref, yp_ref, ys_ref,
                  gext, gshift, zext, s2, s4, s8):
    i = pl.program_id(1)
    ts = SEQ_TILE
    gw = GROUP_WIDTH
    lane = lax.broadcasted_iota(i32, (1, gw), 1)

    g = pc[:, :gw] * jax.nn.sigmoid(pc[:, gw:])
    gext[HALO:HALO + ts, :] = g
    span = ts + HALO - SUBLANES
    for a in range(1, SUBLANES):
        gshift[a - 1, 0:span, :] = gext[pl.ds(a, span), :]
    acc = jnp.zeros((ts, gw), f32)
    for j in range(CONV_WIDTH):
        b, a = divmod(HALO - (CONV_WIDTH - 1) + j, SUBLANES)
        tap = gext[pl.ds(SUBLANES * b, ts), :] if a == 0 else gshift[a - 1, pl.ds(SUBLANES * b, ts), :]
        acc = acc + cw_ref[j:j + 1, :] * tap
    y = acc + cb_ref[...]
    r = lax.broadcasted_iota(i32, (gw, gw), 0) // HEAD_DIM
    c = lax.broadcasted_iota(i32, (gw, gw), 1) // HEAD_DIM
    avg = jnp.where(r == c, 1.0 / HEAD_DIM, 0.0).astype(bf16)

    def head_mean(t):
        hi, lo = _split_bf16(t)
        return (jnp.dot(hi, avg, preferred_element_type=f32)
                + jnp.dot(lo, avg, preferred_element_type=f32))

    mu = head_mean(y)
    yc = y - mu
    var = head_mean(yc * yc)
    yn = yc * lax.rsqrt(var + EPS) * cng_ref[...] + cnb_ref[...]
    yconv = yn * jax.nn.sigmoid(yn)
    yc_ref[0] = _group_rms(yconv, mg_ref[:, 0:gw]).astype(bf16)

    zext[HALO:HALO + ts, :] = z
    n2, n4, n8 = ts + 14, ts + 12, ts + 8
    s2[0:n2, :] = zext[pl.ds(HALO - 14, n2), :] + zext[pl.ds(HALO - 15, n2), :]
    s4[0:n4, :] = s2[pl.ds(2, n4), :] + s2[pl.ds(0, n4), :]
    s8[0:n8, :] = s4[pl.ds(4, n8), :] + s4[pl.ds(0, n8), :]
    w2 = s2[pl.ds(14, ts), :]
    w4 = s4[pl.ds(12, ts), :]
    w8 = s8[pl.ds(8, ts), :]
    w16 = w8 + s8[pl.ds(0, ts), :]
    tpos = (i * ts + lax.broadcasted_iota(i32, (ts, 1), 0) + 1).astype(f32)
    grp = lane // (gw // len(POOL_WINDOWS))
    pooled = jnp.zeros((ts, gw), f32)
    for gi, (w, sw) in enumerate(zip(POOL_WINDOWS, (w2, w4, w8, w16))):
        pooled = jnp.where(grp == gi, sw / jnp.minimum(tpos, float(w)), pooled)
    pooled = pooled - z
    yp = jnp.dot(pooled.astype(bf16), pw_ref[...], preferred_element_type=f32) * psc_ref[...]
    yp_ref[0] = _group_rms(yp, mg_ref[:, 2 * gw:3 * gw]).astype(bf16)

    zz = 0.5 * zz * (1.0 + lax.erf(zz * (1.0 / math.sqrt(2.0))))
    u = zz[:, :gw]
    v = zz[:, gw:]
    vm = jnp.mean(v, axis=-1, keepdims=True)
    vc = v - vm
    vv = jnp.mean(vc * vc, axis=-1, keepdims=True)
    vn = (vc * lax.rsqrt(vv + EPS) * sg_ref[...] + sb_ref[...]).astype(bf16)
    li = lax.broadcasted_iota(i32, (SGU_CHUNK, SGU_CHUNK), 0)
    lj = lax.broadcasted_iota(i32, (SGU_CHUNK, SGU_CHUNK), 1)
    head_of_lane = lane // HEAD_DIM
    wts = [jnp.where(li >= lj, sw_ref[h], 0.0).astype(bf16) for h in range(HEADS)]
    outs = []
    for n in range(ts // SGU_CHUNK):
        vch = vn[n * SGU_CHUNK:(n + 1) * SGU_CHUNK]
        mixed = sbias_ref[...]
        for h in range(HEADS):
            mh = jnp.dot(wts[h], vch, preferred_element_type=f32)
            mixed = mixed + jnp.where(head_of_lane == h, mh, 0.0)
        outs.append(u[n * SGU_CHUNK:(n + 1) * SGU_CHUNK] * mixed)
    ysgu = jnp.concatenate(outs, axis=0)
    ys_ref[0] = _group_rms(ysgu, mg_ref[:, 3 * gw:4 * gw]).astype(bf16)


def _project_and_mix(x, shift, scale, g, w_bf, cw, cb, cng, cnb, pw_bd, psc, sg, sb, sw, sbias, mg):
    B, S, D = x.shape
    gw = GROUP_WIDTH
    ts = SEQ_TILE
    nt = S // ts
    row = lambda b, i: (b, i, 0)
    col = lambda b, i: (b, 0, i)
    vec = lambda b, i: (b, 0, 0)
    full2 = lambda b, i: (0, 0)
    full3 = lambda b, i: (0, 0, 0)
    mixed = jax.ShapeDtypeStruct((B, S, gw), bf16)
    return pl.pallas_call(
        _project_mix_kernel,
        grid=(B, nt),
        in_specs=[pl.BlockSpec((1, ts, D), row),
                  pl.BlockSpec((1, 1, D), vec),
                  pl.BlockSpec((1, 1, D), vec),
                  pl.BlockSpec((1, D), full2),
                  pl.BlockSpec(w_bf.shape, full2),
                  pl.BlockSpec(cw.shape, full2), pl.BlockSpec(cb.shape, full2),
                  pl.BlockSpec(cng.shape, full2), pl.BlockSpec(cnb.shape, full2),
                  pl.BlockSpec(pw_bd.shape, full2), pl.BlockSpec(psc.shape, full2),
                  pl.BlockSpec(sg.shape, full2), pl.BlockSpec(sb.shape, full2),
                  pl.BlockSpec(sw.shape, full3), pl.BlockSpec(sbias.shape, full2),
                  pl.BlockSpec(mg.shape, full2)],
        out_specs=[pl.BlockSpec((1, gw, ts), col),
                   pl.BlockSpec((1, HEADS, ts, HEAD_DIM), lambda b, i: (b, 0, i, 0)),
                   pl.BlockSpec((1, ts // MOBA_BLOCK, HEADS, V_ROWS, MOBA_BLOCK), lambda b, i: (b, i, 0, 0, 0)),
                   pl.BlockSpec((1, 1, ts // MOBA_BLOCK, gw), lambda b, i: (b, i, 0, 0)),
                   pl.BlockSpec((1, ts, gw), row), pl.BlockSpec((1, ts, gw), row), pl.BlockSpec((1, ts, gw), row)],
        out_shape=[jax.ShapeDtypeStruct((B, gw, S), bf16),
                   jax.ShapeDtypeStruct((B, HEADS, S, HEAD_DIM), bf16),
                   jax.ShapeDtypeStruct((B, S // MOBA_BLOCK, HEADS, V_ROWS, MOBA_BLOCK), bf16),
                   jax.ShapeDtypeStruct((B, nt, ts // MOBA_BLOCK, gw), f32),
                   mixed, mixed, mixed],
        scratch_shapes=[pltpu.VMEM((ts + HALO, gw), f32), pltpu.VMEM((SUBLANES - 1, ts + HALO, gw), f32),
                        pltpu.VMEM((ts + HALO, gw), f32),
                        pltpu.VMEM((ts + 16, gw), f32), pltpu.VMEM((ts + 16, gw), f32),
                        pltpu.VMEM((ts + 16, gw), f32)],
        compiler_params=_cparams(("arbitrary", "arbitrary")),
        name="project_and_mix",
    )(x, shift, scale, g, w_bf, cw, cb, cng, cnb, pw_bd, psc, sg, sb, sw, sbias, mg)


def _t5_bucket_table(max_dist):
    d = np.arange(max_dist, dtype=np.int64)
    max_exact = N_BUCKETS // 2
    nf = np.maximum(d, 1).astype(np.float32)
    large = max_exact + (np.log(nf / np.float32(max_exact)) / np.float32(math.log(T5_MAX_DISTANCE / max_exact))
                         * np.float32(N_BUCKETS - max_exact)).astype(np.int32)
    large = np.minimum(large, N_BUCKETS - 1)
    return np.where(d < max_exact, d, large).astype(np.int32)


_TILE_BASES = (0, MOBA_BLOCK)


def _bias_kernel(tab_ref, o_ref):
    blk, qc = MOBA_BLOCK, MOBA_BLOCK
    table = _t5_bucket_table(2 * blk + qc)
    first = [int(np.argmax(table >= b)) for b in range(N_BUCKETS)]
    j = lax.broadcasted_iota(i32, (blk, qc), 0)
    q = lax.broadcasted_iota(i32, (blk, qc), 1)
    for t, base in enumerate(_TILE_BASES):
        d = base + q - j
        lo, hi = max(base - (blk - 1), 0), base + qc - 1
        for h in range(HEADS):
            val = jnp.full((blk, qc), tab_ref[h], f32)
            for b in range(1, N_BUCKETS):
                if first[b] > hi:
                    continue
                if first[b] <= lo:
                    val = jnp.full((blk, qc), tab_ref[b * HEADS + h], f32)
                else:
                    val = jnp.where(d >= first[b], tab_ref[b * HEADS + h], val)
            o_ref[t, h] = jnp.where(d >= 0, val * LOG2E, NEG)


def _bias_tiles(rel_bias):
    return pl.pallas_call(
        _bias_kernel,
        in_specs=[pl.BlockSpec(memory_space=pltpu.SMEM)],
        out_shape=jax.ShapeDtypeStruct((len(_TILE_BASES), HEADS, MOBA_BLOCK, MOBA_BLOCK), f32),
        name="bias_tiles",
    )(rel_bias.astype(f32).reshape(-1))


def _attn_kernel(far_ref, qT_ref, k_ref, vT_ref, km_ref, bias_ref, mg_ref, o_ref, mask_ref, sa_ref, sb_ref):
    own = pl.program_id(1)
    blk, gw, hd = MOBA_BLOCK, GROUP_WIDTH, HEAD_DIM
    nb = km_ref.shape[1]

    nio = lax.broadcasted_iota(i32, (nb, blk), 0)
    past = nio < own
    km = km_ref[0]
    q_heads = []
    for h in range(HEADS):
        qh = qT_ref[0, h * hd:(h + 1) * hd, :]
        q_heads.append(qh)
        km_hi, km_lo = _split_bf16(km[:, h * hd:(h + 1) * hd])
        gate = (jnp.dot(km_hi, qh, preferred_element_type=f32)
                + jnp.dot(km_lo, qh, preferred_element_type=f32))
        gate = jnp.where(past, gate, -jnp.inf)
        picked = jnp.zeros((nb, blk), jnp.bool_)
        for _ in range(MOBA_TOPK):
            top = jnp.max(gate, axis=0, keepdims=True)
            first = jnp.min(jnp.where(gate == top, nio, nb), axis=0, keepdims=True)
            hit = nio == first
            picked = picked | hit
            gate = jnp.where(hit, -jnp.inf, gate)
        sel = picked & past
        mask_ref[0, h] = jnp.where(sel, 0.0, NEG)
        mask_ref[1, h] = jnp.where(sel & (nio < own - 1), far_ref[h] * LOG2E, NEG)

    def qk(n, h):
        kb = k_ref[0, h, pl.ds(pl.multiple_of(n * blk, blk), blk), :]
        return jnp.dot(kb, q_heads[h], preferred_element_type=f32)

    def far_scores(n, h):
        return qk(n, h) + mask_ref[1, h, pl.ds(n, 1), :]

    lane_groups = blk // ATTN_QUERY_GROUP

    def update(state, scores, n):
        out = []
        for h in range(HEADS):
            vb = vT_ref[0, n, h]
            for c in range(lane_groups):
                u = h * lane_groups + c
                m, acc = state[2 * u:2 * u + 2]
                s = scores[h][:, c * ATTN_QUERY_GROUP:(c + 1) * ATTN_QUERY_GROUP]
                m_new = jnp.maximum(m, jnp.max(s, axis=0, keepdims=True))
                alpha = jnp.exp2(m - m_new)
                p = jnp.exp2(s - m_new)
                acc = acc * alpha + jnp.dot(vb, p.astype(bf16), preferred_element_type=f32)
                out += [m_new, acc]
        return tuple(out)

    adj = jnp.maximum(own - 1, 0)
    n_far = jnp.maximum(own - 1, 0)
    s_own = [qk(own, h) + bias_ref[0, h] for h in range(HEADS)]
    s_adj = [qk(adj, h) + bias_ref[1, h] + mask_ref[0, h, pl.ds(adj, 1), :] for h in range(HEADS)]
    for h in range(HEADS):
        sa_ref[h] = far_scores(0, h)

    state = []
    for h in range(HEADS):
        vb = jnp.concatenate([vT_ref[0, own, h], vT_ref[0, adj, h]], axis=1)
        for c in range(lane_groups):
            so = s_own[h][:, c * ATTN_QUERY_GROUP:(c + 1) * ATTN_QUERY_GROUP]
            sj = s_adj[h][:, c * ATTN_QUERY_GROUP:(c + 1) * ATTN_QUERY_GROUP]
            m0 = jnp.maximum(jnp.max(so, axis=0, keepdims=True), jnp.max(sj, axis=0, keepdims=True))
            p = jnp.concatenate([jnp.exp2(so - m0).astype(bf16), jnp.exp2(sj - m0).astype(bf16)], axis=0)
            state += [m0, jnp.dot(vb, p, preferred_element_type=f32)]
    state = tuple(state)

    def body(i, state):
        first = jnp.minimum(2 * i, nb - 1)
        second = jnp.minimum(2 * i + 1, nb - 1)
        third = jnp.minimum(2 * i + 2, nb - 1)
        for h in range(HEADS):
            sb_ref[h] = far_scores(second, h)
        state = update(state, [sa_ref[h] for h in range(HEADS)], first)
        for h in range(HEADS):
            sa_ref[h] = far_scores(third, h)
        return update(state, [sb_ref[h] for h in range(HEADS)], second)

    fin = lax.fori_loop(0, (n_far + 1) // 2, body, state)

    def normalised(acc):
        return acc[0:hd, :] / acc[hd:hd + 1, :]

    outT = jnp.concatenate(
        [jnp.concatenate([normalised(fin[2 * (h * lane_groups + c) + 1]) for c in range(lane_groups)], axis=1)
         for h in range(HEADS)], axis=0)
    o_ref[0] = _group_rms(outT.T, mg_ref[:, gw:2 * gw]).astype(bf16)


def _moba_attention(qT, k, vT, kmean, bias_tiles, far_bias, mg):
    B, _, S, _ = k.shape
    gw, blk = GROUP_WIDTH, MOBA_BLOCK
    nb = kmean.shape[1]
    return pl.pallas_call(
        _attn_kernel,
        grid=(B, nb),
        in_specs=[pl.BlockSpec(memory_space=pltpu.SMEM),
                  pl.BlockSpec((1, gw, blk), lambda b, c: (b, 0, c)),
                  pl.BlockSpec((1, HEADS, S, HEAD_DIM), lambda b, c: (b, 0, 0, 0)),
                  pl.BlockSpec((1, nb, HEADS, V_ROWS, blk), lambda b, c: (b, 0, 0, 0, 0)),
                  pl.BlockSpec((1, nb, gw), lambda b, c: (b, 0, 0)),
                  pl.BlockSpec(bias_tiles.shape, lambda b, c: (0, 0, 0, 0)),
                  pl.BlockSpec(mg.shape, lambda b, c: (0, 0))],
        out_specs=pl.BlockSpec((1, blk, gw), lambda b, c: (b, c, 0)),
        out_shape=jax.ShapeDtypeStruct((B, S, gw), bf16),
        scratch_shapes=[pltpu.VMEM((2, HEADS, nb, blk), f32),
                        pltpu.VMEM((HEADS, blk, blk), f32), pltpu.VMEM((HEADS, blk, blk), f32)],
        compiler_params=_cparams(("arbitrary", "arbitrary")),
        name="moba_attention",
    )(far_bias, qT, k, vT, kmean, bias_tiles, mg)


def _outproj_router_kernel(yc_ref, ya_ref, yp_ref, ys_ref, x_ref, g1_ref, wo_ref, n2_ref, sh_ref, sc_ref,
                           rw_ref, rb_ref,
                           x1_ref, h2_ref, dest_ref, gate_ref, pad_ref):
    W = ROUTE_TILE
    windows = range(ROUTE_WINDOWS_PER_STEP)
    rows = [slice(u * W, (u + 1) * W) for u in windows]
    mixed = [jnp.concatenate([r[rows[u], :] for r in (yc_ref, ya_ref, yp_ref, ys_ref)], axis=1) for u in windows]
    projected = [jnp.dot(m, wo_ref[...], preferred_element_type=f32) for m in mixed]

    his, los = [], []
    for u in windows:
        x1 = x_ref[rows[u], :] + g1_ref[0] * projected[u]
        x1_ref[rows[u], :] = x1
        ms = jnp.mean(x1 * x1, axis=-1, keepdims=True)
        h = x1 * lax.rsqrt(ms + EPS) * n2_ref[...]
        h = h * (1.0 + sc_ref[0]) + sh_ref[0]
        h_hi, h_lo = _split_bf16(h)
        h2_ref[rows[u], :] = h_hi
        his.append(h_hi)
        los.append(h_lo)

    nt = (((1,), (1,)), ((), ()))
    rw_hi, rw_lo = _split_bf16(rw_ref[...])
    logits = [(lax.dot_general(rw_hi, his[u], nt, preferred_element_type=f32)
               + lax.dot_general(rw_hi, los[u], nt, preferred_element_type=f32)
               + lax.dot_general(rw_lo, his[u], nt, preferred_element_type=f32)) + rb_ref[...] for u in windows]

    eio = lax.broadcasted_iota(i32, (N_EXPERTS, W), 0)
    sels, multis = [], []
    for u in windows:
        work = logits[u]
        vals, sel_u = [], []
        for k in range(TOP_K):
            m = jnp.max(work, axis=0, keepdims=True)
            idx = jnp.min(jnp.where(work == m, eio, N_EXPERTS), axis=0, keepdims=True)
            sel = eio == idx
            vals.append(m)
            sel_u.append(sel)
            work = jnp.where(sel, -jnp.inf, work)
        exps = [jnp.exp(v - vals[0]) for v in vals]
        denom = exps[0] + exps[1] + exps[2] + exps[3]
        for k in range(TOP_K):
            gate_ref[k:k + 1, rows[u]] = exps[k] / denom
        multi = jnp.zeros((N_EXPERTS, W), f32)
        for sel in sel_u:
            multi = multi + sel.astype(f32)
        sels.append(sel_u)
        multis.append(multi)

    before = (lax.broadcasted_iota(i32, (W, W), 0) < lax.broadcasted_iota(i32, (W, W), 1)).astype(bf16)
    earlier = [jnp.dot(multis[u].astype(bf16), before, preferred_element_type=f32) for u in windows]

    lower = (lax.broadcasted_iota(i32, (N_EXPERTS, N_EXPERTS), 1)
             < lax.broadcasted_iota(i32, (N_EXPERTS, N_EXPERTS), 0)).astype(bf16)
    seg_start = []
    for u in windows:
        cnt = jnp.sum(multis[u], axis=1, keepdims=True).astype(i32)
        padded = (cnt + (ROW_CHUNK - 1)) // ROW_CHUNK * ROW_CHUNK
        pad_ref[u] = padded
        seg_start.append(jnp.dot(lower, jnp.broadcast_to(padded.astype(f32), (N_EXPERTS, W)).astype(bf16),
                                 preferred_element_type=f32))
    for u in windows:
        row = seg_start[u] + earlier[u]
        for k in range(TOP_K):
            dest_ref[k:k + 1, rows[u]] = jnp.sum(jnp.where(sels[u][k], row, 0.0), axis=0,
                                                 keepdims=True).astype(i32)


def _outproj_router(yc, ya, yp, ys, x, gate1, wo_bf, n2g, shift2, scale2, rwT, rb, seq):
    N, D = x.shape
    gw = GROUP_WIDTH
    nw = N // ROUTE_TILE
    wps = ROUTE_WINDOWS_PER_STEP
    W = ROUTE_TILE * wps
    per_b = seq // W
    row = lambda i: (i, 0)
    vec = lambda i: (i // per_b, 0, 0)
    full = lambda i: (0, 0)
    colblk = lambda i: (0, i)
    return pl.pallas_call(
        _outproj_router_kernel,
        grid=(nw // wps,),
        in_specs=[pl.BlockSpec((W, gw), row)] * 4 + [
            pl.BlockSpec((W, D), row),
            pl.BlockSpec((1, 1, D), vec),
            pl.BlockSpec(wo_bf.shape, full),
            pl.BlockSpec((1, D), full),
            pl.BlockSpec((1, 1, D), vec),
            pl.BlockSpec((1, 1, D), vec),
            pl.BlockSpec(rwT.shape, full),
            pl.BlockSpec(rb.shape, full)],
        out_specs=[pl.BlockSpec((W, D), row),
                   pl.BlockSpec((W, D), row),
                   pl.BlockSpec((TOP_K, W), colblk),
                   pl.BlockSpec((TOP_K, W), colblk),
                   pl.BlockSpec((wps, N_EXPERTS, 1), lambda i: (i, 0, 0))],
        out_shape=[jax.ShapeDtypeStruct((N, D), f32),
                   jax.ShapeDtypeStruct((N, D), bf16),
                   jax.ShapeDtypeStruct((TOP_K, N), i32),
                   jax.ShapeDtypeStruct((TOP_K, N), f32),
                   jax.ShapeDtypeStruct((nw, N_EXPERTS, 1), i32)],
        compiler_params=_cparams(("arbitrary",)),
        name="outproj_router",
    )(yc, ya, yp, ys, x, gate1, wo_bf, n2g, shift2, scale2, rwT, rb)


def _max_window_rows():
    return -(-(ROUTE_TILE * TOP_K + N_EXPERTS * (ROW_CHUNK - 1)) // 128) * 128


def _pack_pairs(lo, hi):
    lo_bits = lax.bitcast_convert_type(lo, jnp.uint32)
    hi_bits = lax.bitcast_convert_type(hi, jnp.uint32)
    return lax.bitcast_convert_type(hi_bits | (lo_bits >> 16), i32)


def _unpack_pairs(words):
    bits = lax.bitcast_convert_type(words, jnp.uint32)
    lo = lax.bitcast_convert_type(bits << 16, f32).astype(bf16)
    hi = lax.bitcast_convert_type(bits & jnp.uint32(0xFFFF0000), f32).astype(bf16)
    return lo, hi


def _dispatch_kernel(big_src_s, big_dst_s, wbig_s, small_src_s, small_dst_s, wsmall_s, wchunks_s,
                     tail_start_s, tail_chunks_s, n_used_s,
                     h_ref, dest_ref, xs_ref, sorted_ref, zero_ref, sem):
    w = pl.program_id(0)
    nw = pl.num_programs(0)
    W = ROUTE_TILE
    R = sorted_ref.shape[1]
    half = sorted_ref.shape[2]
    slot = w % DISPATCH_BUFFERS
    for r0 in range(0, R, DISPATCH_ROW_BLOCK):
        rio = r0 + lax.broadcasted_iota(i32, (DISPATCH_ROW_BLOCK, W), 0)
        hit = rio == dest_ref[0:1, :]
        for k in range(1, TOP_K):
            hit = hit | (rio == dest_ref[k:k + 1, :])
        onehot = jnp.where(hit, 1.0, 0.0).astype(bf16)
        rows = jnp.dot(onehot, h_ref[...], preferred_element_type=f32)
        sorted_ref[slot, r0:r0 + DISPATCH_ROW_BLOCK, :] = _pack_pairs(rows[:, :half], rows[:, half:])

    def rows_copy(s, local_row, global_row, rows):
        return pltpu.make_async_copy(
            sorted_ref.at[s, pl.ds(pl.multiple_of(local_row, ROW_CHUNK), rows), :],
            xs_ref.at[pl.ds(pl.multiple_of(global_row, ROW_CHUNK), rows), :], sem.at[s])

    def zero_copy(dst_row):
        return pltpu.make_async_copy(
            zero_ref.at[0:ROW_CHUNK, :],
            xs_ref.at[pl.ds(pl.multiple_of(dst_row, ROW_CHUNK), ROW_CHUNK), :], sem.at[0])

    def zero_tile_copy(tile):
        return pltpu.make_async_copy(
            zero_ref, xs_ref.at[pl.ds(pl.multiple_of(tile * FFN_TILE, FFN_TILE), FFN_TILE), :], sem.at[0])

    def drain(win):
        chunks = wchunks_s[win]

        @pl.when(chunks > 0)
        def _():
            rows = pl.multiple_of(chunks * ROW_CHUNK, ROW_CHUNK)
            pltpu.make_async_copy(xs_ref.at[pl.ds(0, rows), :], xs_ref.at[pl.ds(0, rows), :],
                                  sem.at[win % DISPATCH_BUFFERS]).wait()

    @pl.when(w >= DISPATCH_BUFFERS - 1)
    def _():
        drain(jnp.maximum(w - (DISPATCH_BUFFERS - 1), 0))

    for parity in range(DISPATCH_BUFFERS):
        @pl.when(slot == parity)
        def _():
            def big(q, carry):
                rows_copy(parity, big_dst_s[w * MAX_COMBINE_PIECES + q], big_src_s[w * MAX_COMBINE_PIECES + q],
                          COMBINE_COPY_ROWS).start()
                return carry
            lax.fori_loop(0, wbig_s[w], big, 0)

            def small(q, carry):
                rows_copy(parity, small_dst_s[w * MAX_COMBINE_CHUNKS + q], small_src_s[w * MAX_COMBINE_CHUNKS + q],
                          ROW_CHUNK).start()
                return carry
            lax.fori_loop(0, wsmall_s[w], small, 0)

    @pl.when(w == nw - 1)
    def _():
        for back in range(DISPATCH_BUFFERS - 2, -1, -1):
            @pl.when(w - back >= 0)
            def _(back=back):
                drain(jnp.maximum(w - back, 0))
        zero_ref[...] = jnp.zeros(zero_ref.shape, i32)

        def per_tail(e, total):
            n = tail_chunks_s[e]
            dst = tail_start_s[e]

            def issue(j, c):
                zero_copy(dst + j * ROW_CHUNK).start()
                return c
            lax.fori_loop(0, n, issue, 0)
            return total + n
        tails = lax.fori_loop(0, N_EXPERTS, per_tail, 0)

        def drain_tail(j, c):
            zero_copy(0).wait()
            return c
        lax.fori_loop(0, tails, drain_tail, 0)

        n_tiles = xs_ref.shape[0] // FFN_TILE

        def issue_tile(j, c):
            zero_tile_copy(j).start()
            return c
        lax.fori_loop(n_used_s[0], n_tiles, issue_tile, 0)

        def drain_tile(j, c):
            zero_tile_copy(0).wait()
            return c
        lax.fori_loop(n_used_s[0], n_tiles, drain_tile, 0)


def _dispatch(h2, destT, t, p_rows):
    N, D = h2.shape
    W = ROUTE_TILE
    nw = N // W
    R = _max_window_rows()
    grid_spec = pltpu.PrefetchScalarGridSpec(
        num_scalar_prefetch=10,
        grid=(nw,),
        in_specs=[pl.BlockSpec((W, D), lambda w, *_: (w, 0)),
                  pl.BlockSpec((TOP_K, W), lambda w, *_: (0, w))],
        out_specs=pl.BlockSpec(memory_space=pl.ANY),
        scratch_shapes=[pltpu.VMEM((DISPATCH_BUFFERS, R, D // 2), i32), pltpu.VMEM((FFN_TILE, D // 2), i32),
                        pltpu.SemaphoreType.DMA((DISPATCH_BUFFERS,))],
    )
    return pl.pallas_call(
        _dispatch_kernel,
        grid_spec=grid_spec,
        out_shape=jax.ShapeDtypeStruct((p_rows, D // 2), i32),
        compiler_params=_cparams(("arbitrary",)),
        name="expert_dispatch",
    )(t['big_src'], t['big_dst'], t['wbig'], t['small_src'], t['small_dst'], t['wsmall'], t['wchunks'],
      t['tail_start'], t['tail_chunks'], t['n_used'], h2, destT)


def _ffn_kernel(tile_expert_s, next_expert_s, n_used_s, x_ref, w1_hbm, b1_ref, w2_hbm, b2_ref, y_ref,
                w1f_ref, w2f_ref, w1b_ref, w2b_ref, sem, *, layer):
    i = pl.program_id(0)
    last = n_used_s[0] - 1
    expert = tile_expert_s[jnp.minimum(i, last)]
    prev_expert = tile_expert_s[jnp.minimum(jnp.maximum(i - 1, 0), last)]

    def weight_copies(e):
        return (pltpu.make_async_copy(w1_hbm.at[layer, e], w1f_ref, sem.at[0]),
                pltpu.make_async_copy(w2_hbm.at[layer, e], w2f_ref, sem.at[1]))

    @pl.when(i == 0)
    def _():
        for c in weight_copies(expert):
            c.start()

    @pl.when((i == 0) | (expert != prev_expert))
    def _():
        for c in weight_copies(expert):
            c.wait()
        w1b_ref[...] = w1f_ref[...].astype(bf16)
        w2b_ref[...] = w2f_ref[...].astype(bf16)
        nxt = next_expert_s[expert]

        @pl.when(nxt < N_EXPERTS)
        def _():
            for c in weight_copies(nxt):
                c.start()

    @pl.when(i < n_used_s[0])
    def _():
        dff = w2b_ref.shape[0]
        half = x_ref.shape[1]
        sub = x_ref.shape[0] // FFN_SUBTILES
        hidden = []
        for r in range(FFN_SUBTILES):
            x_lo, x_hi = _unpack_pairs(x_ref[r * sub:(r + 1) * sub, :])
            hidden.append(jnp.dot(x_lo, w1b_ref[0:half, :], preferred_element_type=f32)
                          + jnp.dot(x_hi, w1b_ref[half:2 * half, :], preferred_element_type=f32))
        for r in range(FFN_SUBTILES):
            hh = hidden[r] + b1_ref[0, 0]
            x_glu = jnp.minimum(hh[:, :dff], SWIGLU_LIMIT)
            x_lin = jnp.clip(hh[:, dff:], -SWIGLU_LIMIT, SWIGLU_LIMIT)
            act = x_glu * jax.nn.sigmoid(SWIGLU_ALPHA * x_glu) * (x_lin + 1.0)
            y = jnp.dot(act.astype(bf16), w2b_ref[...], preferred_element_type=f32) + b2_ref[0, 0]
            y = y.astype(bf16).astype(f32)
            y_ref[r * sub:(r + 1) * sub, :] = _pack_pairs(y[:, :half], y[:, half:])

    @pl.when(i >= n_used_s[0])
    def _():
        y_ref[...] = jnp.zeros(y_ref.shape, i32)


def _expert_ffn(xs, w1, b1, w2, b2, t, layer):
    P, half = xs.shape
    L, E, D, F2 = w1.shape
    tm = FFN_TILE
    nt = P // tm

    def tile(i, te, ne, nu):
        return (jnp.minimum(i, nu[0] - 1), 0)

    def expert4(i, te, ne, nu):
        return (layer, te[jnp.minimum(i, nu[0] - 1)], 0, 0)

    grid_spec = pltpu.PrefetchScalarGridSpec(
        num_scalar_prefetch=3,
        grid=(nt,),
        in_specs=[pl.BlockSpec((tm, half), tile),
                  pl.BlockSpec(memory_space=pl.ANY),
                  pl.BlockSpec((1, 1, 1, F2), expert4),
                  pl.BlockSpec(memory_space=pl.ANY),
                  pl.BlockSpec((1, 1, 1, D), expert4)],
        out_specs=pl.BlockSpec((tm, half), lambda i, te, ne, nu: (i, 0)),
        scratch_shapes=[pltpu.VMEM((D, F2), f32), pltpu.VMEM((F2 // 2, D), f32),
                        pltpu.VMEM((D, F2), bf16), pltpu.VMEM((F2 // 2, D), bf16),
                        pltpu.SemaphoreType.DMA((2,))],
    )
    return pl.pallas_call(
        functools.partial(_ffn_kernel, layer=layer),
        grid_spec=grid_spec,
        out_shape=jax.ShapeDtypeStruct((P, half), i32),
        compiler_params=_cparams(("arbitrary",)),
        name="expert_ffn",
    )(t['tile_expert'], t['next_expert'], t['n_used'], xs, w1, b1.reshape(L, E, 1, F2), w2,
      b2.reshape(L, E, 1, D))


def _combine_kernel(big_src_s, big_dst_s, wbig_s, small_src_s, small_dst_s, wsmall_s, wchunks_s,
                    ys_ref, dest_ref, gate_ref, x1_ref, g2_ref, fg_ref, o_ref, local_ref, sem, *, final):
    step = pl.program_id(0)
    n_steps = pl.num_programs(0)
    W = ROUTE_TILE
    wps = COMBINE_WINDOWS_PER_STEP
    R = local_ref.shape[2]
    slot = step % 2

    def rows_copy(s, u, src_row, dst_row, rows):
        return pltpu.make_async_copy(
            ys_ref.at[pl.ds(pl.multiple_of(src_row, ROW_CHUNK), rows), :],
            local_ref.at[s, u, pl.ds(pl.multiple_of(dst_row, ROW_CHUNK), rows), :], sem.at[s, u])

    def fetch(at_step, s):
        for u in range(wps):
            win = at_step * wps + u

            def big(q, carry, win=win, u=u):
                rows_copy(s, u, big_src_s[win * MAX_COMBINE_PIECES + q], big_dst_s[win * MAX_COMBINE_PIECES + q],
                          COMBINE_COPY_ROWS).start()
                return carry
            lax.fori_loop(0, wbig_s[win], big, 0)

            def small(q, carry, win=win, u=u):
                rows_copy(s, u, small_src_s[win * MAX_COMBINE_CHUNKS + q],
                          small_dst_s[win * MAX_COMBINE_CHUNKS + q], ROW_CHUNK).start()
                return carry
            lax.fori_loop(0, wsmall_s[win], small, 0)

    @pl.when(step == 0)
    def _():
        local_ref[...] = jnp.zeros(local_ref.shape, i32)
        fetch(0, 0)

    for parity in range(2):
        @pl.when((step + 1 < n_steps) & (slot == parity))
        def _():
            fetch(jnp.minimum(step + 1, n_steps - 1), 1 - parity)

    for u in range(wps):
        chunks = wchunks_s[step * wps + u]

        @pl.when(chunks > 0)
        def _(chunks=chunks, u=u):
            rows = pl.multiple_of(chunks * ROW_CHUNK, ROW_CHUNK)
            pltpu.make_async_copy(ys_ref.at[pl.ds(0, rows), :], local_ref.at[slot, u, pl.ds(0, rows), :],
                                  sem.at[slot, u]).wait()

    rio = lax.broadcasted_iota(i32, (R, W), 0)
    tn = (((0,), (0,)), ((), ()))
    weights, halves = [], []
    for u in range(wps):
        cols = slice(u * W, (u + 1) * W)
        wt = jnp.zeros((R, W), f32)
        for k in range(TOP_K):
            wt = jnp.where(rio == dest_ref[k:k + 1, cols], gate_ref[k:k + 1, cols], wt)
        weights.append(wt.astype(bf16))
        halves.append(_unpack_pairs(local_ref[slot, u]))
    for u in range(wps):
        y_lo, y_hi = halves[u]
        moe = jnp.concatenate([lax.dot_general(weights[u], y_lo, tn, preferred_element_type=f32),
                               lax.dot_general(weights[u], y_hi, tn, preferred_element_type=f32)], axis=1)
        rows = slice(u * W, (u + 1) * W)
        x2 = x1_ref[rows, :] + g2_ref[0] * moe
        if final:
            ms = jnp.mean(x2 * x2, axis=-1, keepdims=True)
            x2 = x2 * lax.rsqrt(ms + EPS) * fg_ref[...]
        o_ref[rows, :] = x2


def _combine(ys, dest, gates, x1, gate2, final_g, t, seq, final):
    N, D = x1.shape
    wps = COMBINE_WINDOWS_PER_STEP
    W = ROUTE_TILE * wps
    per_b = seq // W
    R = _max_window_rows()
    grid_spec = pltpu.PrefetchScalarGridSpec(
        num_scalar_prefetch=7,
        grid=(N // W,),
        in_specs=[pl.BlockSpec(memory_space=pl.ANY),
                  pl.BlockSpec((TOP_K, W), lambda w, *_: (0, w)),
                  pl.BlockSpec((TOP_K, W), lambda w, *_: (0, w)),
                  pl.BlockSpec((W, D), lambda w, *_: (w, 0)),
                  pl.BlockSpec((1, 1, D), lambda w, *_: (w // per_b, 0, 0)),
                  pl.BlockSpec((1, D), lambda w, *_: (0, 0))],
        out_specs=pl.BlockSpec((W, D), lambda w, *_: (w, 0)),
        scratch_shapes=[pltpu.VMEM((2, wps, R, D // 2), i32), pltpu.SemaphoreType.DMA((2, wps))],
    )
    return pl.pallas_call(
        functools.partial(_combine_kernel, final=final),
        grid_spec=grid_spec,
        out_shape=jax.ShapeDtypeStruct((N, D), f32),
        compiler_params=_cparams(("arbitrary",)),
        name="expert_combine",
    )(t['big_src'], t['big_dst'], t['wbig'], t['small_src'], t['small_dst'], t['wsmall'], t['wchunks'],
      ys, dest, gates, x1, gate2, final_g)


def _routing_tables(padded, n_tiles):
    nw, E = padded.shape
    lstart = jnp.cumsum(padded, axis=1) - padded
    tot = jnp.sum(padded, axis=0)
    region = (tot + FFN_TILE - 1) // FFN_TILE * FFN_TILE
    region_end = jnp.cumsum(region)
    region_start = region_end - region
    gbase = region_start[None, :] + jnp.cumsum(padded, axis=0) - padded
    nchunk = padded // ROW_CHUNK
    tail_start = region_start + tot
    tail_chunks = (region - tot) // ROW_CHUNK
    n_used = (region_end[-1] // FFN_TILE).astype(i32).reshape(1)
    tile_row = jnp.arange(n_tiles, dtype=i32) * FFN_TILE
    tile_expert = jnp.minimum(
        jnp.sum((region_end[None, :] <= tile_row[:, None]).astype(i32), axis=1), E - 1).astype(i32)
    eidx = jnp.arange(E, dtype=i32)
    later_nonempty = (eidx[None, :] > eidx[:, None]) & (region[None, :] > 0)
    next_expert = jnp.min(jnp.where(later_nonempty, eidx[None, :], E), axis=1)
    flat = lambda a: a.reshape(-1).astype(i32)

    def copy_list(counts, window_rows, expert_rows, step, max_items):
        ends = jnp.cumsum(counts, axis=1)
        slot = jnp.arange(max_items, dtype=i32)
        owner = jnp.sum((ends[:, None, :] <= slot[None, :, None]).astype(i32), axis=2)
        hit = owner[:, :, None] == eidx[None, None, :]
        pick = lambda a: jnp.sum(jnp.where(hit, a[:, None, :], 0), axis=2)
        offset = step * (slot[None, :] - pick(ends - counts))
        return flat(pick(window_rows) + offset), flat(pick(expert_rows) + offset), flat(ends[:, -1])

    nbig = padded // COMBINE_COPY_ROWS
    big_dst, big_src, wbig = copy_list(nbig, lstart, gbase, COMBINE_COPY_ROWS, MAX_COMBINE_PIECES)
    rest = nbig * COMBINE_COPY_ROWS
    small_dst, small_src, wsmall = copy_list(nchunk - nbig * (COMBINE_COPY_ROWS // ROW_CHUNK),
                                             lstart + rest, gbase + rest, ROW_CHUNK, MAX_COMBINE_CHUNKS)
    return dict(next_expert=flat(next_expert), wchunks=flat(jnp.sum(nchunk, axis=1)),
                big_src=big_src, big_dst=big_dst, wbig=wbig,
                small_src=small_src, small_dst=small_dst, wsmall=wsmall,
                tail_start=flat(tail_start), tail_chunks=flat(tail_chunks),
                n_used=n_used, tile_expert=tile_expert)


def _block_diag(w):
    g, a, b = w.shape
    out = jnp.zeros((g * a, g * b), w.dtype)
    for i in range(g):
        out = out.at[i * a:(i + 1) * a, i * b:(i + 1) * b].set(w[i])
    return out


def kernel(x, c, w_mod, b_mod, norm1_g, w_in, conv_w, conv_b, conv_norm_g, conv_norm_b, rel_bias, pool_w, pool_scale, sgu_norm_g, sgu_norm_b, sgu_w, sgu_b, mix_out_g, w_out, norm2_g, router_w, router_b, exp_w1, exp_b1, exp_w2, exp_b2, final_norm_g):
    B, S, D = x.shape
    L = w_mod.shape[0]
    N = B * S
    nw = N // ROUTE_TILE
    assert S % SEQ_TILE == 0 and S % MOBA_BLOCK == 0 and N % ROUTE_TILE == 0 and S % ROUTE_TILE == 0
    p_bound = N * TOP_K + nw * N_EXPERTS * (ROW_CHUNK - 1) + N_EXPERTS * (FFN_TILE - 1)
    n_tiles = -(-p_bound // FFN_TILE)
    p_rows = n_tiles * FFN_TILE

    mod = _modulation(c, w_mod, b_mod)
    bias_tiles = _bias_tiles(rel_bias)
    far_bucket = int(_t5_bucket_table(MOBA_BLOCK + 2)[MOBA_BLOCK + 1])
    assert far_bucket == int(_t5_bucket_table(S + 1)[S])
    far_bias = rel_bias[far_bucket].astype(f32)
    row = lambda a: a.reshape(1, -1)
    for l in range(L):
        m6 = mod[l].reshape(B, 6, 1, D)
        shift1, scale1, gate1, shift2, scale2, gate2 = (m6[:, j] for j in range(6))
        mg = row(mix_out_g[l])
        qT, k, vT, kmean, yc, yp, ys = _project_and_mix(
            x, shift1, scale1, row(norm1_g[l]), w_in[l].astype(bf16),
            conv_w[l], row(conv_b[l]), row(conv_norm_g[l]), row(conv_norm_b[l]),
            _block_diag(pool_w[l]).astype(bf16), row(pool_scale[l]),
            row(sgu_norm_g[l]), row(sgu_norm_b[l]), sgu_w[l],
            jnp.repeat(sgu_b[l].T, HEAD_DIM, axis=1), mg)
        ya = _moba_attention(qT, k, vT, kmean.reshape(B, -1, GROUP_WIDTH), bias_tiles, far_bias, mg)
        flat = lambda a: a.reshape(N, -1)
        x1, h2, destT, gateT, padded = _outproj_router(
            flat(yc), flat(ya), flat(yp), flat(ys), x.reshape(N, D), gate1, w_out[l].astype(bf16),
            row(norm2_g[l]), shift2, scale2, router_w[l].T, router_b[l].reshape(-1, 1), S)
        t = _routing_tables(padded.reshape(nw, N_EXPERTS), n_tiles)
        xs = _dispatch(h2, destT, t, p_rows)
        ysort = _expert_ffn(xs, exp_w1, exp_b1, exp_w2, exp_b2, t, l)
        x = _combine(ysort, destT, gateT, x1, gate2, row(final_norm_g), t, S,
                     final=(l == L - 1)).reshape(B, S, D)
    return x
```

```python
import functools
import math

import numpy as np
import jax
import jax.numpy as jnp
from jax import lax
from jax.experimental import pallas as pl
from jax.experimental.pallas import tpu as pltpu

f32, bf16, i32 = jnp.float32, jnp.bfloat16, jnp.int32

GROUP_WIDTH = 256
HEADS = 4
HEAD_DIM = 64
V_ROWS = HEAD_DIM + 16
LOG2E = math.log2(math.e)
CONV_WIDTH = 31
MOBA_BLOCK = 256
MOBA_TOPK = 3
Q_CHUNK = 128
N_BUCKETS = 32
T5_MAX_DISTANCE = 128
POOL_WINDOWS = (2, 4, 8, 16)
SGU_CHUNK = 128
N_EXPERTS = 32
TOP_K = 4
SWIGLU_LIMIT = 7.0
SWIGLU_ALPHA = 1.702
EPS = 1e-6

SUBLANES = 8
LANES = 128
ATTN_QUERY_GROUP = 256
HALO = 32
SEQ_TILE = 1024
ROUTE_TILE = 256
ROUTE_WINDOWS_PER_STEP = 4
ROW_CHUNK = 8
DISPATCH_ROW_BLOCK = 256
DISPATCH_BUFFERS = 3
COMBINE_COPY_ROWS = 32
COMBINE_WINDOWS_PER_STEP = 4
MAX_COMBINE_PIECES = (ROUTE_TILE * TOP_K + N_EXPERTS * (ROW_CHUNK - 1)) // COMBINE_COPY_ROWS
MAX_COMBINE_CHUNKS = N_EXPERTS * (COMBINE_COPY_ROWS // ROW_CHUNK - 1)
FFN_TILE = 512
FFN_SUBTILES = 2
NEG = -1e30
VMEM_LIMIT = 56 * 1024 * 1024


def _cparams(sem):
    return pltpu.CompilerParams(dimension_semantics=sem, vmem_limit_bytes=VMEM_LIMIT)


def _split_bf16(a):
    hi = a.astype(bf16)
    lo = (a - hi.astype(f32)).astype(bf16)
    return hi, lo


def _mod_kernel(c_ref, w_ref, b_ref, o_ref):
    c = c_ref[...]
    cond = c * jax.nn.sigmoid(c)
    o_ref[0] = jnp.dot(cond, w_ref[0], preferred_element_type=f32,
                       precision=lax.Precision.HIGHEST) + b_ref[0]


def _modulation(c, w_mod, b_mod):
    L, D, M = w_mod.shape
    B = c.shape[0]
    tn = 1536
    return pl.pallas_call(
        _mod_kernel,
        grid=(L, M // tn),
        in_specs=[pl.BlockSpec((B, D), lambda l, j: (0, 0)),
                  pl.BlockSpec((1, D, tn), lambda l, j: (l, 0, j)),
                  pl.BlockSpec((1, 1, tn), lambda l, j: (l, 0, j))],
        out_specs=pl.BlockSpec((1, B, tn), lambda l, j: (l, 0, j)),
        out_shape=jax.ShapeDtypeStruct((L, B, M), f32),
        compiler_params=_cparams(("arbitrary", "arbitrary")),
        name="modulation",
    )(c, w_mod, b_mod.reshape(L, 1, M))


def _project_mix_kernel(x_ref, sh_ref, sc_ref, g_ref, w_ref,
                        cw_ref, cb_ref, cng_ref, cnb_ref, pw_ref, psc_ref, sg_ref, sb_ref, sw_ref, sbias_ref, mg_ref,
                        qT_ref, k_ref, vT_ref, km_ref, yc_ref, yp_ref, ys_ref,
                        gext, gshift, zext, s2, s4, s8):
    first = pl.program_id(1) == 0

    @pl.when(first)
    def _():
        gext[0:HALO, :] = jnp.zeros((HALO, GROUP_WIDTH), f32)
        zext[0:HALO, :] = jnp.zeros((HALO, GROUP_WIDTH), f32)

    @pl.when(jnp.logical_not(first))
    def _():
        gext[0:HALO, :] = gext[SEQ_TILE:SEQ_TILE + HALO, :]
        zext[0:HALO, :] = zext[SEQ_TILE:SEQ_TILE + HALO, :]

    x = x_ref[0]
    ms = jnp.mean(x * x, axis=-1, keepdims=True)
    h = x * lax.rsqrt(ms + EPS) * g_ref[...]
    h = h * (1.0 + sc_ref[0]) + sh_ref[0]
    gw = GROUP_WIDTH
    hb = h.astype(bf16)
    pc = jnp.dot(hb, w_ref[:, 0:2 * gw], preferred_element_type=f32)
    qkv = jnp.dot(hb, w_ref[:, 2 * gw:5 * gw], preferred_element_type=f32)
    rest = jnp.dot(hb, w_ref[:, 5 * gw:8 * gw], preferred_element_type=f32)
    _local_mixers(pc, rest[:, 0:gw], rest[:, gw:3 * gw],
                  cw_ref, cb_ref, cng_ref, cnb_ref, pw_ref, psc_ref, sg_ref, sb_ref, sw_ref, sbias_ref, mg_ref,
                  yc_ref, yp_ref, ys_ref, gext, gshift, zext, s2, s4, s8)
    proj = jnp.concatenate([pc, qkv], axis=1)
    q = proj[:, 2 * gw:3 * gw] * (HEAD_DIM ** -0.5 * LOG2E)
    qT_ref[0] = q.T.astype(bf16)
    kk = proj[:, 3 * gw:4 * gw]
    for h in range(HEADS):
        k_ref[0, h] = kk[:, h * HEAD_DIM:(h + 1) * HEAD_DIM].astype(bf16)
    for j in range(SEQ_TILE // MOBA_BLOCK):
        km_ref[0, 0, j:j + 1, :] = jnp.mean(kk[j * MOBA_BLOCK:(j + 1) * MOBA_BLOCK], axis=0, keepdims=True)
    for j in range(SEQ_TILE // MOBA_BLOCK):
        vT = proj[j * MOBA_BLOCK:(j + 1) * MOBA_BLOCK, 4 * gw:5 * gw].T.astype(bf16)
        for h in range(HEADS):
            vT_ref[0, j, h] = jnp.concatenate(
                [vT[h * HEAD_DIM:(h + 1) * HEAD_DIM, :], jnp.ones((V_ROWS - HEAD_DIM, MOBA_BLOCK), bf16)], axis=0)


def _group_rms(y, g):
    return y * lax.rsqrt(jnp.mean(y * y, axis=-1, keepdims=True) + EPS) * g


def _local_mixers(pc, z, zz,
                  cw_ref, cb_ref, cng_ref, cnb_ref, pw_ref, psc_ref,
                  sg_ref, sb_ref, sw_ref, sbias_ref, mg_ref,
                  yc_ref, yp_ref, ys_ref,
                  gext, gshift, zext, s2, s4, s8):
    i = pl.program_id(1)
    ts = SEQ_TILE
    gw = GROUP_WIDTH
    lane = lax.broadcasted_iota(i32, (1, gw), 1)

    g = pc[:, :gw] * jax.nn.sigmoid(pc[:, gw:])
    gext[HALO:HALO + ts, :] = g
    span = ts + HALO - SUBLANES
    for a in range(1, SUBLANES):
        gshift[a - 1, 0:span, :] = gext[pl.ds(a, span), :]
    acc = jnp.zeros((ts, gw), f32)
    for j in range(CONV_WIDTH):
        b, a = divmod(HALO - (CONV_WIDTH - 1) + j, SUBLANES)
        tap = gext[pl.ds(SUBLANES * b, ts), :] if a == 0 else gshift[a - 1, pl.ds(SUBLANES * b, ts), :]
        acc = acc + cw_ref[j:j + 1, :] * tap
    y = acc + cb_ref[...]
    r = lax.broadcasted_iota(i32, (gw, gw), 0) // HEAD_DIM
    c = lax.broadcasted_iota(i32, (gw, gw), 1) // HEAD_DIM
    avg = jnp.where(r == c, 1.0 / HEAD_DIM, 0.0).astype(bf16)

    def head_mean(t):
        hi, lo = _split_bf16(t)
        return (jnp.dot(hi, avg, preferred_element_type=f32)
                + jnp.dot(lo, avg, preferred_element_type=f32))

    mu = head_mean(y)
    yc = y - mu
    var = head_mean(yc * yc)
    yn = yc * lax.rsqrt(var + EPS) * cng_ref[...] + cnb_ref[...]
    yconv = yn * jax.nn.sigmoid(yn)
    yc_ref[0] = _group_rms(yconv, mg_ref[:, 0:gw]).astype(bf16)

    zext[HALO:HALO + ts, :] = z
    n2, n4, n8 = ts + 14, ts + 12, ts + 8
    s2[0:n2, :] = zext[pl.ds(HALO - 14, n2), :] + zext[pl.ds(HALO - 15, n2), :]
    s4[0:n4, :] = s2[pl.ds(2, n4), :] + s2[pl.ds(0, n4), :]
    s8[0:n8, :] = s4[pl.ds(4, n8), :] + s4[pl.ds(0, n8), :]
    w2 = s2[pl.ds(14, ts), :]
    w4 = s4[pl.ds(12, ts), :]
    w8 = s8[pl.ds(8, ts), :]
    w16 = w8 + s8[pl.ds(0, ts), :]
    tpos = (i * ts + lax.broadcasted_iota(i32, (ts, 1), 0) + 1).astype(f32)
    grp = lane // (gw // len(POOL_WINDOWS))
    pooled = jnp.zeros((ts, gw), f32)
    for gi, (w, sw) in enumerate(zip(POOL_WINDOWS, (w2, w4, w8, w16))):
        pooled = jnp.where(grp == gi, sw / jnp.minimum(tpos, float(w)), pooled)
    pooled = pooled - z
    yp = jnp.dot(pooled.astype(bf16), pw_ref[...], preferred_element_type=f32) * psc_ref[...]
    yp_ref[0] = _group_rms(yp, mg_ref[:, 2 * gw:3 * gw]).astype(bf16)

    zz = 0.5 * zz * (1.0 + lax.erf(zz * (1.0 / math.sqrt(2.0))))
    u = zz[:, :gw]
    v = zz[:, gw:]
    vm = jnp.mean(v, axis=-1, keepdims=True)
    vc = v - vm
    vv = jnp.mean(vc * vc, axis=-1, keepdims=True)
    vn = (vc * lax.rsqrt(vv + EPS) * sg_ref[...] + sb_ref[...]).astype(bf16)
    li = lax.broadcasted_iota(i32, (SGU_CHUNK, SGU_CHUNK), 0)
    lj = lax.broadcasted_iota(i32, (SGU_CHUNK, SGU_CHUNK), 1)
    head_of_lane = lane // HEAD_DIM
    wts = [jnp.where(li >= lj, sw_ref[h], 0.0).astype(bf16) for h in range(HEADS)]
    outs = []
    for n in range(ts // SGU_CHUNK):
        vch = vn[n * SGU_CHUNK:(n + 1) * SGU_CHUNK]
        mixed = sbias_ref[...]
        for h in range(HEADS):
            mh = jnp.dot(wts[h], vch, preferred_element_type=f32)
            mixed = mixed + jnp.where(head_of_lane == h, mh, 0.0)
        outs.append(u[n * SGU_CHUNK:(n + 1) * SGU_CHUNK] * mixed)
    ysgu = jnp.concatenate(outs, axis=0)
    ys_ref[0] = _group_rms(ysgu, mg_ref[:, 3 * gw:4 * gw]).astype(bf16)


def _project_and_mix(x, shift, scale, g, w_bf, cw, cb, cng, cnb, pw_bd, psc, sg, sb, sw, sbias, mg):
    B, S, D = x.shape
    gw = GROUP_WIDTH
    ts = SEQ_TILE
    nt = S // ts
    row = lambda b, i: (b, i, 0)
    col = lambda b, i: (b, 0, i)
    vec = lambda b, i: (b, 0, 0)
    full2 = lambda b, i: (0, 0)
    full3 = lambda b, i: (0, 0, 0)
    mixed = jax.ShapeDtypeStruct((B, S, gw), bf16)
    return pl.pallas_call(
        _project_mix_kernel,
        grid=(B, nt),
        in_specs=[pl.BlockSpec((1, ts, D), row),
                  pl.BlockSpec((1, 1, D), vec),
                  pl.BlockSpec((1, 1, D), vec),
                  pl.BlockSpec((1, D), full2),
                  pl.BlockSpec(w_bf.shape, full2),
                  pl.BlockSpec(cw.shape, full2), pl.BlockSpec(cb.shape, full2),
                  pl.BlockSpec(cng.shape, full2), pl.BlockSpec(cnb.shape, full2),
                  pl.BlockSpec(pw_bd.shape, full2), pl.BlockSpec(psc.shape, full2),
                  pl.BlockSpec(sg.shape, full2), pl.BlockSpec(sb.shape, full2),
                  pl.BlockSpec(sw.shape, full3), pl.BlockSpec(sbias.shape, full2),
                  pl.BlockSpec(mg.shape, full2)],
        out_specs=[pl.BlockSpec((1, gw, ts), col),
                   pl.BlockSpec((1, HEADS, ts, HEAD_DIM), lambda b, i: (b, 0, i, 0)),
                   pl.BlockSpec((1, ts // MOBA_BLOCK, HEADS, V_ROWS, MOBA_BLOCK), lambda b, i: (b, i, 0, 0, 0)),
                   pl.BlockSpec((1, 1, ts // MOBA_BLOCK, gw), lambda b, i: (b, i, 0, 0)),
                   pl.BlockSpec((1, ts, gw), row), pl.BlockSpec((1, ts, gw), row), pl.BlockSpec((1, ts, gw), row)],
        out_shape=[jax.ShapeDtypeStruct((B, gw, S), bf16),
                   jax.ShapeDtypeStruct((B, HEADS, S, HEAD_DIM), bf16),
                   jax.ShapeDtypeStruct((B, S // MOBA_BLOCK, HEADS, V_ROWS, MOBA_BLOCK), bf16),
                   jax.ShapeDtypeStruct((B, nt, ts // MOBA_BLOCK, gw), f32),
                   mixed, mixed, mixed],
        scratch_shapes=[pltpu.VMEM((ts + HALO, gw), f32), pltpu.VMEM((SUBLANES - 1, ts + HALO, gw), f32),
                        pltpu.VMEM((ts + HALO, gw), f32),
                        pltpu.VMEM((ts + 16, gw), f32), pltpu.VMEM((ts + 16, gw), f32),
                        pltpu.VMEM((ts + 16, gw), f32)],
        compiler_params=_cparams(("arbitrary", "arbitrary")),
        name="project_and_mix",
    )(x, shift, scale, g, w_bf, cw, cb, cng, cnb, pw_bd, psc, sg, sb, sw, sbias, mg)


def _t5_bucket_table(max_dist):
    d = np.arange(max_dist, dtype=np.int64)
    max_exact = N_BUCKETS // 2
    nf = np.maximum(d, 1).astype(np.float32)
    large = max_exact + (np.log(nf / np.float32(max_exact)) / np.float32(math.log(T5_MAX_DISTANCE / max_exact))
                         * np.float32(N_BUCKETS - max_exact)).astype(np.int32)
    large = np.minimum(large, N_BUCKETS - 1)
    return np.where(d < max_exact, d, large).astype(np.int32)


_TILE_BASES = (0, MOBA_BLOCK)


def _bias_kernel(tab_ref, o_ref):
    blk, qc = MOBA_BLOCK, MOBA_BLOCK
    table = _t5_bucket_table(2 * blk + qc)
    first = [int(np.argmax(table >= b)) for b in range(N_BUCKETS)]
    j = lax.broadcasted_iota(i32, (blk, qc), 0)
    q = lax.broadcasted_iota(i32, (blk, qc), 1)
    for t, base in enumerate(_TILE_BASES):
        d = base + q - j
        lo, hi = max(base - (blk - 1), 0), base + qc - 1
        for h in range(HEADS):
            val = jnp.full((blk, qc), tab_ref[h], f32)
            for b in range(1, N_BUCKETS):
                if first[b] > hi:
                    continue
                if first[b] <= lo:
                    val = jnp.full((blk, qc), tab_ref[b * HEADS + h], f32)
                else:
                    val = jnp.where(d >= first[b], tab_ref[b * HEADS + h], val)
            o_ref[t, h] = jnp.where(d >= 0, val * LOG2E, NEG)


def _bias_tiles(rel_bias):
    return pl.pallas_call(
        _bias_kernel,
        in_specs=[pl.BlockSpec(memory_space=pltpu.SMEM)],
        out_shape=jax.ShapeDtypeStruct((len(_TILE_BASES), HEADS, MOBA_BLOCK, MOBA_BLOCK), f32),
        name="bias_tiles",
    )(rel_bias.astype(f32).reshape(-1))


def _attn_kernel(far_ref, qT_ref, k_ref, vT_ref, km_ref, bias_ref, mg_ref, o_ref, mask_ref, sa_ref, sb_ref):
    own = pl.program_id(1)
    blk, gw, hd = MOBA_BLOCK, GROUP_WIDTH, HEAD_DIM
    nb = km_ref.shape[1]

    nio = lax.broadcasted_iota(i32, (nb, blk), 0)
    past = nio < own
    km = km_ref[0]
    q_heads = []
    for h in range(HEADS):
        qh = qT_ref[0, h * hd:(h + 1) * hd, :]
        q_heads.append(qh)
        km_hi, km_lo = _split_bf16(km[:, h * hd:(h + 1) * hd])
        gate = (jnp.dot(km_hi, qh, preferred_element_type=f32)
                + jnp.dot(km_lo, qh, preferred_element_type=f32))
        gate = jnp.where(past, gate, -jnp.inf)
        picked = jnp.zeros((nb, blk), jnp.bool_)
        for _ in range(MOBA_TOPK):
            top = jnp.max(gate, axis=0, keepdims=True)
            first = jnp.min(jnp.where(gate == top, nio, nb), axis=0, keepdims=True)
            hit = nio == first
            picked = picked | hit
            gate = jnp.where(hit, -jnp.inf, gate)
        sel = picked & past
        mask_ref[0, h] = jnp.where(sel, 0.0, NEG)
        mask_ref[1, h] = jnp.where(sel & (nio < own - 1), far_ref[h] * LOG2E, NEG)

    def qk(n, h):
        kb = k_ref[0, h, pl.ds(pl.multiple_of(n * blk, blk), blk), :]
        return jnp.dot(kb, q_heads[h], preferred_element_type=f32)

    def far_scores(n, h):
        return qk(n, h) + mask_ref[1, h, pl.ds(n, 1), :]

    lane_groups = blk // ATTN_QUERY_GROUP

    def update(state, scores, n):
        out = []
        for h in range(HEADS):
            vb = vT_ref[0, n, h]
            for c in range(lane_groups):
                u = h * lane_groups + c
                m, acc = state[2 * u:2 * u + 2]
                s = scores[h][:, c * ATTN_QUERY_GROUP:(c + 1) * ATTN_QUERY_GROUP]
                m_new = jnp.maximum(m, jnp.max(s, axis=0, keepdims=True))
                alpha = jnp.exp2(m - m_new)
                p = jnp.exp2(s - m_new)
                acc = acc * alpha + jnp.dot(vb, p.astype(bf16), preferred_element_type=f32)
                out += [m_new, acc]
        return tuple(out)

    adj = jnp.maximum(own - 1, 0)
    n_far = jnp.maximum(own - 1, 0)
    s_own = [qk(own, h) + bias_ref[0, h] for h in range(HEADS)]
    s_adj = [qk(adj, h) + bias_ref[1, h] + mask_ref[0, h, pl.ds(adj, 1), :] for h in range(HEADS)]
    for h in range(HEADS):
        sa_ref[h] = far_scores(0, h)

    state = []
    for h in range(HEADS):
        vb = jnp.concatenate([vT_ref[0, own, h], vT_ref[0, adj, h]], axis=1)
        for c in range(lane_groups):
            so = s_own[h][:, c * ATTN_QUERY_GROUP:(c + 1) * ATTN_QUERY_GROUP]
            sj = s_adj[h][:, c * ATTN_QUERY_GROUP:(c + 1) * ATTN_QUERY_GROUP]
            m0 = jnp.maximum(jnp.max(so, axis=0, keepdims=True), jnp.max(sj, axis=0, keepdims=True))
            p = jnp.concatenate([jnp.exp2(so - m0).astype(bf16), jnp.exp2(sj - m0).astype(bf16)], axis=0)
            state += [m0, jnp.dot(vb, p, preferred_element_type=f32)]
    state = tuple(state)

    def body(i, state):
        first = jnp.minimum(2 * i, nb - 1)
        second = jnp.minimum(2 * i + 1, nb - 1)
        third = jnp.minimum(2 * i + 2, nb - 1)
        for h in range(HEADS):
            sb_ref[h] = far_scores(second, h)
        state = update(state, [sa_ref[h] for h in range(HEADS)], first)
        for h in range(HEADS):
            sa_ref[h] = far_scores(third, h)
        return update(state, [sb_ref[h] for h in range(HEADS)], second)

    fin = lax.fori_loop(0, (n_far + 1) // 2, body, state)

    def normalised(acc):
        return acc[0:hd, :] / acc[hd:hd + 1, :]

    outT = jnp.concatenate(
        [jnp.concatenate([normalised(fin[2 * (h * lane_groups + c) + 1]) for c in range(lane_groups)], axis=1)
         for h in range(HEADS)], axis=0)
    o_ref[0] = _group_rms(outT.T, mg_ref[:, gw:2 * gw]).astype(bf16)


def _moba_attention(qT, k, vT, kmean, bias_tiles, far_bias, mg):
    B, _, S, _ = k.shape
    gw, blk = GROUP_WIDTH, MOBA_BLOCK
    nb = kmean.shape[1]
    return pl.pallas_call(
        _attn_kernel,
        grid=(B, nb),
        in_specs=[pl.BlockSpec(memory_space=pltpu.SMEM),
                  pl.BlockSpec((1, gw, blk), lambda b, c: (b, 0, c)),
                  pl.BlockSpec((1, HEADS, S, HEAD_DIM), lambda b, c: (b, 0, 0, 0)),
                  pl.BlockSpec((1, nb, HEADS, V_ROWS, blk), lambda b, c: (b, 0, 0, 0, 0)),
                  pl.BlockSpec((1, nb, gw), lambda b, c: (b, 0, 0)),
                  pl.BlockSpec(bias_tiles.shape, lambda b, c: (0, 0, 0, 0)),
                  pl.BlockSpec(mg.shape, lambda b, c: (0, 0))],
        out_specs=pl.BlockSpec((1, blk, gw), lambda b, c: (b, c, 0)),
        out_shape=jax.ShapeDtypeStruct((B, S, gw), bf16),
        scratch_shapes=[pltpu.VMEM((2, HEADS, nb, blk), f32),
                        pltpu.VMEM((HEADS, blk, blk), f32), pltpu.VMEM((HEADS, blk, blk), f32)],
        compiler_params=_cparams(("arbitrary", "arbitrary")),
        name="moba_attention",
    )(far_bias, qT, k, vT, kmean, bias_tiles, mg)


def _outproj_router_kernel(yc_ref, ya_ref, yp_ref, ys_ref, x_ref, g1_ref, wo_ref, n2_ref, sh_ref, sc_ref,
                           rw_ref, rb_ref,
                           x1_ref, h2_ref, dest_ref, gate_ref, pad_ref):
    W = ROUTE_TILE
    windows = range(ROUTE_WINDOWS_PER_STEP)
    rows = [slice(u * W, (u + 1) * W) for u in windows]
    mixed = [jnp.concatenate([r[rows[u], :] for r in (yc_ref, ya_ref, yp_ref, ys_ref)], axis=1) for u in windows]
    projected = [jnp.dot(m, wo_ref[...], preferred_element_type=f32) for m in mixed]

    his, los = [], []
    for u in windows:
        x1 = x_ref[rows[u], :] + g1_ref[0] * projected[u]
        x1_ref[rows[u], :] = x1
        ms = jnp.mean(x1 * x1, axis=-1, keepdims=True)
        h = x1 * lax.rsqrt(ms + EPS) * n2_ref[...]
        h = h * (1.0 + sc_ref[0]) + sh_ref[0]
        h_hi, h_lo = _split_bf16(h)
        h2_ref[rows[u], :] = h_hi
        his.append(h_hi)
        los.append(h_lo)

    nt = (((1,), (1,)), ((), ()))
    rw_hi, rw_lo = _split_bf16(rw_ref[...])
    logits = [(lax.dot_general(rw_hi, his[u], nt, preferred_element_type=f32)
               + lax.dot_general(rw_hi, los[u], nt, preferred_element_type=f32)
               + lax.dot_general(rw_lo, his[u], nt, preferred_element_type=f32)) + rb_ref[...] for u in windows]

    eio = lax.broadcasted_iota(i32, (N_EXPERTS, W), 0)
    sels, multis = [], []
    for u in windows:
        work = logits[u]
        vals, sel_u = [], []
        for k in range(TOP_K):
            m = jnp.max(work, axis=0, keepdims=True)
            idx = jnp.min(jnp.where(work == m, eio, N_EXPERTS), axis=0, keepdims=True)
            sel = eio == idx
            vals.append(m)
            sel_u.append(sel)
            work = jnp.where(sel, -jnp.inf, work)
        exps = [jnp.exp(v - vals[0]) for v in vals]
        denom = exps[0] + exps[1] + exps[2] + exps[3]
        for k in range(TOP_K):
            gate_ref[k:k + 1, rows[u]] = exps[k] / denom
        multi = jnp.zeros((N_EXPERTS, W), f32)
        for sel in sel_u:
            multi = multi + sel.astype(f32)
        sels.append(sel_u)
        multis.append(multi)

    before = (lax.broadcasted_iota(i32, (W, W), 0) < lax.broadcasted_iota(i32, (W, W), 1)).astype(bf16)
    earlier = [jnp.dot(multis[u].astype(bf16), before, preferred_element_type=f32) for u in windows]

    lower = (lax.broadcasted_iota(i32, (N_EXPERTS, N_EXPERTS), 1)
             < lax.broadcasted_iota(i32, (N_EXPERTS, N_EXPERTS), 0)).astype(bf16)
    seg_start = []
    for u in windows:
        cnt = jnp.sum(multis[u], axis=1, keepdims=True).astype(i32)
        padded = (cnt + (ROW_CHUNK - 1)) // ROW_CHUNK * ROW_CHUNK
        pad_ref[u] = padded
        seg_start.append(jnp.dot(lower, jnp.broadcast_to(padded.astype(f32), (N_EXPERTS, W)).astype(bf16),
                                 preferred_element_type=f32))
    for u in windows:
        row = seg_start[u] + earlier[u]
        for k in range(TOP_K):
            dest_ref[k:k + 1, rows[u]] = jnp.sum(jnp.where(sels[u][k], row, 0.0), axis=0,
                                                 keepdims=True).astype(i32)


def _outproj_router(yc, ya, yp, ys, x, gate1, wo_bf, n2g, shift2, scale2, rwT, rb, seq):
    N, D = x.shape
    gw = GROUP_WIDTH
    nw = N // ROUTE_TILE
    wps = ROUTE_WINDOWS_PER_STEP
    W = ROUTE_TILE * wps
    per_b = seq // W
    row = lambda i: (i, 0)
    vec = lambda i: (i // per_b, 0, 0)
    full = lambda i: (0, 0)
    colblk = lambda i: (0, i)
    return pl.pallas_call(
        _outproj_router_kernel,
        grid=(nw // wps,),
        in_specs=[pl.BlockSpec((W, gw), row)] * 4 + [
            pl.BlockSpec((W, D), row),
            pl.BlockSpec((1, 1, D), vec),
            pl.BlockSpec(wo_bf.shape, full),
            pl.BlockSpec((1, D), full),
            pl.BlockSpec((1, 1, D), vec),
            pl.BlockSpec((1, 1, D), vec),
            pl.BlockSpec(rwT.shape, full),
            pl.BlockSpec(rb.shape, full)],
        out_specs=[pl.BlockSpec((W, D), row),
                   pl.BlockSpec((W, D), row),
                   pl.BlockSpec((TOP_K, W), colblk),
                   pl.BlockSpec((TOP_K, W), colblk),
                   pl.BlockSpec((wps, N_EXPERTS, 1), lambda i: (i, 0, 0))],
        out_shape=[jax.ShapeDtypeStruct((N, D), f32),
                   jax.ShapeDtypeStruct((N, D), bf16),
                   jax.ShapeDtypeStruct((TOP_K, N), i32),
                   jax.ShapeDtypeStruct((TOP_K, N), f32),
                   jax.ShapeDtypeStruct((nw, N_EXPERTS, 1), i32)],
        compiler_params=_cparams(("arbitrary",)),
        name="outproj_router",
    )(yc, ya, yp, ys, x, gate1, wo_bf, n2g, shift2, scale2, rwT, rb)


def _max_window_rows():
    return -(-(ROUTE_TILE * TOP_K + N_EXPERTS * (ROW_CHUNK - 1)) // 128) * 128


def _pack_pairs(lo, hi):
    lo_bits = lax.bitcast_convert_type(lo, jnp.uint32)
    hi_bits = lax.bitcast_convert_type(hi, jnp.uint32)
    return lax.bitcast_convert_type(hi_bits | (lo_bits >> 16), i32)


def _unpack_pairs(words):
    bits = lax.bitcast_convert_type(words, jnp.uint32)
    lo = lax.bitcast_convert_type(bits << 16, f32).astype(bf16)
    hi = lax.bitcast_convert_type(bits & jnp.uint32(0xFFFF0000), f32).astype(bf16)
    return lo, hi


def _dispatch_kernel(big_src_s, big_dst_s, wbig_s, small_src_s, small_dst_s, wsmall_s, wchunks_s,
                     tail_start_s, tail_chunks_s, n_used_s,
                     h_ref, dest_ref, xs_ref, sorted_ref, zero_ref, sem):
    w = pl.program_id(0)
    nw = pl.num_programs(0)
    W = ROUTE_TILE
    R = sorted_ref.shape[1]
    half = sorted_ref.shape[2]
    slot = w % DISPATCH_BUFFERS
    for r0 in range(0, R, DISPATCH_ROW_BLOCK):
        rio = r0 + lax.broadcasted_iota(i32, (DISPATCH_ROW_BLOCK, W), 0)
        hit = rio == dest_ref[0:1, :]
        for k in range(1, TOP_K):
            hit = hit | (rio == dest_ref[k:k + 1, :])
        onehot = jnp.where(hit, 1.0, 0.0).astype(bf16)
        rows = jnp.dot(onehot, h_ref[...], preferred_element_type=f32)
        sorted_ref[slot, r0:r0 + DISPATCH_ROW_BLOCK, :] = _pack_pairs(rows[:, :half], rows[:, half:])

    def rows_copy(s, local_row, global_row, rows):
        return pltpu.make_async_copy(
            sorted_ref.at[s, pl.ds(pl.multiple_of(local_row, ROW_CHUNK), rows), :],
            xs_ref.at[pl.ds(pl.multiple_of(global_row, ROW_CHUNK), rows), :], sem.at[s])

    def zero_copy(dst_row):
        return pltpu.make_async_copy(
            zero_ref.at[0:ROW_CHUNK, :],
            xs_ref.at[pl.ds(pl.multiple_of(dst_row, ROW_CHUNK), ROW_CHUNK), :], sem.at[0])

    def zero_tile_copy(tile):
        return pltpu.make_async_copy(
            zero_ref, xs_ref.at[pl.ds(pl.multiple_of(tile * FFN_TILE, FFN_TILE), FFN_TILE), :], sem.at[0])

    def drain(win):
        chunks = wchunks_s[win]

        @pl.when(chunks > 0)
        def _():
            rows = pl.multiple_of(chunks * ROW_CHUNK, ROW_CHUNK)
            pltpu.make_async_copy(xs_ref.at[pl.ds(0, rows), :], xs_ref.at[pl.ds(0, rows), :],
                                  sem.at[win % DISPATCH_BUFFERS]).wait()

    @pl.when(w >= DISPATCH_BUFFERS - 1)
    def _():
        drain(jnp.maximum(w - (DISPATCH_BUFFERS - 1), 0))

    for parity in range(DISPATCH_BUFFERS):
        @pl.when(slot == parity)
        def _():
            def big(q, carry):
                rows_copy(parity, big_dst_s[w * MAX_COMBINE_PIECES + q], big_src_s[w * MAX_COMBINE_PIECES + q],
                          COMBINE_COPY_ROWS).start()
                return carry
            lax.fori_loop(0, wbig_s[w], big, 0)

            def small(q, carry):
                rows_copy(parity, small_dst_s[w * MAX_COMBINE_CHUNKS + q], small_src_s[w * MAX_COMBINE_CHUNKS + q],
                          ROW_CHUNK).start()
                return carry
            lax.fori_loop(0, wsmall_s[w], small, 0)

    @pl.when(w == nw - 1)
    def _():
        for back in range(DISPATCH_BUFFERS - 2, -1, -1):
            @pl.when(w - back >= 0)
            def _(back=back):
                drain(jnp.maximum(w - back, 0))
        zero_ref[...] = jnp.zeros(zero_ref.shape, i32)

        def per_tail(e, total):
            n = tail_chunks_s[e]
            dst = tail_start_s[e]

            def issue(j, c):
                zero_copy(dst + j * ROW_CHUNK).start()
                return c
            lax.fori_loop(0, n, issue, 0)
            return total + n
        tails = lax.fori_loop(0, N_EXPERTS, per_tail, 0)

        def drain_tail(j, c):
            zero_copy(0).wait()
            return c
        lax.fori_loop(0, tails, drain_tail, 0)

        n_tiles = xs_ref.shape[0] // FFN_TILE

        def issue_tile(j, c):
            zero_tile_copy(j).start()
            return c
        lax.fori_loop(n_used_s[0], n_tiles, issue_tile, 0)

        def drain_tile(j, c):
            zero_tile_copy(0).wait()
            return c
        lax.fori_loop(n_used_s[0], n_tiles, drain_tile, 0)


def _dispatch(h2, destT, t, p_rows):
    N, D = h2.shape
    W = ROUTE_TILE
    nw = N // W
    R = _max_window_rows()
    grid_spec = pltpu.PrefetchScalarGridSpec(
        num_scalar_prefetch=10,
        grid=(nw,),
        in_specs=[pl.BlockSpec((W, D), lambda w, *_: (w, 0)),
                  pl.BlockSpec((TOP_K, W), lambda w, *_: (0, w))],
        out_specs=pl.BlockSpec(memory_space=pl.ANY),
        scratch_shapes=[pltpu.VMEM((DISPATCH_BUFFERS, R, D // 2), i32), pltpu.VMEM((FFN_TILE, D // 2), i32),
                        pltpu.SemaphoreType.DMA((DISPATCH_BUFFERS,))],
    )
    return pl.pallas_call(
        _dispatch_kernel,
        grid_spec=grid_spec,
        out_shape=jax.ShapeDtypeStruct((p_rows, D // 2), i32),
        compiler_params=_cparams(("arbitrary",)),
        name="expert_dispatch",
    )(t['big_src'], t['big_dst'], t['wbig'], t['small_src'], t['small_dst'], t['wsmall'], t['wchunks'],
      t['tail_start'], t['tail_chunks'], t['n_used'], h2, destT)


def _ffn_kernel(tile_expert_s, next_expert_s, n_used_s, x_ref, w1_hbm, b1_ref, w2_hbm, b2_ref, y_ref,
                w1f_ref, w2f_ref, w1b_ref, w2b_ref, sem, *, layer):
    i = pl.program_id(0)
    last = n_used_s[0] - 1
    expert = tile_expert_s[jnp.minimum(i, last)]
    prev_expert = tile_expert_s[jnp.minimum(jnp.maximum(i - 1, 0), last)]

    def weight_copies(e):
        return (pltpu.make_async_copy(w1_hbm.at[layer, e], w1f_ref, sem.at[0]),
                pltpu.make_async_copy(w2_hbm.at[layer, e], w2f_ref, sem.at[1]))

    @pl.when(i == 0)
    def _():
        for c in weight_copies(expert):
            c.start()

    @pl.when((i == 0) | (expert != prev_expert))
    def _():
        for c in weight_copies(expert):
            c.wait()
        w1b_ref[...] = w1f_ref[...].astype(bf16)
        w2b_ref[...] = w2f_ref[...].astype(bf16)
        nxt = next_expert_s[expert]

        @pl.when(nxt < N_EXPERTS)
        def _():
            for c in weight_copies(nxt):
                c.start()

    @pl.when(i < n_used_s[0])
    def _():
        dff = w2b_ref.shape[0]
        half = x_ref.shape[1]
        sub = x_ref.shape[0] // FFN_SUBTILES
        hidden = []
        for r in range(FFN_SUBTILES):
            x_lo, x_hi = _unpack_pairs(x_ref[r * sub:(r + 1) * sub, :])
            hidden.append(jnp.dot(x_lo, w1b_ref[0:half, :], preferred_element_type=f32)
                          + jnp.dot(x_hi, w1b_ref[half:2 * half, :], preferred_element_type=f32))
        for r in range(FFN_SUBTILES):
            hh = hidden[r] + b1_ref[0, 0]
            x_glu = jnp.minimum(hh[:, :dff], SWIGLU_LIMIT)
            x_lin = jnp.clip(hh[:, dff:], -SWIGLU_LIMIT, SWIGLU_LIMIT)
            act = x_glu * jax.nn.sigmoid(SWIGLU_ALPHA * x_glu) * (x_lin + 1.0)
            y = jnp.dot(act.astype(bf16), w2b_ref[...], preferred_element_type=f32) + b2_ref[0, 0]
            y = y.astype(bf16).astype(f32)
            y_ref[r * sub:(r + 1) * sub, :] = _pack_pairs(y[:, :half], y[:, half:])

    @pl.when(i >= n_used_s[0])
    def _():
        y_ref[...] = jnp.zeros(y_ref.shape, i32)


def _expert_ffn(xs, w1, b1, w2, b2, t, layer):
    P, half = xs.shape
    L, E, D, F2 = w1.shape
    tm = FFN_TILE
    nt = P // tm

    def tile(i, te, ne, nu):
        return (jnp.minimum(i, nu[0] - 1), 0)

    def expert4(i, te, ne, nu):
        return (layer, te[jnp.minimum(i, nu[0] - 1)], 0, 0)

    grid_spec = pltpu.PrefetchScalarGridSpec(
        num_scalar_prefetch=3,
        grid=(nt,),
        in_specs=[pl.BlockSpec((tm, half), tile),
                  pl.BlockSpec(memory_space=pl.ANY),
                  pl.BlockSpec((1, 1, 1, F2), expert4),
                  pl.BlockSpec(memory_space=pl.ANY),
                  pl.BlockSpec((1, 1, 1, D), expert4)],
        out_specs=pl.BlockSpec((tm, half), lambda i, te, ne, nu: (i, 0)),
        scratch_shapes=[pltpu.VMEM((D, F2), f32), pltpu.VMEM((F2 // 2, D), f32),
                        pltpu.VMEM((D, F2), bf16), pltpu.VMEM((F2 // 2, D), bf16),
                        pltpu.SemaphoreType.DMA((2,))],
    )
    return pl.pallas_call(
        functools.partial(_ffn_kernel, layer=layer),
        grid_spec=grid_spec,
        out_shape=jax.ShapeDtypeStruct((P, half), i32),
        compiler_params=_cparams(("arbitrary",)),
        name="expert_ffn",
    )(t['tile_expert'], t['next_expert'], t['n_used'], xs, w1, b1.reshape(L, E, 1, F2), w2,
      b2.reshape(L, E, 1, D))


def _combine_kernel(big_src_s, big_dst_s, wbig_s, small_src_s, small_dst_s, wsmall_s, wchunks_s,
                    ys_ref, dest_ref, gate_ref, x1_ref, g2_ref, fg_ref, o_ref, local_ref, sem, *, final):
    step = pl.program_id(0)
    n_steps = pl.num_programs(0)
    W = ROUTE_TILE
    wps = COMBINE_WINDOWS_PER_STEP
    R = local_ref.shape[2]
    slot = step % 2

    def rows_copy(s, u, src_row, dst_row, rows):
        return pltpu.make_async_copy(
            ys_ref.at[pl.ds(pl.multiple_of(src_row, ROW_CHUNK), rows), :],
            local_ref.at[s, u, pl.ds(pl.multiple_of(dst_row, ROW_CHUNK), rows), :], sem.at[s, u])

    def fetch(at_step, s):
        for u in range(wps):
            win = at_step * wps + u

            def big(q, carry, win=win, u=u):
                rows_copy(s, u, big_src_s[win * MAX_COMBINE_PIECES + q], big_dst_s[win * MAX_COMBINE_PIECES + q],
                          COMBINE_COPY_ROWS).start()
                return carry
            lax.fori_loop(0, wbig_s[win], big, 0)

            def small(q, carry, win=win, u=u):
                rows_copy(s, u, small_src_s[win * MAX_COMBINE_CHUNKS + q],
                          small_dst_s[win * MAX_COMBINE_CHUNKS + q], ROW_CHUNK).start()
                return carry
            lax.fori_loop(0, wsmall_s[win], small, 0)

    @pl.when(step == 0)
    def _():
        local_ref[...] = jnp.zeros(local_ref.shape, i32)
        fetch(0, 0)

    for parity in range(2):
        @pl.when((step + 1 < n_steps) & (slot == parity))
        def _():
            fetch(jnp.minimum(step + 1, n_steps - 1), 1 - parity)

    for u in range(wps):
        chunks = wchunks_s[step * wps + u]

        @pl.when(chunks > 0)
        def _(chunks=chunks, u=u):
            rows = pl.multiple_of(chunks * ROW_CHUNK, ROW_CHUNK)
            pltpu.make_async_copy(ys_ref.at[pl.ds(0, rows), :], local_ref.at[slot, u, pl.ds(0, rows), :],
                                  sem.at[slot, u]).wait()

    rio = lax.broadcasted_iota(i32, (R, W), 0)
    tn = (((0,), (0,)), ((), ()))
    weights, halves = [], []
    for u in range(wps):
        cols = slice(u * W, (u + 1) * W)
        wt = jnp.zeros((R, W), f32)
        for k in range(TOP_K):
            wt = jnp.where(rio == dest_ref[k:k + 1, cols], gate_ref[k:k + 1, cols], wt)
        weights.append(wt.astype(bf16))
        halves.append(_unpack_pairs(local_ref[slot, u]))
    for u in range(wps):
        y_lo, y_hi = halves[u]
        moe = jnp.concatenate([lax.dot_general(weights[u], y_lo, tn, preferred_element_type=f32),
                               lax.dot_general(weights[u], y_hi, tn, preferred_element_type=f32)], axis=1)
        rows = slice(u * W, (u + 1) * W)
        x2 = x1_ref[rows, :] + g2_ref[0] * moe
        if final:
            ms = jnp.mean(x2 * x2, axis=-1, keepdims=True)
            x2 = x2 * lax.rsqrt(ms + EPS) * fg_ref[...]
        o_ref[rows, :] = x2


def _combine(ys, dest, gates, x1, gate2, final_g, t, seq, final):
    N, D = x1.shape
    wps = COMBINE_WINDOWS_PER_STEP
    W = ROUTE_TILE * wps
    per_b = seq // W
    R = _max_window_rows()
    grid_spec = pltpu.PrefetchScalarGridSpec(
        num_scalar_prefetch=7,
        grid=(N // W,),
        in_specs=[pl.BlockSpec(memory_space=pl.ANY),
                  pl.BlockSpec((TOP_K, W), lambda w, *_: (0, w)),
                  pl.BlockSpec((TOP_K, W), lambda w, *_: (0, w)),
                  pl.BlockSpec((W, D), lambda w, *_: (w, 0)),
                  pl.BlockSpec((1, 1, D), lambda w, *_: (w // per_b, 0, 0)),
                  pl.BlockSpec((1, D), lambda w, *_: (0, 0))],
        out_specs=pl.BlockSpec((W, D), lambda w, *_: (w, 0)),
        scratch_shapes=[pltpu.VMEM((2, wps, R, D // 2), i32), pltpu.SemaphoreType.DMA((2, wps))],
    )
    return pl.pallas_call(
        functools.partial(_combine_kernel, final=final),
        grid_spec=grid_spec,
        out_shape=jax.ShapeDtypeStruct((N, D), f32),
        compiler_params=_cparams(("arbitrary",)),
        name="expert_combine",
    )(t['big_src'], t['big_dst'], t['wbig'], t['small_src'], t['small_dst'], t['wsmall'], t['wchunks'],
      ys, dest, gates, x1, gate2, final_g)


def _routing_tables(padded, n_tiles):
    nw, E = padded.shape
    lstart = jnp.cumsum(padded, axis=1) - padded
    tot = jnp.sum(padded, axis=0)
    region = (tot + FFN_TILE - 1) // FFN_TILE * FFN_TILE
    region_end = jnp.cumsum(region)
    region_start = region_end - region
    gbase = region_start[None, :] + jnp.cumsum(padded, axis=0) - padded
    nchunk = padded // ROW_CHUNK
    tail_start = region_start + tot
    tail_chunks = (region - tot) // ROW_CHUNK
    n_used = (region_end[-1] // FFN_TILE).astype(i32).reshape(1)
    tile_row = jnp.arange(n_tiles, dtype=i32) * FFN_TILE
    tile_expert = jnp.minimum(
        jnp.sum((region_end[None, :] <= tile_row[:, None]).astype(i32), axis=1), E - 1).astype(i32)
    eidx = jnp.arange(E, dtype=i32)
    later_nonempty = (eidx[None, :] > eidx[:, None]) & (region[None, :] > 0)
    next_expert = jnp.min(jnp.where(later_nonempty, eidx[None, :], E), axis=1)
    flat = lambda a: a.reshape(-1).astype(i32)

    def copy_list(counts, window_rows, expert_rows, step, max_items):
        ends = jnp.cumsum(counts, axis=1)
        slot = jnp.arange(max_items, dtype=i32)
        owner = jnp.sum((ends[:, None, :] <= slot[None, :, None]).astype(i32), axis=2)
        hit = owner[:, :, None] == eidx[None, None, :]
        pick = lambda a: jnp.sum(jnp.where(hit, a[:, None, :], 0), axis=2)
        offset = step * (slot[None, :] - pick(ends - counts))
        return flat(pick(window_rows) + offset), flat(pick(expert_rows) + offset), flat(ends[:, -1])

    nbig = padded // COMBINE_COPY_ROWS
    big_dst, big_src, wbig = copy_list(nbig, lstart, gbase, COMBINE_COPY_ROWS, MAX_COMBINE_PIECES)
    rest = nbig * COMBINE_COPY_ROWS
    small_dst, small_src, wsmall = copy_list(nchunk - nbig * (COMBINE_COPY_ROWS // ROW_CHUNK),
                                             lstart + rest, gbase + rest, ROW_CHUNK, MAX_COMBINE_CHUNKS)
    return dict(next_expert=flat(next_expert), wchunks=flat(jnp.sum(nchunk, axis=1)),
                big_src=big_src, big_dst=big_dst, wbig=wbig,
                small_src=small_src, small_dst=small_dst, wsmall=wsmall,
                tail_start=flat(tail_start), tail_chunks=flat(tail_chunks),
                n_used=n_used, tile_expert=tile_expert)


def _block_diag(w):
    g, a, b = w.shape
    out = jnp.zeros((g * a, g * b), w.dtype)
    for i in range(g):
        out = out.at[i * a:(i + 1) * a, i * b:(i + 1) * b].set(w[i])
    return out


def kernel(x, c, w_mod, b_mod, norm1_g, w_in, conv_w, conv_b, conv_norm_g, conv_norm_b, rel_bias, pool_w, pool_scale, sgu_norm_g, sgu_norm_b, sgu_w, sgu_b, mix_out_g, w_out, norm2_g, router_w, router_b, exp_w1, exp_b1, exp_w2, exp_b2, final_norm_g):
    B, S, D = x.shape
    L = w_mod.shape[0]
    N = B * S
    nw = N // ROUTE_TILE
    assert S % SEQ_TILE == 0 and S % MOBA_BLOCK == 0 and N % ROUTE_TILE == 0 and S % ROUTE_TILE == 0
    p_bound = N * TOP_K + nw * N_EXPERTS * (ROW_CHUNK - 1) + N_EXPERTS * (FFN_TILE - 1)
    n_tiles = -(-p_bound // FFN_TILE)
    p_rows = n_tiles * FFN_TILE

    mod = _modulation(c, w_mod, b_mod)
    bias_tiles = _bias_tiles(rel_bias)
    far_bucket = int(_t5_bucket_table(MOBA_BLOCK + 2)[MOBA_BLOCK + 1])
    assert far_bucket == int(_t5_bucket_table(S + 1)[S])
    far_bias = rel_bias[far_bucket].astype(f32)
    row = lambda a: a.reshape(1, -1)
    for l in range(L):
        m6 = mod[l].reshape(B, 6, 1, D)
        shift1, scale1, gate1, shift2, scale2, gate2 = (m6[:, j] for j in range(6))
        mg = row(mix_out_g[l])
        qT, k, vT, kmean, yc, yp, ys = _project_and_mix(
            x, shift1, scale1, row(norm1_g[l]), w_in[l].astype(bf16),
            conv_w[l], row(conv_b[l]), row(conv_norm_g[l]), row(conv_norm_b[l]),
            _block_diag(pool_w[l]).astype(bf16), row(pool_scale[l]),
            row(sgu_norm_g[l]), row(sgu_norm_b[l]), sgu_w[l],
            jnp.repeat(sgu_b[l].T, HEAD_DIM, axis=1), mg)
        ya = _moba_attention(qT, k, vT, kmean.reshape(B, -1, GROUP_WIDTH), bias_tiles, far_bias, mg)
        flat = lambda a: a.reshape(N, -1)
        x1, h2, destT, gateT, padded = _outproj_router(
            flat(yc), flat(ya), flat(yp), flat(ys), x.reshape(N, D), gate1, w_out[l].astype(bf16),
            row(norm2_g[l]), shift2, scale2, router_w[l].T, router_b[l].reshape(-1, 1), S)
        t = _routing_tables(padded.reshape(nw, N_EXPERTS), n_tiles)
        xs = _dispatch(h2, destT, t, p_rows)
        ysort = _expert_ffn(xs, exp_w1, exp_b1, exp_w2, exp_b2, t, l)
        x = _combine(ysort, destT, gateT, x1, gate2, row(final_norm_g), t, S,
                     final=(l == L - 1)).reshape(B, S, D)
    return x
```

```python
import functools
import math

import numpy as np
import jax
import jax.numpy as jnp
from jax import lax
from jax.experimental import pallas as pl
from jax.experimental.pallas import tpu as pltpu

f32, bf16, i32 = jnp.float32, jnp.bfloat16, jnp.int32

GROUP_WIDTH = 256
HEADS = 4
HEAD_DIM = 64
V_ROWS = HEAD_DIM + 16
LOG2E = math.log2(math.e)
CONV_WIDTH = 31
MOBA_BLOCK = 256
MOBA_TOPK = 3
Q_CHUNK = 128
N_BUCKETS = 32
T5_MAX_DISTANCE = 128
POOL_WINDOWS = (2, 4, 8, 16)
SGU_CHUNK = 128
N_EXPERTS = 32
TOP_K = 4
SWIGLU_LIMIT = 7.0
SWIGLU_ALPHA = 1.702
EPS = 1e-6

SUBLANES = 8
LANES = 128
ATTN_QUERY_GROUP = 256
HALO = 32
SEQ_TILE = 1024
ROUTE_TILE = 256
ROUTE_WINDOWS_PER_STEP = 4
ROW_CHUNK = 8
DISPATCH_ROW_BLOCK = 256
DISPATCH_BUFFERS = 3
COMBINE_COPY_ROWS = 32
COMBINE_WINDOWS_PER_STEP = 4
MAX_COMBINE_PIECES = (ROUTE_TILE * TOP_K + N_EXPERTS * (ROW_CHUNK - 1)) // COMBINE_COPY_ROWS
MAX_COMBINE_CHUNKS = N_EXPERTS * (COMBINE_COPY_ROWS // ROW_CHUNK - 1)
FFN_TILE = 512
FFN_SUBTILES = 2
NEG = -1e30
VMEM_LIMIT = 56 * 1024 * 1024


def _cparams(sem):
    return pltpu.CompilerParams(dimension_semantics=sem, vmem_limit_bytes=VMEM_LIMIT)


def _split_bf16(a):
    hi = a.astype(bf16)
    lo = (a - hi.astype(f32)).astype(bf16)
    return hi, lo


def _mod_kernel(c_ref, w_ref, b_ref, o_ref):
    c = c_ref[...]
    cond = c * jax.nn.sigmoid(c)
    o_ref[0] = jnp.dot(cond, w_ref[0], preferred_element_type=f32,
                       precision=lax.Precision.HIGHEST) + b_ref[0]


def _modulation(c, w_mod, b_mod):
    L, D, M = w_mod.shape
    B = c.shape[0]
    tn = 1536
    return pl.pallas_call(
        _mod_kernel,
        grid=(L, M // tn),
        in_specs=[pl.BlockSpec((B, D), lambda l, j: (0, 0)),
                  pl.BlockSpec((1, D, tn), lambda l, j: (l, 0, j)),
                  pl.BlockSpec((1, 1, tn), lambda l, j: (l, 0, j))],
        out_specs=pl.BlockSpec((1, B, tn), lambda l, j: (l, 0, j)),
        out_shape=jax.ShapeDtypeStruct((L, B, M), f32),
        compiler_params=_cparams(("arbitrary", "arbitrary")),
        name="modulation",
    )(c, w_mod, b_mod.reshape(L, 1, M))


def _project_mix_kernel(x_ref, sh_ref, sc_ref, g_ref, w_ref,
                        cw_ref, cb_ref, cng_ref, cnb_ref, pw_ref, psc_ref, sg_ref, sb_ref, sw_ref, sbias_ref, mg_ref,
                        qT_ref, k_ref, vT_ref, km_ref, yc_ref, yp_ref, ys_ref,
                        gext, gshift, zext, s2, s4, s8):
    first = pl.program_id(1) == 0

    @pl.when(first)
    def _():
        gext[0:HALO, :] = jnp.zeros((HALO, GROUP_WIDTH), f32)
        zext[0:HALO, :] = jnp.zeros((HALO, GROUP_WIDTH), f32)

    @pl.when(jnp.logical_not(first))
    def _():
        gext[0:HALO, :] = gext[SEQ_TILE:SEQ_TILE + HALO, :]
        zext[0:HALO, :] = zext[SEQ_TILE:SEQ_TILE + HALO, :]

    x = x_ref[0]
    ms = jnp.mean(x * x, axis=-1, keepdims=True)
    h = x * lax.rsqrt(ms + EPS) * g_ref[...]
    h = h * (1.0 + sc_ref[0]) + sh_ref[0]
    gw = GROUP_WIDTH
    hb = h.astype(bf16)
    pc = jnp.dot(hb, w_ref[:, 0:2 * gw], preferred_element_type=f32)
    qkv = jnp.dot(hb, w_ref[:, 2 * gw:5 * gw], preferred_element_type=f32)
    rest = jnp.dot(hb, w_ref[:, 5 * gw:8 * gw], preferred_element_type=f32)
    _local_mixers(pc, rest[:, 0:gw], rest[:, gw:3 * gw],
                  cw_ref, cb_ref, cng_ref, cnb_ref, pw_ref, psc_ref, sg_ref, sb_ref, sw_ref, sbias_ref, mg_ref,
                  yc_ref, yp_ref, ys_ref, gext, gshift, zext, s2, s4, s8)
    proj = jnp.concatenate([pc, qkv], axis=1)
    q = proj[:, 2 * gw:3 * gw] * (HEAD_DIM ** -0.5 * LOG2E)
    qT_ref[0] = q.T.astype(bf16)
    kk = proj[:, 3 * gw:4 * gw]
    for h in range(HEADS):
        k_ref[0, h] = kk[:, h * HEAD_DIM:(h + 1) * HEAD_DIM].astype(bf16)
    for j in range(SEQ_TILE // MOBA_BLOCK):
        km_ref[0, 0, j:j + 1, :] = jnp.mean(kk[j * MOBA_BLOCK:(j + 1) * MOBA_BLOCK], axis=0, keepdims=True)
    for j in range(SEQ_TILE // MOBA_BLOCK):
        vT = proj[j * MOBA_BLOCK:(j + 1) * MOBA_BLOCK, 4 * gw:5 * gw].T.astype(bf16)
        for h in range(HEADS):
            vT_ref[0, j, h] = jnp.concatenate(
                [vT[h * HEAD_DIM:(h + 1) * HEAD_DIM, :], jnp.ones((V_ROWS - HEAD_DIM, MOBA_BLOCK), bf16)], axis=0)


def _group_rms(y, g):
    return y * lax.rsqrt(jnp.mean(y * y, axis=-1, keepdims=True) + EPS) * g


def _local_mixers(pc, z, zz,
                  cw_ref, cb_ref, cng_ref, cnb_ref, pw_ref, psc_ref,
                  sg_ref, sb_ref, sw_ref, sbias_ref, mg_ref,
                  yc_ref, yp_ref, ys_ref,
                  gext, gshift, zext, s2, s4, s8):
    i = pl.program_id(1)
    ts = SEQ_TILE
    gw = GROUP_WIDTH
    lane = lax.broadcasted_iota(i32, (1, gw), 1)

    g = pc[:, :gw] * jax.nn.sigmoid(pc[:, gw:])
    gext[HALO:HALO + ts, :] = g
    span = ts + HALO - SUBLANES
    for a in range(1, SUBLANES):
        gshift[a - 1, 0:span, :] = gext[pl.ds(a, span), :]
    acc = jnp.zeros((ts, gw), f32)
    for j in range(CONV_WIDTH):
        b, a = divmod(HALO - (CONV_WIDTH - 1) + j, SUBLANES)
        tap = gext[pl.ds(SUBLANES * b, ts), :] if a == 0 else gshift[a - 1, pl.ds(SUBLANES * b, ts), :]
        acc = acc + cw_ref[j:j + 1, :] * tap
    y = acc + cb_ref[...]
    r = lax.broadcasted_iota(i32, (gw, gw), 0) // HEAD_DIM
    c = lax.broadcasted_iota(i32, (gw, gw), 1) // HEAD_DIM
    avg = jnp.where(r == c, 1.0 / HEAD_DIM, 0.0).astype(bf16)

    def head_mean(t):
        hi, lo = _split_bf16(t)
        return (jnp.dot(hi, avg, preferred_element_type=f32)
                + jnp.dot(lo, avg, preferred_element_type=f32))

    mu = head_mean(y)
    yc = y - mu
    var = head_mean(yc * yc)
    yn = yc * lax.rsqrt(var + EPS) * cng_ref[...] + cnb_ref[...]
    yconv = yn * jax.nn.sigmoid(yn)
    yc_ref[0] = _group_rms(yconv, mg_ref[:, 0:gw]).astype(bf16)

    zext[HALO:HALO + ts, :] = z
    n2, n4, n8 = ts + 14, ts + 12, ts + 8
    s2[0:n2, :] = zext[pl.ds(HALO - 14, n2), :] + zext[pl.ds(HALO - 15, n2), :]
    s4[0:n4, :] = s2[pl.ds(2, n4), :] + s2[pl.ds(0, n4), :]
    s8[0:n8, :] = s4[pl.ds(4, n8), :] + s4[pl.ds(0, n8), :]
    w2 = s2[pl.ds(14, ts), :]
    w4 = s4[pl.ds(12, ts), :]
    w8 = s8[pl.ds(8, ts), :]
    w16 = w8 + s8[pl.ds(0, ts), :]
    tpos = (i * ts + lax.broadcasted_iota(i32, (ts, 1), 0) + 1).astype(f32)
    grp = lane // (gw // len(POOL_WINDOWS))
    pooled = jnp.zeros((ts, gw), f32)
    for gi, (w, sw) in enumerate(zip(POOL_WINDOWS, (w2, w4, w8, w16))):
        pooled = jnp.where(grp == gi, sw / jnp.minimum(tpos, float(w)), pooled)
    pooled = pooled - z
    yp = jnp.dot(pooled.astype(bf16), pw_ref[...], preferred_element_type=f32) * psc_ref[...]
    yp_ref[0] = _group_rms(yp, mg_ref[:, 2 * gw:3 * gw]).astype(bf16)

    zz = 0.5 * zz * (1.0 + lax.erf(zz * (1.0 / math.sqrt(2.0))))
    u = zz[:, :gw]
    v = zz[:, gw:]
    vm = jnp.mean(v, axis=-1, keepdims=True)
    vc = v - vm
    vv = jnp.mean(vc * vc, axis=-1, keepdims=True)
    vn = (vc * lax.rsqrt(vv + EPS) * sg_ref[...] + sb_ref[...]).astype(bf16)
    li = lax.broadcasted_iota(i32, (SGU_CHUNK, SGU_CHUNK), 0)
    lj = lax.broadcasted_iota(i32, (SGU_CHUNK, SGU_CHUNK), 1)
    head_of_lane = lane // HEAD_DIM
    wts = [jnp.where(li >= lj, sw_ref[h], 0.0).astype(bf16) for h in range(HEADS)]
    outs = []
    for n in range(ts // SGU_CHUNK):
        vch = vn[n * SGU_CHUNK:(n + 1) * SGU_CHUNK]
        mixed = sbias_ref[...]
        for h in range(HEADS):
            mh = jnp.dot(wts[h], vch, preferred_element_type=f32)
            mixed = mixed + jnp.where(head_of_lane == h, mh, 0.0)
        outs.append(u[n * SGU_CHUNK:(n + 1) * SGU_CHUNK] * mixed)
    ysgu = jnp.concatenate(outs, axis=0)
    ys_ref[0] = _group_rms(ysgu, mg_ref[:, 3 * gw:4 * gw]).astype(bf16)


def _project_and_mix(x, shift, scale, g, w_bf, cw, cb, cng, cnb, pw_bd, psc, sg, sb, sw, sbias, mg):
    B, S, D = x.shape
    gw = GROUP_WIDTH
    ts = SEQ_TILE
    nt = S // ts
    row = lambda b, i: (b, i, 0)
    col = lambda b, i: (b, 0, i)
    vec = lambda b, i: (b, 0, 0)
    full2 = lambda b, i: (0, 0)
    full3 = lambda b, i: (0, 0, 0)
    mixed = jax.ShapeDtypeStruct((B, S, gw), bf16)
    return pl.pallas_call(
        _project_mix_kernel,
        grid=(B, nt),
        in_specs=[pl.BlockSpec((1, ts, D), row),
                  pl.BlockSpec((1, 1, D), vec),
                  pl.BlockSpec((1, 1, D), vec),
                  pl.BlockSpec((1, D), full2),
                  pl.BlockSpec(w_bf.shape, full2),
                  pl.BlockSpec(cw.shape, full2), pl.BlockSpec(cb.shape, full2),
                  pl.BlockSpec(cng.shape, full2), pl.BlockSpec(cnb.shape, full2),
                  pl.BlockSpec(pw_bd.shape, full2), pl.BlockSpec(psc.shape, full2),
                  pl.BlockSpec(sg.shape, full2), pl.BlockSpec(sb.shape, full2),
                  pl.BlockSpec(sw.shape, full3), pl.BlockSpec(sbias.shape, full2),
                  pl.BlockSpec(mg.shape, full2)],
        out_specs=[pl.BlockSpec((1, gw, ts), col),
                   pl.BlockSpec((1, HEADS, ts, HEAD_DIM), lambda b, i: (b, 0, i, 0)),
                   pl.BlockSpec((1, ts // MOBA_BLOCK, HEADS, V_ROWS, MOBA_BLOCK), lambda b, i: (b, i, 0, 0, 0)),
                   pl.BlockSpec((1, 1, ts // MOBA_BLOCK, gw), lambda b, i: (b, i, 0, 0)),
                   pl.BlockSpec((1, ts, gw), row), pl.BlockSpec((1, ts, gw), row), pl.BlockSpec((1, ts, gw), row)],
        out_shape=[jax.ShapeDtypeStruct((B, gw, S), bf16),
                   jax.ShapeDtypeStruct((B, HEADS, S, HEAD_DIM), bf16),
                   jax.ShapeDtypeStruct((B, S // MOBA_BLOCK, HEADS, V_ROWS, MOBA_BLOCK), bf16),
                   jax.ShapeDtypeStruct((B, nt, ts // MOBA_BLOCK, gw), f32),
                   mixed, mixed, mixed],
        scratch_shapes=[pltpu.VMEM((ts + HALO, gw), f32), pltpu.VMEM((SUBLANES - 1, ts + HALO, gw), f32),
                        pltpu.VMEM((ts + HALO, gw), f32),
                        pltpu.VMEM((ts + 16, gw), f32), pltpu.VMEM((ts + 16, gw), f32),
                        pltpu.VMEM((ts + 16, gw), f32)],
        compiler_params=_cparams(("arbitrary", "arbitrary")),
        name="project_and_mix",
    )(x, shift, scale, g, w_bf, cw, cb, cng, cnb, pw_bd, psc, sg, sb, sw, sbias, mg)


def _t5_bucket_table(max_dist):
    d = np.arange(max_dist, dtype=np.int64)
    max_exact = N_BUCKETS // 2
    nf = np.maximum(d, 1).astype(np.float32)
    large = max_exact + (np.log(nf / np.float32(max_exact)) / np.float32(math.log(T5_MAX_DISTANCE / max_exact))
                         * np.float32(N_BUCKETS - max_exact)).astype(np.int32)
    large = np.minimum(large, N_BUCKETS - 1)
    return np.where(d < max_exact, d, large).astype(np.int32)


_TILE_BASES = (0, MOBA_BLOCK)


def _bias_kernel(tab_ref, o_ref):
    blk, qc = MOBA_BLOCK, MOBA_BLOCK
    table = _t5_bucket_table(2 * blk + qc)
    first = [int(np.argmax(table >= b)) for b in range(N_BUCKETS)]
    j = lax.broadcasted_iota(i32, (blk, qc), 0)
    q = lax.broadcasted_iota(i32, (blk, qc), 1)
    for t, base in enumerate(_TILE_BASES):
        d = base + q - j
        lo, hi = max(base - (blk - 1), 0), base + qc - 1
        for h in range(HEADS):
            val = jnp.full((blk, qc), tab_ref[h], f32)
            for b in range(1, N_BUCKETS):
                if first[b] > hi:
                    continue
                if first[b] <= lo:
                    val = jnp.full((blk, qc), tab_ref[b * HEADS + h], f32)
                else:
                    val = jnp.where(d >= first[b], tab_ref[b * HEADS + h], val)
            o_ref[t, h] = jnp.where(d >= 0, val * LOG2E, NEG)


def _bias_tiles(rel_bias):
    return pl.pallas_call(
        _bias_kernel,
        in_specs=[pl.BlockSpec(memory_space=pltpu.SMEM)],
        out_shape=jax.ShapeDtypeStruct((len(_TILE_BASES), HEADS, MOBA_BLOCK, MOBA_BLOCK), f32),
        name="bias_tiles",
    )(rel_bias.astype(f32).reshape(-1))


def _attn_kernel(far_ref, qT_ref, k_ref, vT_ref, km_ref, bias_ref, mg_ref, o_ref, mask_ref, sa_ref, sb_ref):
    own = pl.program_id(1)
    blk, gw, hd = MOBA_BLOCK, GROUP_WIDTH, HEAD_DIM
    nb = km_ref.shape[1]

    nio = lax.broadcasted_iota(i32, (nb, blk), 0)
    past = nio < own
    km = km_ref[0]
    q_heads = []
    for h in range(HEADS):
        qh = qT_ref[0, h * hd:(h + 1) * hd, :]
        q_heads.append(qh)
        km_hi, km_lo = _split_bf16(km[:, h * hd:(h + 1) * hd])
        gate = (jnp.dot(km_hi, qh, preferred_element_type=f32)
                + jnp.dot(km_lo, qh, preferred_element_type=f32))
        gate = jnp.where(past, gate, -jnp.inf)
        picked = jnp.zeros((nb, blk), jnp.bool_)
        for _ in range(MOBA_TOPK):
            top = jnp.max(gate, axis=0, keepdims=True)
            first = jnp.min(jnp.where(gate == top, nio, nb), axis=0, keepdims=True)
            hit = nio == first
            picked = picked | hit
            gate = jnp.where(hit, -jnp.inf, gate)
        sel = picked & past
        mask_ref[0, h] = jnp.where(sel, 0.0, NEG)
        mask_ref[1, h] = jnp.where(sel & (nio < own - 1), far_ref[h] * LOG2E, NEG)

    def qk(n, h):
        kb = k_ref[0, h, pl.ds(pl.multiple_of(n * blk, blk), blk), :]
        return jnp.dot(kb, q_heads[h], preferred_element_type=f32)

    def far_scores(n, h):
        return qk(n, h) + mask_ref[1, h, pl.ds(n, 1), :]

    lane_groups = blk // ATTN_QUERY_GROUP

    def update(state, scores, n):
        out = []
        for h in range(HEADS):
            vb = vT_ref[0, n, h]
            for c in range(lane_groups):
                u = h * lane_groups + c
                m, acc = state[2 * u:2 * u + 2]
                s = scores[h][:, c * ATTN_QUERY_GROUP:(c + 1) * ATTN_QUERY_GROUP]
                m_new = jnp.maximum(m, jnp.max(s, axis=0, keepdims=True))
                alpha = jnp.exp2(m - m_new)
                p = jnp.exp2(s - m_new)
                acc = acc * alpha + jnp.dot(vb, p.astype(bf16), preferred_element_type=f32)
                out += [m_new, acc]
        return tuple(out)

    adj = jnp.maximum(own - 1, 0)
    n_far = jnp.maximum(own - 1, 0)
    s_own = [qk(own, h) + bias_ref[0, h] for h in range(HEADS)]
    s_adj = [qk(adj, h) + bias_ref[1, h] + mask_ref[0, h, pl.ds(adj, 1), :] for h in range(HEADS)]
    for h in range(HEADS):
        sa_ref[h] = far_scores(0, h)

    state = []
    for h in range(HEADS):
        vb = jnp.concatenate([vT_ref[0, own, h], vT_ref[0, adj, h]], axis=1)
        for c in range(lane_groups):
            so = s_own[h][:, c * ATTN_QUERY_GROUP:(c + 1) * ATTN_QUERY_GROUP]
            sj = s_adj[h][:, c * ATTN_QUERY_GROUP:(c + 1) * ATTN_QUERY_GROUP]
            m0 = jnp.maximum(jnp.max(so, axis=0, keepdims=True), jnp.max(sj, axis=0, keepdims=True))
            p = jnp.concatenate([jnp.exp2(so - m0).astype(bf16), jnp.exp2(sj - m0).astype(bf16)], axis=0)
            state += [m0, jnp.dot(vb, p, preferred_element_type=f32)]
    state = tuple(state)

    def body(i, state):
        first = jnp.minimum(2 * i, nb - 1)
        second = jnp.minimum(2 * i + 1, nb - 1)
        third = jnp.minimum(2 * i + 2, nb - 1)
        for h in range(HEADS):
            sb_ref[h] = far_scores(second, h)
        state = update(state, [sa_ref[h] for h in range(HEADS)], first)
        for h in range(HEADS):
            sa_ref[h] = far_scores(third, h)
        return update(state, [sb_ref[h] for h in range(HEADS)], second)

    fin = lax.fori_loop(0, (n_far + 1) // 2, body, state)

    def normalised(acc):
        return acc[0:hd, :] / acc[hd:hd + 1, :]

    outT = jnp.concatenate(
        [jnp.concatenate([normalised(fin[2 * (h * lane_groups + c) + 1]) for c in range(lane_groups)], axis=1)
         for h in range(HEADS)], axis=0)
    o_ref[0] = _group_rms(outT.T, mg_ref[:, gw:2 * gw]).astype(bf16)


def _moba_attention(qT, k, vT, kmean, bias_tiles, far_bias, mg):
    B, _, S, _ = k.shape
    gw, blk = GROUP_WIDTH, MOBA_BLOCK
    nb = kmean.shape[1]
    return pl.pallas_call(
        _attn_kernel,
        grid=(B, nb),
        in_specs=[pl.BlockSpec(memory_space=pltpu.SMEM),
                  pl.BlockSpec((1, gw, blk), lambda b, c: (b, 0, c)),
                  pl.BlockSpec((1, HEADS, S, HEAD_DIM), lambda b, c: (b, 0, 0, 0)),
                  pl.BlockSpec((1, nb, HEADS, V_ROWS, blk), lambda b, c: (b, 0, 0, 0, 0)),
                  pl.BlockSpec((1, nb, gw), lambda b, c: (b, 0, 0)),
                  pl.BlockSpec(bias_tiles.shape, lambda b, c: (0, 0, 0, 0)),
                  pl.BlockSpec(mg.shape, lambda b, c: (0, 0))],
        out_specs=pl.BlockSpec((1, blk, gw), lambda b, c: (b, c, 0)),
        out_shape=jax.ShapeDtypeStruct((B, S, gw), bf16),
        scratch_shapes=[pltpu.VMEM((2, HEADS, nb, blk), f32),
                        pltpu.VMEM((HEADS, blk, blk), f32), pltpu.VMEM((HEADS, blk, blk), f32)],
        compiler_params=_cparams(("arbitrary", "arbitrary")),
        name="moba_attention",
    )(far_bias, qT, k, vT, kmean, bias_tiles, mg)


def _outproj_router_kernel(yc_ref, ya_ref, yp_ref, ys_ref, x_ref, g1_ref, wo_ref, n2_ref, sh_ref, sc_ref,
                           rw_ref, rb_ref,
                           x1_ref, h2_ref, dest_ref, gate_ref, pad_ref):
    W = ROUTE_TILE
    windows = range(ROUTE_WINDOWS_PER_STEP)
    rows = [slice(u * W, (u + 1) * W) for u in windows]
    mixed = [jnp.concatenate([r[rows[u], :] for r in (yc_ref, ya_ref, yp_ref, ys_ref)], axis=1) for u in windows]
    projected = [jnp.dot(m, wo_ref[...], preferred_element_type=f32) for m in mixed]

    his, los = [], []
    for u in windows:
        x1 = x_ref[rows[u], :] + g1_ref[0] * projected[u]
        x1_ref[rows[u], :] = x1
        ms = jnp.mean(x1 * x1, axis=-1, keepdims=True)
        h = x1 * lax.rsqrt(ms + EPS) * n2_ref[...]
        h = h * (1.0 + sc_ref[0]) + sh_ref[0]
        h_hi, h_lo = _split_bf16(h)
        h2_ref[rows[u], :] = h_hi
        his.append(h_hi)
        los.append(h_lo)

    nt = (((1,), (1,)), ((), ()))
    rw_hi, rw_lo = _split_bf16(rw_ref[...])
    logits = [(lax.dot_general(rw_hi, his[u], nt, preferred_element_type=f32)
               + lax.dot_general(rw_hi, los[u], nt, preferred_element_type=f32)
               + lax.dot_general(rw_lo, his[u], nt, preferred_element_type=f32)) + rb_ref[...] for u in windows]

    eio = lax.broadcasted_iota(i32, (N_EXPERTS, W), 0)
    sels, multis = [], []
    for u in windows:
        work = logits[u]
        vals, sel_u = [], []
        for k in range(TOP_K):
            m = jnp.max(work, axis=0, keepdims=True)
            idx = jnp.min(jnp.where(work == m, eio, N_EXPERTS), axis=0, keepdims=True)
            sel = eio == idx
            vals.append(m)
            sel_u.append(sel)
            work = jnp.where(sel, -jnp.inf, work)
        exps = [jnp.exp(v - vals[0]) for v in vals]
        denom = exps[0] + exps[1] + exps[2] + exps[3]
        for k in range(TOP_K):
            gate_ref[k:k + 1, rows[u]] = exps[k] / denom
        multi = jnp.zeros((N_EXPERTS, W), f32)
        for sel in sel_u:
            multi = multi + sel.astype(f32)
        sels.append(sel_u)
        multis.append(multi)

    before = (lax.broadcasted_iota(i32, (W, W), 0) < lax.broadcasted_iota(i32, (W, W), 1)).astype(bf16)
    earlier = [jnp.dot(multis[u].astype(bf16), before, preferred_element_type=f32) for u in windows]

    lower = (lax.broadcasted_iota(i32, (N_EXPERTS, N_EXPERTS), 1)
             < lax.broadcasted_iota(i32, (N_EXPERTS, N_EXPERTS), 0)).astype(bf16)
    seg_start = []
    for u in windows:
        cnt = jnp.sum(multis[u], axis=1, keepdims=True).astype(i32)
        padded = (cnt + (ROW_CHUNK - 1)) // ROW_CHUNK * ROW_CHUNK
        pad_ref[u] = padded
        seg_start.append(jnp.dot(lower, jnp.broadcast_to(padded.astype(f32), (N_EXPERTS, W)).astype(bf16),
                                 preferred_element_type=f32))
    for u in windows:
        row = seg_start[u] + earlier[u]
        for k in range(TOP_K):
            dest_ref[k:k + 1, rows[u]] = jnp.sum(jnp.where(sels[u][k], row, 0.0), axis=0,
                                                 keepdims=True).astype(i32)


def _outproj_router(yc, ya, yp, ys, x, gate1, wo_bf, n2g, shift2, scale2, rwT, rb, seq):
    N, D = x.shape
    gw = GROUP_WIDTH
    nw = N // ROUTE_TILE
    wps = ROUTE_WINDOWS_PER_STEP
    W = ROUTE_TILE * wps
    per_b = seq // W
    row = lambda i: (i, 0)
    vec = lambda i: (i // per_b, 0, 0)
    full = lambda i: (0, 0)
    colblk = lambda i: (0, i)
    return pl.pallas_call(
        _outproj_router_kernel,
        grid=(nw // wps,),
        in_specs=[pl.BlockSpec((W, gw), row)] * 4 + [
            pl.BlockSpec((W, D), row),
            pl.BlockSpec((1, 1, D), vec),
            pl.BlockSpec(wo_bf.shape, full),
            pl.BlockSpec((1, D), full),
            pl.BlockSpec((1, 1, D), vec),
            pl.BlockSpec((1, 1, D), vec),
            pl.BlockSpec(rwT.shape, full),
            pl.BlockSpec(rb.shape, full)],
        out_specs=[pl.BlockSpec((W, D), row),
                   pl.BlockSpec((W, D), row),
                   pl.BlockSpec((TOP_K, W), colblk),
                   pl.BlockSpec((TOP_K, W), colblk),
                   pl.BlockSpec((wps, N_EXPERTS, 1), lambda i: (i, 0, 0))],
        out_shape=[jax.ShapeDtypeStruct((N, D), f32),
                   jax.ShapeDtypeStruct((N, D), bf16),
                   jax.ShapeDtypeStruct((TOP_K, N), i32),
                   jax.ShapeDtypeStruct((TOP_K, N), f32),
                   jax.ShapeDtypeStruct((nw, N_EXPERTS, 1), i32)],
        compiler_params=_cparams(("arbitrary",)),
        name="outproj_router",
    )(yc, ya, yp, ys, x, gate1, wo_bf, n2g, shift2, scale2, rwT, rb)


def _max_window_rows():
    return -(-(ROUTE_TILE * TOP_K + N_EXPERTS * (ROW_CHUNK - 1)) // 128) * 128


def _pack_pairs(lo, hi):
    lo_bits = lax.bitcast_convert_type(lo, jnp.uint32)
    hi_bits = lax.bitcast_convert_type(hi, jnp.uint32)
    return lax.bitcast_convert_type(hi_bits | (lo_bits >> 16), i32)


def _unpack_pairs(words):
    bits = lax.bitcast_convert_type(words, jnp.uint32)
    lo = lax.bitcast_convert_type(bits << 16, f32).astype(bf16)
    hi = lax.bitcast_convert_type(bits & jnp.uint32(0xFFFF0000), f32).astype(bf16)
    return lo, hi


def _dispatch_kernel(big_src_s, big_dst_s, wbig_s, small_src_s, small_dst_s, wsmall_s, wchunks_s,
                     tail_start_s, tail_chunks_s, n_used_s,
                     h_ref, dest_ref, xs_ref, sorted_ref, zero_ref, sem):
    w = pl.program_id(0)
    nw = pl.num_programs(0)
    W = ROUTE_TILE
    R = sorted_ref.shape[1]
    half = sorted_ref.shape[2]
    slot = w % DISPATCH_BUFFERS
    for r0 in range(0, R, DISPATCH_ROW_BLOCK):
        rio = r0 + lax.broadcasted_iota(i32, (DISPATCH_ROW_BLOCK, W), 0)
        hit = rio == dest_ref[0:1, :]
        for k in range(1, TOP_K):
            hit = hit | (rio == dest_ref[k:k + 1, :])
        onehot = jnp.where(hit, 1.0, 0.0).astype(bf16)
        rows = jnp.dot(onehot, h_ref[...], preferred_element_type=f32)
        sorted_ref[slot, r0:r0 + DISPATCH_ROW_BLOCK, :] = _pack_pairs(rows[:, :half], rows[:, half:])

    def rows_copy(s, local_row, global_row, rows):
        return pltpu.make_async_copy(
            sorted_ref.at[s, pl.ds(pl.multiple_of(local_row, ROW_CHUNK), rows), :],
            xs_ref.at[pl.ds(pl.multiple_of(global_row, ROW_CHUNK), rows), :], sem.at[s])

    def zero_copy(dst_row):
        return pltpu.make_async_copy(
            zero_ref.at[0:ROW_CHUNK, :],
            xs_ref.at[pl.ds(pl.multiple_of(dst_row, ROW_CHUNK), ROW_CHUNK), :], sem.at[0])

    def zero_tile_copy(tile):
        return pltpu.make_async_copy(
            zero_ref, xs_ref.at[pl.ds(pl.multiple_of(tile * FFN_TILE, FFN_TILE), FFN_TILE), :], sem.at[0])

    def drain(win):
        chunks = wchunks_s[win]

        @pl.when(chunks > 0)
        def _():
            rows = pl.multiple_of(chunks * ROW_CHUNK, ROW_CHUNK)
            pltpu.make_async_copy(xs_ref.at[pl.ds(0, rows), :], xs_ref.at[pl.ds(0, rows), :],
                                  sem.at[win % DISPATCH_BUFFERS]).wait()

    @pl.when(w >= DISPATCH_BUFFERS - 1)
    def _():
        drain(jnp.maximum(w - (DISPATCH_BUFFERS - 1), 0))

    for parity in range(DISPATCH_BUFFERS):
        @pl.when(slot == parity)
        def _():
            def big(q, carry):
                rows_copy(parity, big_dst_s[w * MAX_COMBINE_PIECES + q], big_src_s[w * MAX_COMBINE_PIECES + q],
                          COMBINE_COPY_ROWS).start()
                return carry
            lax.fori_loop(0, wbig_s[w], big, 0)

            def small(q, carry):
                rows_copy(parity, small_dst_s[w * MAX_COMBINE_CHUNKS + q], small_src_s[w * MAX_COMBINE_CHUNKS + q],
                          ROW_CHUNK).start(priority=1)
                return carry
            lax.fori_loop(0, wsmall_s[w], small, 0)

    @pl.when(w == nw - 1)
    def _():
        for back in range(DISPATCH_BUFFERS - 2, -1, -1):
            @pl.when(w - back >= 0)
            def _(back=back):
                drain(jnp.maximum(w - back, 0))
        zero_ref[...] = jnp.zeros(zero_ref.shape, i32)

        def per_tail(e, total):
            n = tail_chunks_s[e]
            dst = tail_start_s[e]

            def issue(j, c):
                zero_copy(dst + j * ROW_CHUNK).start()
                return c
            lax.fori_loop(0, n, issue, 0)
            return total + n
        tails = lax.fori_loop(0, N_EXPERTS, per_tail, 0)

        def drain_tail(j, c):
            zero_copy(0).wait()
            return c
        lax.fori_loop(0, tails, drain_tail, 0)

        n_tiles = xs_ref.shape[0] // FFN_TILE

        def issue_tile(j, c):
            zero_tile_copy(j).start()
            return c
        lax.fori_loop(n_used_s[0], n_tiles, issue_tile, 0)

        def drain_tile(j, c):
            zero_tile_copy(0).wait()
            return c
        lax.fori_loop(n_used_s[0], n_tiles, drain_tile, 0)


def _dispatch(h2, destT, t, p_rows):
    N, D = h2.shape
    W = ROUTE_TILE
    nw = N // W
    R = _max_window_rows()
    grid_spec = pltpu.PrefetchScalarGridSpec(
        num_scalar_prefetch=10,
        grid=(nw,),
        in_specs=[pl.BlockSpec((W, D), lambda w, *_: (w, 0)),
                  pl.BlockSpec((TOP_K, W), lambda w, *_: (0, w))],
        out_specs=pl.BlockSpec(memory_space=pl.ANY),
        scratch_shapes=[pltpu.VMEM((DISPATCH_BUFFERS, R, D // 2), i32), pltpu.VMEM((FFN_TILE, D // 2), i32),
                        pltpu.SemaphoreType.DMA((DISPATCH_BUFFERS,))],
    )
    return pl.pallas_call(
        _dispatch_kernel,
        grid_spec=grid_spec,
        out_shape=jax.ShapeDtypeStruct((p_rows, D // 2), i32),
        compiler_params=_cparams(("arbitrary",)),
        name="expert_dispatch",
    )(t['big_src'], t['big_dst'], t['wbig'], t['small_src'], t['small_dst'], t['wsmall'], t['wchunks'],
      t['tail_start'], t['tail_chunks'], t['n_used'], h2, destT)


def _ffn_kernel(tile_expert_s, next_expert_s, n_used_s, x_ref, w1_hbm, b1_ref, w2_hbm, b2_ref, y_ref,
                w1f_ref, w2f_ref, w1b_ref, w2b_ref, sem, *, layer):
    i = pl.program_id(0)
    last = n_used_s[0] - 1
    expert = tile_expert_s[jnp.minimum(i, last)]
    prev_expert = tile_expert_s[jnp.minimum(jnp.maximum(i - 1, 0), last)]

    def weight_copies(e):
        return (pltpu.make_async_copy(w1_hbm.at[layer, e], w1f_ref, sem.at[0]),
                pltpu.make_async_copy(w2_hbm.at[layer, e], w2f_ref, sem.at[1]))

    @pl.when(i == 0)
    def _():
        for c in weight_copies(expert):
            c.start()

    @pl.when((i == 0) | (expert != prev_expert))
    def _():
        for c in weight_copies(expert):
            c.wait()
        w1b_ref[...] = w1f_ref[...].astype(bf16)
        w2b_ref[...] = w2f_ref[...].astype(bf16)
        nxt = next_expert_s[expert]

        @pl.when(nxt < N_EXPERTS)
        def _():
            for c in weight_copies(nxt):
                c.start()

    @pl.when(i < n_used_s[0])
    def _():
        dff = w2b_ref.shape[0]
        half = x_ref.shape[1]
        sub = x_ref.shape[0] // FFN_SUBTILES
        hidden = []
        for r in range(FFN_SUBTILES):
            x_lo, x_hi = _unpack_pairs(x_ref[r * sub:(r + 1) * sub, :])
            hidden.append(jnp.dot(x_lo, w1b_ref[0:half, :], preferred_element_type=f32)
                          + jnp.dot(x_hi, w1b_ref[half:2 * half, :], preferred_element_type=f32))
        for r in range(FFN_SUBTILES):
            hh = hidden[r] + b1_ref[0, 0]
            x_glu = jnp.minimum(hh[:, :dff], SWIGLU_LIMIT)
            x_lin = jnp.clip(hh[:, dff:], -SWIGLU_LIMIT, SWIGLU_LIMIT)
            act = x_glu * jax.nn.sigmoid(SWIGLU_ALPHA * x_glu) * (x_lin + 1.0)
            y = jnp.dot(act.astype(bf16), w2b_ref[...], preferred_element_type=f32) + b2_ref[0, 0]
            y = y.astype(bf16).astype(f32)
            y_ref[r * sub:(r + 1) * sub, :] = _pack_pairs(y[:, :half], y[:, half:])

    @pl.when(i >= n_used_s[0])
    def _():
        y_ref[...] = jnp.zeros(y_ref.shape, i32)


def _expert_ffn(xs, w1, b1, w2, b2, t, layer):
    P, half = xs.shape
    L, E, D, F2 = w1.shape
    tm = FFN_TILE
    nt = P // tm

    def tile(i, te, ne, nu):
        return (jnp.minimum(i, nu[0] - 1), 0)

    def expert4(i, te, ne, nu):
        return (layer, te[jnp.minimum(i, nu[0] - 1)], 0, 0)

    grid_spec = pltpu.PrefetchScalarGridSpec(
        num_scalar_prefetch=3,
        grid=(nt,),
        in_specs=[pl.BlockSpec((tm, half), tile),
                  pl.BlockSpec(memory_space=pl.ANY),
                  pl.BlockSpec((1, 1, 1, F2), expert4),
                  pl.BlockSpec(memory_space=pl.ANY),
                  pl.BlockSpec((1, 1, 1, D), expert4)],
        out_specs=pl.BlockSpec((tm, half), lambda i, te, ne, nu: (i, 0)),
        scratch_shapes=[pltpu.VMEM((D, F2), f32), pltpu.VMEM((F2 // 2, D), f32),
                        pltpu.VMEM((D, F2), bf16), pltpu.VMEM((F2 // 2, D), bf16),
                        pltpu.SemaphoreType.DMA((2,))],
    )
    return pl.pallas_call(
        functools.partial(_ffn_kernel, layer=layer),
        grid_spec=grid_spec,
        out_shape=jax.ShapeDtypeStruct((P, half), i32),
        compiler_params=_cparams(("arbitrary",)),
        name="expert_ffn",
    )(t['tile_expert'], t['next_expert'], t['n_used'], xs, w1, b1.reshape(L, E, 1, F2), w2,
      b2.reshape(L, E, 1, D))


def _combine_kernel(big_src_s, big_dst_s, wbig_s, small_src_s, small_dst_s, wsmall_s, wchunks_s,
                    ys_ref, dest_ref, gate_ref, x1_ref, g2_ref, fg_ref, o_ref, local_ref, sem, *, final):
    step = pl.program_id(0)
    n_steps = pl.num_programs(0)
    W = ROUTE_TILE
    wps = COMBINE_WINDOWS_PER_STEP
    R = local_ref.shape[2]
    slot = step % 2

    def rows_copy(s, u, src_row, dst_row, rows):
        return pltpu.make_async_copy(
            ys_ref.at[pl.ds(pl.multiple_of(src_row, ROW_CHUNK), rows), :],
            local_ref.at[s, u, pl.ds(pl.multiple_of(dst_row, ROW_CHUNK), rows), :], sem.at[s, u])

    def fetch(at_step, s):
        for u in range(wps):
            win = at_step * wps + u

            def big(q, carry, win=win, u=u):
                rows_copy(s, u, big_src_s[win * MAX_COMBINE_PIECES + q], big_dst_s[win * MAX_COMBINE_PIECES + q],
                          COMBINE_COPY_ROWS).start()
                return carry
            lax.fori_loop(0, wbig_s[win], big, 0)

            def small(q, carry, win=win, u=u):
                rows_copy(s, u, small_src_s[win * MAX_COMBINE_CHUNKS + q],
                          small_dst_s[win * MAX_COMBINE_CHUNKS + q], ROW_CHUNK).start(priority=1)
                return carry
            lax.fori_loop(0, wsmall_s[win], small, 0)

    @pl.when(step == 0)
    def _():
        local_ref[...] = jnp.zeros(local_ref.shape, i32)
        fetch(0, 0)

    for parity in range(2):
        @pl.when((step + 1 < n_steps) & (slot == parity))
        def _():
            fetch(jnp.minimum(step + 1, n_steps - 1), 1 - parity)

    for u in range(wps):
        chunks = wchunks_s[step * wps + u]

        @pl.when(chunks > 0)
        def _(chunks=chunks, u=u):
            rows = pl.multiple_of(chunks * ROW_CHUNK, ROW_CHUNK)
            pltpu.make_async_copy(ys_ref.at[pl.ds(0, rows), :], local_ref.at[slot, u, pl.ds(0, rows), :],
                                  sem.at[slot, u]).wait()

    rio = lax.broadcasted_iota(i32, (R, W), 0)
    tn = (((0,), (0,)), ((), ()))
    weights, halves = [], []
    for u in range(wps):
        cols = slice(u * W, (u + 1) * W)
        wt = jnp.zeros((R, W), f32)
        for k in range(TOP_K):
            wt = jnp.where(rio == dest_ref[k:k + 1, cols], gate_ref[k:k + 1, cols], wt)
        weights.append(wt.astype(bf16))
        halves.append(_unpack_pairs(local_ref[slot, u]))
    for u in range(wps):
        y_lo, y_hi = halves[u]
        moe = jnp.concatenate([lax.dot_general(weights[u], y_lo, tn, preferred_element_type=f32),
                               lax.dot_general(weights[u], y_hi, tn, preferred_element_type=f32)], axis=1)
        rows = slice(u * W, (u + 1) * W)
        x2 = x1_ref[rows, :] + g2_ref[0] * moe
        if final:
            ms = jnp.mean(x2 * x2, axis=-1, keepdims=True)
            x2 = x2 * lax.rsqrt(ms + EPS) * fg_ref[...]
        o_ref[rows, :] = x2


def _combine(ys, dest, gates, x1, gate2, final_g, t, seq, final):
    N, D = x1.shape
    wps = COMBINE_WINDOWS_PER_STEP
    W = ROUTE_TILE * wps
    per_b = seq // W
    R = _max_window_rows()
    grid_spec = pltpu.PrefetchScalarGridSpec(
        num_scalar_prefetch=7,
        grid=(N // W,),
        in_specs=[pl.BlockSpec(memory_space=pl.ANY),
                  pl.BlockSpec((TOP_K, W), lambda w, *_: (0, w)),
                  pl.BlockSpec((TOP_K, W), lambda w, *_: (0, w)),
                  pl.BlockSpec((W, D), lambda w, *_: (w, 0)),
                  pl.BlockSpec((1, 1, D), lambda w, *_: (w // per_b, 0, 0)),
                  pl.BlockSpec((1, D), lambda w, *_: (0, 0))],
        out_specs=pl.BlockSpec((W, D), lambda w, *_: (w, 0)),
        scratch_shapes=[pltpu.VMEM((2, wps, R, D // 2), i32), pltpu.SemaphoreType.DMA((2, wps))],
    )
    return pl.pallas_call(
        functools.partial(_combine_kernel, final=final),
        grid_spec=grid_spec,
        out_shape=jax.ShapeDtypeStruct((N, D), f32),
        compiler_params=_cparams(("arbitrary",)),
        name="expert_combine",
    )(t['big_src'], t['big_dst'], t['wbig'], t['small_src'], t['small_dst'], t['wsmall'], t['wchunks'],
      ys, dest, gates, x1, gate2, final_g)


def _routing_tables(padded, n_tiles):
    nw, E = padded.shape
    lstart = jnp.cumsum(padded, axis=1) - padded
    tot = jnp.sum(padded, axis=0)
    region = (tot + FFN_TILE - 1) // FFN_TILE * FFN_TILE
    region_end = jnp.cumsum(region)
    region_start = region_end - region
    gbase = region_start[None, :] + jnp.cumsum(padded, axis=0) - padded
    nchunk = padded // ROW_CHUNK
    tail_start = region_start + tot
    tail_chunks = (region - tot) // ROW_CHUNK
    n_used = (region_end[-1] // FFN_TILE).astype(i32).reshape(1)
    tile_row = jnp.arange(n_tiles, dtype=i32) * FFN_TILE
    tile_expert = jnp.minimum(
        jnp.sum((region_end[None, :] <= tile_row[:, None]).astype(i32), axis=1), E - 1).astype(i32)
    eidx = jnp.arange(E, dtype=i32)
    later_nonempty = (eidx[None, :] > eidx[:, None]) & (region[None, :] > 0)
    next_expert = jnp.min(jnp.where(later_nonempty, eidx[None, :], E), axis=1)
    flat = lambda a: a.reshape(-1).astype(i32)

    def copy_list(counts, window_rows, expert_rows, step, max_items):
        ends = jnp.cumsum(counts, axis=1)
        slot = jnp.arange(max_items, dtype=i32)
        owner = jnp.sum((ends[:, None, :] <= slot[None, :, None]).astype(i32), axis=2)
        hit = owner[:, :, None] == eidx[None, None, :]
        pick = lambda a: jnp.sum(jnp.where(hit, a[:, None, :], 0), axis=2)
        offset = step * (slot[None, :] - pick(ends - counts))
        return flat(pick(window_rows) + offset), flat(pick(expert_rows) + offset), flat(ends[:, -1])

    nbig = padded // COMBINE_COPY_ROWS
    big_dst, big_src, wbig = copy_list(nbig, lstart, gbase, COMBINE_COPY_ROWS, MAX_COMBINE_PIECES)
    rest = nbig * COMBINE_COPY_ROWS
    small_dst, small_src, wsmall = copy_list(nchunk - nbig * (COMBINE_COPY_ROWS // ROW_CHUNK),
                                             lstart + rest, gbase + rest, ROW_CHUNK, MAX_COMBINE_CHUNKS)
    return dict(next_expert=flat(next_expert), wchunks=flat(jnp.sum(nchunk, axis=1)),
                big_src=big_src, big_dst=big_dst, wbig=wbig,
                small_src=small_src, small_dst=small_dst, wsmall=wsmall,
                tail_start=flat(tail_start), tail_chunks=flat(tail_chunks),
                n_used=n_used, tile_expert=tile_expert)


def _block_diag(w):
    g, a, b = w.shape
    out = jnp.zeros((g * a, g * b), w.dtype)
    for i in range(g):
        out = out.at[i * a:(i + 1) * a, i * b:(i + 1) * b].set(w[i])
    return out


def kernel(x, c, w_mod, b_mod, norm1_g, w_in, conv_w, conv_b, conv_norm_g, conv_norm_b, rel_bias, pool_w, pool_scale, sgu_norm_g, sgu_norm_b, sgu_w, sgu_b, mix_out_g, w_out, norm2_g, router_w, router_b, exp_w1, exp_b1, exp_w2, exp_b2, final_norm_g):
    B, S, D = x.shape
    L = w_mod.shape[0]
    N = B * S
    nw = N // ROUTE_TILE
    assert S % SEQ_TILE == 0 and S % MOBA_BLOCK == 0 and N % ROUTE_TILE == 0 and S % ROUTE_TILE == 0
    p_bound = N * TOP_K + nw * N_EXPERTS * (ROW_CHUNK - 1) + N_EXPERTS * (FFN_TILE - 1)
    n_tiles = -(-p_bound // FFN_TILE)
    p_rows = n_tiles * FFN_TILE

    mod = _modulation(c, w_mod, b_mod)
    bias_tiles = _bias_tiles(rel_bias)
    far_bucket = int(_t5_bucket_table(MOBA_BLOCK + 2)[MOBA_BLOCK + 1])
    assert far_bucket == int(_t5_bucket_table(S + 1)[S])
    far_bias = rel_bias[far_bucket].astype(f32)
    row = lambda a: a.reshape(1, -1)
    for l in range(L):
        m6 = mod[l].reshape(B, 6, 1, D)
        shift1, scale1, gate1, shift2, scale2, gate2 = (m6[:, j] for j in range(6))
        mg = row(mix_out_g[l])
        qT, k, vT, kmean, yc, yp, ys = _project_and_mix(
            x, shift1, scale1, row(norm1_g[l]), w_in[l].astype(bf16),
            conv_w[l], row(conv_b[l]), row(conv_norm_g[l]), row(conv_norm_b[l]),
            _block_diag(pool_w[l]).astype(bf16), row(pool_scale[l]),
            row(sgu_norm_g[l]), row(sgu_norm_b[l]), sgu_w[l],
            jnp.repeat(sgu_b[l].T, HEAD_DIM, axis=1), mg)
        ya = _moba_attention(qT, k, vT, kmean.reshape(B, -1, GROUP_WIDTH), bias_tiles, far_bias, mg)
        flat = lambda a: a.reshape(N, -1)
        x1, h2, destT, gateT, padded = _outproj_router(
            flat(yc), flat(ya), flat(yp), flat(ys), x.reshape(N, D), gate1, w_out[l].astype(bf16),
            row(norm2_g[l]), shift2, scale2, router_w[l].T, router_b[l].reshape(-1, 1), S)
        t = _routing_tables(padded.reshape(nw, N_EXPERTS), n_tiles)
        xs = _dispatch(h2, destT, t, p_rows)
        ysort = _expert_ffn(xs, exp_w1, exp_b1, exp_w2, exp_b2, t, l)
        x = _combine(ysort, destT, gateT, x1, gate2, row(final_norm_g), t, S,
                     final=(l == L - 1)).reshape(B, S, D)
    return x
```
